```python
import math
import jax
import jax.numpy as jnp
from jax import lax
import numpy as np

D_MODEL = 2048
BATCH = 1
SEQ = 8192
DEPTH = 2
DEC_BATCH = 128
DEC_SEQ = 8
PAST_LEN = 2048
PAGE_SIZE = 128

RMS_EPS = 1e-6
N_META = 16
Q_BLOCK = 128
MIX_WIDTH = D_MODEL
S5_WIDTH = MIX_WIDTH // 2
S5_GROUP = 16
S5_GROUPS = S5_WIDTH // S5_GROUP
S5_STATE = 64
S5_DT_MIN = 1e-3
S5_DT_MAX = 1e-1
RWKV_WIDTH = MIX_WIDTH // 2
RWKV_HEAD = 64
RWKV_HEADS = RWKV_WIDTH // RWKV_HEAD
RWKV_DECAY_LORA = 64
RWKV_AAA_LORA = 64
RWKV_GATE_LORA = 256
RWKV_PROJ = 3 * RWKV_WIDTH + RWKV_DECAY_LORA + RWKV_AAA_LORA + RWKV_GATE_LORA
RWKV_LN_EPS = 64e-5
IN0_WIDTH = S5_WIDTH + RWKV_PROJ
SB_WIDTH = MIX_WIDTH // 2
SB_HEAD = 64
SB_HEADS = SB_WIDTH // SB_HEAD
DIFF_WIDTH = MIX_WIDTH // 2
DIFF_QK = 64
DIFF_V = 2 * DIFF_QK
DIFF_HEADS = DIFF_WIDTH // DIFF_V
DIFF_QK_WIDTH = DIFF_HEADS * 2 * DIFF_QK
DIFF_LAYER_NUMBER = 2
DIFF_LAMBDA_INIT = 0.8 - 0.6 * math.exp(-0.3 * (DIFF_LAYER_NUMBER - 1))
IN1_WIDTH = 3 * SB_WIDTH + 2 * DIFF_QK_WIDTH + DIFF_HEADS * DIFF_V
FFN_DENSE = 5632
N_EXPERTS = 8
TOP_K = 2
FFN_EXPERT = 2816
NEG_INF = -1e30

kernel_name = 'hybrid_s5_rwkv7_stickbreak_diffattn_step'


def _rmsnorm(x, gain, eps=RMS_EPS):
    xf = x.astype(jnp.float32)
    y = xf * lax.rsqrt(jnp.mean(xf * xf, axis=-1, keepdims=True) + eps)
    return (y * gain.astype(jnp.float32)).astype(x.dtype)


def _swiglu(h, w1, w3, w2):
    return (jax.nn.silu(h @ w1) * (h @ w3)) @ w2


def _moe(h, router, w1, w3, w2):
    logits = (h @ router).astype(jnp.float32)
    top_val, top_idx = lax.top_k(logits, TOP_K)
    top_w = jax.nn.softmax(top_val, axis=-1)
    gates = jnp.einsum('blk,blke->ble', top_w, jax.nn.one_hot(top_idx, N_EXPERTS, dtype=jnp.float32))
    out = jnp.zeros(h.shape, jnp.float32)
    for e in range(N_EXPERTS):
        out = out + gates[..., e:e + 1] * _swiglu(h, w1[e], w3[e], w2[e]).astype(jnp.float32)
    return out.astype(h.dtype)


def _complex_affine_combine(e1, e2):
    a1r, a1i, b1r, b1i = e1
    a2r, a2i, b2r, b2i = e2
    return (a2r * a1r - a2i * a1i, a2r * a1i + a2i * a1r,
            a2r * b1r - a2i * b1i + b2r, a2r * b1i + a2i * b1r + b2i)


def _s5_mixer(u, h0_re, h0_im, lam_re, lam_im, log_dt, b_re, b_im, c_re, c_im, d_skip, w_glu):
    f32 = jnp.float32
    bsz, L, _ = u.shape
    uf = u.astype(f32).reshape(bsz, L, S5_GROUPS, S5_GROUP)
    lr = lam_re.astype(f32)
    li = lam_im.astype(f32)
    dt = jnp.exp(log_dt.astype(f32))[:, None]
    mag = jnp.exp(lr * dt)
    lb_re = mag * jnp.cos(li * dt)
    lb_im = mag * jnp.sin(li * dt)
    den = lr * lr + li * li
    f_re = ((lb_re - 1.0) * lr + lb_im * li) / den
    f_im = (lb_im * lr - (lb_re - 1.0) * li) / den
    br = b_re.astype(f32)
    bi = b_im.astype(f32)
    bb_re = f_re[..., None] * br - f_im[..., None] * bi
    bb_im = f_re[..., None] * bi + f_im[..., None] * br
    bu_re = jnp.einsum('blgh,gph->blgp', uf, bb_re)
    bu_im = jnp.einsum('blgh,gph->blgp', uf, bb_im)
    h0r = h0_re.astype(f32)
    h0i = h0_im.astype(f32)
    bu_re = bu_re.at[:, 0].add(lb_re * h0r - lb_im * h0i)
    bu_im = bu_im.at[:, 0].add(lb_re * h0i + lb_im * h0r)
    a_re = jnp.broadcast_to(lb_re, bu_re.shape)
    a_im = jnp.broadcast_to(lb_im, bu_im.shape)
    _, _, h_re, h_im = lax.associative_scan(_complex_affine_combine, (a_re, a_im, bu_re, bu_im), axis=1)
    y = (jnp.einsum('blgp,ghp->blgh', h_re, c_re.astype(f32))
         - jnp.einsum('blgp,ghp->blgh', h_im, c_im.astype(f32))
         + d_skip.astype(f32) * uf)
    z = jax.nn.gelu(y.reshape(bsz, L, S5_WIDTH))
    z = z * jax.nn.sigmoid(z @ w_glu.astype(f32))
    return z.astype(u.dtype), h_re[:, -1], h_im[:, -1]


def _rwkv7_mixer(p, shift_prev, s0, mu, w0, w2, a0, a2, g2, k_k, k_a, r_k, ln_w, ln_b):
    f32 = jnp.float32
    bsz, L, _ = p.shape
    pf = p.astype(f32)
    prev = jnp.concatenate([shift_prev.astype(f32)[:, None], pf[:, :-1]], axis=1)
    xm = pf + (prev - pf) * mu.astype(f32)
    r, k, v, w_lo, a_lo, g_lo = jnp.split(
        xm, [RWKV_WIDTH, 2 * RWKV_WIDTH, 3 * RWKV_WIDTH, 3 * RWKV_WIDTH + RWKV_DECAY_LORA,
             3 * RWKV_WIDTH + RWKV_DECAY_LORA + RWKV_AAA_LORA], axis=-1)
    w = -jax.nn.softplus(-(w0.astype(f32) + jnp.tanh(w_lo) @ w2.astype(f32))) - 0.5
    decay = jnp.exp(-jnp.exp(w))
    a = jax.nn.sigmoid(a0.astype(f32) + a_lo @ a2.astype(f32))
    g = jax.nn.sigmoid(g_lo) @ g2.astype(f32)
    kk = (k * k_k.astype(f32)).reshape(bsz, L, RWKV_HEADS, RWKV_HEAD)
    kk = kk / jnp.maximum(jnp.linalg.norm(kk, axis=-1, keepdims=True), 1e-12)
    k = k * (1.0 + (a - 1.0) * k_a.astype(f32))

    def hd(t):
        return t.reshape(bsz, L, RWKV_HEADS, RWKV_HEAD)

    r_h, k_h, v_h, w_h, a_h = hd(r), hd(k), hd(v), hd(decay), hd(a)

    def step(S, inp):
        r_t, w_t, k_t, v_t, kk_t, a_t = inp
        sa = jnp.einsum('bhij,bhj->bhi', S, -kk_t)
        S = (S * w_t[:, :, None, :] + sa[..., None] * (kk_t * a_t)[:, :, None, :]
             + v_t[..., None] * k_t[:, :, None, :])
        return S, jnp.einsum('bhij,bhj->bhi', S, r_t)

    xs = tuple(jnp.moveaxis(t, 1, 0) for t in (r_h, w_h, k_h, v_h, kk, a_h))
    s_last, ys = lax.scan(step, s0.astype(f32), xs)
    y = jnp.moveaxis(ys, 0, 1)
    mean = jnp.mean(y, axis=-1, keepdims=True)
    var = jnp.mean(jnp.square(y - mean), axis=-1, keepdims=True)
    y = ((y - mean) * lax.rsqrt(var + RWKV_LN_EPS) * ln_w.astype(f32).reshape(RWKV_HEADS, RWKV_HEAD)
         + ln_b.astype(f32).reshape(RWKV_HEADS, RWKV_HEAD))
    bonus = jnp.sum(r_h * k_h * r_k.astype(f32).reshape(RWKV_HEADS, RWKV_HEAD), axis=-1, keepdims=True) * v_h
    y = (y + bonus).reshape(bsz, L, RWKV_WIDTH) * g
    return y.astype(p.dtype), s_last, p[:, -1]


def _sweep_queries(fn, q, q_pos):
    L = q.shape[1]
    if L <= Q_BLOCK:
        return fn(q, q_pos)
    pad = (-L) % Q_BLOCK
    qp = jnp.pad(q, [(0, 0), (pad, 0)] + [(0, 0)] * (q.ndim - 2))
    pp = jnp.pad(q_pos, (pad, 0))
    nb = (L + pad) // Q_BLOCK
    qb = jnp.moveaxis(qp.reshape((q.shape[0], nb, Q_BLOCK) + q.shape[2:]), 1, 0)
    pb = pp.reshape(nb, Q_BLOCK)
    out = lax.map(lambda a: fn(a[0], a[1]), (qb, pb))
    out = jnp.moveaxis(out, 0, 1).reshape((q.shape[0], nb * Q_BLOCK) + out.shape[3:])
    return out[:, pad:]


def _stick_breaking(q, k, v, q_pos, k_pos):
    f32 = jnp.float32
    kf = k.astype(f32)
    vf = v.astype(f32)
    scale = SB_HEAD ** -0.5

    def block(qb, pb):
        z = jnp.einsum('bqhd,bkhd->bhqk', qb.astype(f32), kf) * scale
        mask = k_pos[None, :] < pb[:, None]
        log_1mb = jnp.where(mask, jax.nn.log_sigmoid(-z), 0.0)
        after = lax.cumsum(log_1mb, axis=3, reverse=True) - log_1mb
        att = jnp.where(mask, jnp.exp(jax.nn.log_sigmoid(z) + after), 0.0)
        return jnp.einsum('bhqk,bkhd->bqhd', att, vf)

    return _sweep_queries(block, q, q_pos).astype(q.dtype)


def _diff_attention(q, k, v, q_pos, k_pos, lam):
    f32 = jnp.float32
    kf = k.astype(f32)
    vf = v.astype(f32)
    slopes = jnp.exp2(-8.0 * jnp.arange(1, DIFF_HEADS + 1, dtype=f32) / DIFF_HEADS)
    scale = DIFF_QK ** -0.5

    def block(qb, pb):
        s = jnp.einsum('bqhcd,bkhcd->bchqk', qb.astype(f32), kf) * scale
        dist = (pb[:, None] - k_pos[None, :]).astype(f32)
        s = jnp.where(dist >= 0, s - slopes[:, None, None] * dist, NEG_INF)
        pr = jax.nn.softmax(s, axis=-1)
        att = pr[:, 0] - lam * pr[:, 1]
        return jnp.einsum('bhqk,bkhe->bqhe', att, vf)

    return _sweep_queries(block, q, q_pos).astype(q.dtype)


def _attn_mix(h, q_pos, past, w_in1, diff_q_norm, diff_k_norm, lq1, lk1, lq2, lk2, diff_subln, w_out1):
    f32 = jnp.float32
    bsz, L, _ = h.shape
    proj = h @ w_in1
    sq, sk, sv, dq, dk, dv = jnp.split(
        proj, [SB_WIDTH, 2 * SB_WIDTH, 3 * SB_WIDTH, 3 * SB_WIDTH + DIFF_QK_WIDTH,
               3 * SB_WIDTH + 2 * DIFF_QK_WIDTH], axis=-1)
    sq = sq.reshape(bsz, L, SB_HEADS, SB_HEAD)
    sk = sk.reshape(bsz, L, SB_HEADS, SB_HEAD)
    sv = sv.reshape(bsz, L, SB_HEADS, SB_HEAD)
    dq = _rmsnorm(dq.reshape(bsz, L, DIFF_HEADS, 2, DIFF_QK), diff_q_norm)
    dk = _rmsnorm(dk.reshape(bsz, L, DIFF_HEADS, 2, DIFF_QK), diff_k_norm)
    dv = dv.reshape(bsz, L, DIFF_HEADS, DIFF_V)
    if past is None:
        k_pos = q_pos
        sk_all, sv_all, dk_all, dv_all = sk, sv, dk, dv
    else:
        past_len = past[0].shape[1]
        k_pos = jnp.arange(past_len + L, dtype=jnp.int32)
        sk_all = jnp.concatenate([past[0].astype(sk.dtype), sk], axis=1)
        sv_all = jnp.concatenate([past[1].astype(sv.dtype), sv], axis=1)
        dk_all = jnp.concatenate([past[2].astype(dk.dtype), dk], axis=1)
        dv_all = jnp.concatenate([past[3].astype(dv.dtype), dv], axis=1)
    y_sb = _stick_breaking(sq, sk_all, sv_all, q_pos, k_pos).reshape(bsz, L, SB_WIDTH)
    lam = (jnp.exp(jnp.sum(lq1.astype(f32) * lk1.astype(f32)))
           - jnp.exp(jnp.sum(lq2.astype(f32) * lk2.astype(f32))) + DIFF_LAMBDA_INIT)
    y_d = _diff_attention(dq, dk_all, dv_all, q_pos, k_pos, lam)
    y_d = (_rmsnorm(y_d, diff_subln) * (1.0 - DIFF_LAMBDA_INIT)).reshape(bsz, L, DIFF_WIDTH)
    y = jnp.concatenate([y_sb, y_d.astype(y_sb.dtype)], axis=-1) @ w_out1
    return y, sk, sv, dk, dv


def _gather_pages(cache, page_table):
    pages = cache[page_table]
    return pages.reshape((page_table.shape[0], page_table.shape[1] * cache.shape[1]) + cache.shape[2:])


def _trunk(x, pos0, s5_re0, s5_im0, rwkv_s0, rwkv_shift0, past, l0, l1):
    (norm_mix0, w_in0, s5_lambda_re, s5_lambda_im, s5_log_dt, s5_b_re, s5_b_im, s5_c_re, s5_c_im, s5_d,
     s5_w_glu, rwkv_mu, rwkv_w0, rwkv_w2, rwkv_a0, rwkv_a2, rwkv_g2, rwkv_k_k, rwkv_k_a, rwkv_r_k,
     rwkv_ln_w, rwkv_ln_b, w_out0, norm_ffn0, ffn_w1, ffn_w3, ffn_w2) = l0
    (norm_mix1, w_in1, diff_q_norm, diff_k_norm, lq1, lk1, lq2, lk2, diff_subln, w_out1,
     norm_ffn1, moe_router, moe_w1, moe_w3, moe_w2) = l1
    q_pos = pos0 + jnp.arange(x.shape[1], dtype=jnp.int32)
    for layer in range(DEPTH):
        if layer % 2 == 0:
            proj = _rmsnorm(x, norm_mix0) @ w_in0
            ya, s5_re, s5_im = _s5_mixer(proj[..., :S5_WIDTH], s5_re0, s5_im0, s5_lambda_re, s5_lambda_im,
                                         s5_log_dt, s5_b_re, s5_b_im, s5_c_re, s5_c_im, s5_d, s5_w_glu)
            yb, rwkv_s, rwkv_shift = _rwkv7_mixer(proj[..., S5_WIDTH:], rwkv_shift0, rwkv_s0, rwkv_mu, rwkv_w0,
                                                  rwkv_w2, rwkv_a0, rwkv_a2, rwkv_g2, rwkv_k_k, rwkv_k_a,
                                                  rwkv_r_k, rwkv_ln_w, rwkv_ln_b)
            x = x + jnp.concatenate([ya, yb], axis=-1) @ w_out0
            x = x + _swiglu(_rmsnorm(x, norm_ffn0), ffn_w1, ffn_w3, ffn_w2)
        else:
            ym, sb_k, sb_v, d_k, d_v = _attn_mix(_rmsnorm(x, norm_mix1), q_pos, past, w_in1, diff_q_norm,
                                                 diff_k_norm, lq1, lk1, lq2, lk2, diff_subln, w_out1)
            x = x + ym
            x = x + _moe(_rmsnorm(x, norm_ffn1), moe_router, moe_w1, moe_w3, moe_w2)
    return x, s5_re, s5_im, rwkv_s, rwkv_shift, sb_k, sb_v, d_k, d_v


def setup_inputs(seed: int = 0) -> dict:
    key = jax.random.key(seed)
    ks = iter(jax.random.split(key, 80))
    f32 = jnp.float32

    def nrm(shape, scale=1.0):
        return jax.random.normal(next(ks), shape, f32) * scale

    def gain(n):
        return 1.0 + nrm((n,), 0.02)

    n_pages = PAST_LEN // PAGE_SIZE
    n_used = DEC_BATCH * n_pages
    n_pool = n_used + n_used // 4
    page_table = jax.random.permutation(next(ks), n_pool)[:n_used].reshape(DEC_BATCH, n_pages).astype(jnp.int32)
    s5_lambda_im = (jnp.broadcast_to(jnp.pi * jnp.arange(S5_STATE, dtype=f32), (S5_GROUPS, S5_STATE))
                    + nrm((S5_GROUPS, S5_STATE), 0.01))
    s5_log_dt = jax.random.uniform(next(ks), (S5_GROUPS,), f32, math.log(S5_DT_MIN), math.log(S5_DT_MAX))
    rwkv_mu = jax.random.uniform(next(ks), (RWKV_PROJ,), f32, 0.0, 1.0)
    rwkv_w0 = jax.random.uniform(next(ks), (RWKV_WIDTH,), f32, -5.0, 1.0)
    half = 0.5 ** 0.5
    return {
        'x_prompt': nrm((BATCH, SEQ, D_MODEL)),
        'x_sample': nrm((DEC_BATCH, DEC_SEQ, D_MODEL)),
        'state_s5_re': nrm((DEC_BATCH, S5_GROUPS, S5_STATE), 0.5),
        'state_s5_im': nrm((DEC_BATCH, S5_GROUPS, S5_STATE), 0.5),
        'state_rwkv': nrm((DEC_BATCH, RWKV_HEADS, RWKV_HEAD, RWKV_HEAD), 0.3),
        'state_rwkv_shift': nrm((DEC_BATCH, RWKV_PROJ)),
        'cache_sb_k': nrm((n_pool, PAGE_SIZE, SB_HEADS, SB_HEAD)),
        'cache_sb_v': nrm((n_pool, PAGE_SIZE, SB_HEADS, SB_HEAD)),
        'cache_diff_k': nrm((n_pool, PAGE_SIZE, DIFF_HEADS, 2, DIFF_QK)),
        'cache_diff_v': nrm((n_pool, PAGE_SIZE, DIFF_HEADS, DIFF_V)),
        'page_table': page_table,
        'meta_tokens': nrm((N_META, D_MODEL)),
        'norm_mix0': gain(D_MODEL),
        'w_in0': nrm((D_MODEL, IN0_WIDTH), D_MODEL ** -0.5),
        's5_lambda_re': -0.5 + nrm((S5_GROUPS, S5_STATE), 0.01),
        's5_lambda_im': s5_lambda_im,
        's5_log_dt': s5_log_dt,
        's5_b_re': nrm((S5_GROUPS, S5_STATE, S5_GROUP), half * S5_GROUP ** -0.5),
        's5_b_im': nrm((S5_GROUPS, S5_STATE, S5_GROUP), half * S5_GROUP ** -0.5),
        's5_c_re': nrm((S5_GROUPS, S5_GROUP, S5_STATE), half * S5_STATE ** -0.5),
        's5_c_im': nrm((S5_GROUPS, S5_GROUP, S5_STATE), half * S5_STATE ** -0.5),
        's5_d': nrm((S5_GROUPS, S5_GROUP)),
        's5_w_glu': nrm((S5_WIDTH, S5_WIDTH), S5_WIDTH ** -0.5),
        'rwkv_mu': rwkv_mu,
        'rwkv_w0': rwkv_w0,
        'rwkv_w2': nrm((RWKV_DECAY_LORA, RWKV_WIDTH), RWKV_DECAY_LORA ** -0.5),
        'rwkv_a0': nrm((RWKV_WIDTH,), 0.1),
        'rwkv_a2': nrm((RWKV_AAA_LORA, RWKV_WIDTH), RWKV_AAA_LORA ** -0.5),
        'rwkv_g2': nrm((RWKV_GATE_LORA, RWKV_WIDTH), RWKV_GATE_LORA ** -0.5),
        'rwkv_k_k': 1.0 + nrm((RWKV_WIDTH,), 0.1),
        'rwkv_k_a': 1.0 + nrm((RWKV_WIDTH,), 0.1),
        'rwkv_r_k': nrm((RWKV_WIDTH,), 0.1),
        'rwkv_ln_w': gain(RWKV_WIDTH),
        'rwkv_ln_b': nrm((RWKV_WIDTH,), 0.02),
        'w_out0': nrm((MIX_WIDTH, D_MODEL), MIX_WIDTH ** -0.5),
        'norm_ffn0': gain(D_MODEL),
        'ffn_w1': nrm((D_MODEL, FFN_DENSE), D_MODEL ** -0.5),
        'ffn_w3': nrm((D_MODEL, FFN_DENSE), D_MODEL ** -0.5),
        'ffn_w2': nrm((FFN_DENSE, D_MODEL), FFN_DENSE ** -0.5),
        'norm_mix1': gain(D_MODEL),
        'w_in1': nrm((D_MODEL, IN1_WIDTH), D_MODEL ** -0.5),
        'diff_q_norm': gain(DIFF_QK),
        'diff_k_norm': gain(DIFF_QK),
        'diff_lambda_q1': nrm((DIFF_QK,), 0.1),
        'diff_lambda_k1': nrm((DIFF_QK,), 0.1),
        'diff_lambda_q2': nrm((DIFF_QK,), 0.1),
        'diff_lambda_k2': nrm((DIFF_QK,), 0.1),
        'diff_subln': gain(DIFF_V),
        'w_out1': nrm((MIX_WIDTH, D_MODEL), MIX_WIDTH ** -0.5),
        'norm_ffn1': gain(D_MODEL),
        'moe_router': nrm((D_MODEL, N_EXPERTS), D_MODEL ** -0.5),
        'moe_w1': nrm((N_EXPERTS, D_MODEL, FFN_EXPERT), D_MODEL ** -0.5),
        'moe_w3': nrm((N_EXPERTS, D_MODEL, FFN_EXPERT), D_MODEL ** -0.5),
        'moe_w2': nrm((N_EXPERTS, FFN_EXPERT, D_MODEL), FFN_EXPERT ** -0.5),
    }


def reference(x_prompt, x_sample, state_s5_re, state_s5_im, state_rwkv, state_rwkv_shift,
              cache_sb_k, cache_sb_v, cache_diff_k, cache_diff_v, page_table, meta_tokens,
              norm_mix0, w_in0, s5_lambda_re, s5_lambda_im, s5_log_dt, s5_b_re, s5_b_im, s5_c_re, s5_c_im,
              s5_d, s5_w_glu, rwkv_mu, rwkv_w0, rwkv_w2, rwkv_a0, rwkv_a2, rwkv_g2, rwkv_k_k, rwkv_k_a,
              rwkv_r_k, rwkv_ln_w, rwkv_ln_b, w_out0, norm_ffn0, ffn_w1, ffn_w3, ffn_w2,
              norm_mix1, w_in1, diff_q_norm, diff_k_norm, diff_lambda_q1, diff_lambda_k1, diff_lambda_q2,
              diff_lambda_k2, diff_subln, w_out1, norm_ffn1, moe_router, moe_w1, moe_w3, moe_w2):
    l0 = (norm_mix0, w_in0, s5_lambda_re, s5_lambda_im, s5_log_dt, s5_b_re, s5_b_im, s5_c_re, s5_c_im, s5_d,
          s5_w_glu, rwkv_mu, rwkv_w0, rwkv_w2, rwkv_a0, rwkv_a2, rwkv_g2, rwkv_k_k, rwkv_k_a, rwkv_r_k,
          rwkv_ln_w, rwkv_ln_b, w_out0, norm_ffn0, ffn_w1, ffn_w3, ffn_w2)
    l1 = (norm_mix1, w_in1, diff_q_norm, diff_k_norm, diff_lambda_q1, diff_lambda_k1, diff_lambda_q2,
          diff_lambda_k2, diff_subln, w_out1, norm_ffn1, moe_router, moe_w1, moe_w3, moe_w2)

    bp = x_prompt.shape[0]
    meta = jnp.broadcast_to(meta_tokens.astype(x_prompt.dtype)[None], (bp, N_META, D_MODEL))
    xp = jnp.concatenate([meta, x_prompt], axis=1)
    z_s5 = jnp.zeros((bp, S5_GROUPS, S5_STATE), jnp.float32)
    z_rwkv = jnp.zeros((bp, RWKV_HEADS, RWKV_HEAD, RWKV_HEAD), jnp.float32)
    z_shift = jnp.zeros((bp, RWKV_PROJ), x_prompt.dtype)
    (hp, p_s5_re, p_s5_im, p_rwkv, p_rwkv_shift, p_sb_k, p_sb_v, p_diff_k, p_diff_v) = _trunk(
        xp, 0, z_s5, z_s5, z_rwkv, z_shift, None, l0, l1)
    y_prompt = hp[:, N_META:]

    past = (_gather_pages(cache_sb_k, page_table), _gather_pages(cache_sb_v, page_table),
            _gather_pages(cache_diff_k, page_table), _gather_pages(cache_diff_v, page_table))
    past_len = page_table.shape[1] * cache_sb_k.shape[1]
    (y_sample, s_s5_re, s_s5_im, s_rwkv, s_rwkv_shift, s_sb_k, s_sb_v, s_diff_k, s_diff_v) = _trunk(
        x_sample, past_len, state_s5_re, state_s5_im, state_rwkv, state_rwkv_shift, past, l0, l1)

    return (y_prompt, y_sample,
            p_s5_re, p_s5_im, p_rwkv, p_rwkv_shift, p_sb_k, p_sb_v, p_diff_k, p_diff_v,
            s_s5_re, s_s5_im, s_rwkv, s_rwkv_shift, s_sb_k, s_sb_v, s_diff_k, s_diff_v)
```

```python
import functools
import math

import jax
import jax.numpy as jnp
from jax import lax
from jax.experimental import pallas as pl
from jax.experimental.pallas import tpu as pltpu

F32 = jnp.float32
BF16 = jnp.bfloat16
HIGHEST = lax.Precision.HIGHEST

LANE = 128
HEAD64 = 64
RMS_EPS = 1e-6
RWKV_LN_EPS = 64e-5
N_META = 16
NEG_INF = -1e30
VMEM_LIMIT = 56 * 1024 * 1024


def _cp(sem, vmem=VMEM_LIMIT):
    return pltpu.CompilerParams(dimension_semantics=sem, vmem_limit_bytes=vmem)


def _dot(a, b, precision=None):
    return jnp.dot(a, b, preferred_element_type=F32, precision=precision)


def _dot_nt(a, b, precision=None):
    return lax.dot_general(a, b, (((1,), (1,)), ((), ())), preferred_element_type=F32,
                           precision=precision)


def _dot_tn(a, b, precision=None):
    return lax.dot_general(a, b, (((0,), (0,)), ((), ())), preferred_element_type=F32,
                           precision=precision)


def _rms_rows(x, gain):
    ms = jnp.mean(x * x, axis=-1, keepdims=True)
    return x * lax.rsqrt(ms + RMS_EPS) * gain


def _softplus(t):
    return jnp.maximum(t, 0.0) + jnp.log1p(jnp.exp(-jnp.abs(t)))


def _segsum64(x, e128):
    cols = [_dot(x[:, c * LANE:(c + 1) * LANE], e128, HIGHEST) for c in range(x.shape[1] // LANE)]
    return cols[0] if len(cols) == 1 else jnp.concatenate(cols, axis=1)


def _seg_ones():
    r = lax.broadcasted_iota(jnp.int32, (LANE, LANE), 0) // HEAD64
    c = lax.broadcasted_iota(jnp.int32, (LANE, LANE), 1) // HEAD64
    return (r == c).astype(F32)


def _proj_kernel(x_ref, g_ref, w_ref, o_ref, xn_ref, *, split):
    @pl.when(pl.program_id(1) == 0)
    def _():
        xn_ref[...] = _rms_rows(x_ref[...], g_ref[...]).astype(BF16)

    acc = _dot(xn_ref[...], w_ref[...])
    if split:
        for c in range(acc.shape[1] // LANE):
            o_ref[c] = acc[:, c * LANE:(c + 1) * LANE]
    else:
        o_ref[...] = acc


def _proj(x, gain, w, tm, tn, split=False):
    m, d = x.shape
    n = w.shape[1]
    if split:
        out_shape = jax.ShapeDtypeStruct((n // LANE, m, LANE), F32)
        out_spec = pl.BlockSpec((tn // LANE, tm, LANE), lambda i, j: (j, i, 0))
    else:
        out_shape = jax.ShapeDtypeStruct((m, n), F32)
        out_spec = pl.BlockSpec((tm, tn), lambda i, j: (i, j))
    return pl.pallas_call(
        functools.partial(_proj_kernel, split=split),
        grid=(m // tm, n // tn),
        in_specs=[pl.BlockSpec((tm, d), lambda i, j: (i, 0)),
                  pl.BlockSpec((1, d), lambda i, j: (0, 0)),
                  pl.BlockSpec((d, tn), lambda i, j: (0, j))],
        out_specs=out_spec,
        out_shape=out_shape,
        scratch_shapes=[pltpu.VMEM((tm, d), BF16)],
        compiler_params=_cp(("parallel", "arbitrary")),
        name="proj",
    )(x, gain.reshape(1, d), w)


def _outproj_kernel(res_ref, a_ref, b_ref, wa_ref, wb_ref, o_ref):
    o_ref[...] = (res_ref[...] + _dot(a_ref[...].astype(BF16), wa_ref[...])
                  + _dot(b_ref[...].astype(BF16), wb_ref[...]))


def _outproj(res, a, b, wa, wb, tm, tn):
    m, n = res.shape
    ka, kb = a.shape[1], b.shape[1]
    return pl.pallas_call(
        _outproj_kernel,
        grid=(m // tm, n // tn),
        in_specs=[pl.BlockSpec((tm, tn), lambda i, j: (i, j)),
                  pl.BlockSpec((tm, ka), lambda i, j: (i, 0)),
                  pl.BlockSpec((tm, kb), lambda i, j: (i, 0)),
                  pl.BlockSpec((ka, tn), lambda i, j: (0, j)),
                  pl.BlockSpec((kb, tn), lambda i, j: (0, j))],
        out_specs=pl.BlockSpec((tm, tn), lambda i, j: (i, j)),
        out_shape=jax.ShapeDtypeStruct((m, n), F32),
        compiler_params=_cp(("parallel", "arbitrary")),
        name="outproj",
    )(res, a, b, wa, wb)


def _ffn_kernel(x_ref, g_ref, w1_ref, w3_ref, w2_ref, o_ref, xn_ref):
    @pl.when(pl.program_id(1) == 0)
    def _():
        x = x_ref[...]
        xn_ref[...] = _rms_rows(x, g_ref[...]).astype(BF16)
        o_ref[...] = x

    xn = xn_ref[...]
    h1 = _dot(xn, w1_ref[...])
    h3 = _dot(xn, w3_ref[...])
    h = (h1 * jax.nn.sigmoid(h1) * h3).astype(BF16)
    o_ref[...] += _dot(h, w2_ref[...])


def _ffn(x, gain, w1, w3, w2, tm, tf):
    m, d = x.shape
    f = w1.shape[1]
    return pl.pallas_call(
        _ffn_kernel,
        grid=(m // tm, f // tf),
        in_specs=[pl.BlockSpec((tm, d), lambda i, j: (i, 0)),
                  pl.BlockSpec((1, d), lambda i, j: (0, 0)),
                  pl.BlockSpec((d, tf), lambda i, j: (0, j)),
                  pl.BlockSpec((d, tf), lambda i, j: (0, j)),
                  pl.BlockSpec((tf, d), lambda i, j: (j, 0))],
        out_specs=pl.BlockSpec((tm, d), lambda i, j: (i, 0)),
        out_shape=jax.ShapeDtypeStruct((m, d), F32),
        scratch_shapes=[pltpu.VMEM((tm, d), BF16)],
        compiler_params=_cp(("parallel", "arbitrary")),
        name="ffn",
    )(x, gain.reshape(1, d), w1, w3, w2)


def _moe_kernel(x_ref, g_ref, r_ref, w1_ref, w3_ref, w2_ref, o_ref, xn_ref, gate_ref, *, n_experts):
    e = pl.program_id(1)
    j = pl.program_id(2)
    tm = x_ref.shape[0]
    lane = lax.broadcasted_iota(jnp.int32, (tm, LANE), 1)

    @pl.when((e == 0) & (j == 0))
    def _():
        x = x_ref[...]
        xn = _rms_rows(x, g_ref[...])
        xn_ref[...] = xn.astype(BF16)
        o_ref[...] = x
        logits = jnp.where(lane < n_experts, _dot(xn, r_ref[...], HIGHEST), NEG_INF)
        v1 = jnp.max(logits, axis=-1, keepdims=True)
        i1 = jnp.min(jnp.where(logits == v1, lane, LANE), axis=-1, keepdims=True)
        rest = jnp.where(lane == i1, NEG_INF, logits)
        v2 = jnp.max(rest, axis=-1, keepdims=True)
        i2 = jnp.min(jnp.where(rest == v2, lane, LANE), axis=-1, keepdims=True)
        ex = jnp.exp(v2 - v1)
        den = 1.0 + ex
        gate_ref[...] = jnp.where(lane == i1, 1.0 / den, 0.0) + jnp.where(lane == i2, ex / den, 0.0)

    ge = jnp.sum(jnp.where(lane == e, gate_ref[...], 0.0), axis=-1, keepdims=True)
    xn = xn_ref[...]
    h1 = _dot(xn, w1_ref[0])
    h3 = _dot(xn, w3_ref[0])
    h = (h1 * jax.nn.sigmoid(h1) * h3).astype(BF16)
    o_ref[...] += ge * _dot(h, w2_ref[0])


def _moe(x, gain, router, w1, w3, w2, tm, tf):
    m, d = x.shape
    ne, _, f = w1.shape
    router_p = jnp.zeros((d, LANE), F32).at[:, :ne].set(router.astype(F32))
    return pl.pallas_call(
        functools.partial(_moe_kernel, n_experts=ne),
        grid=(m // tm, ne, f // tf),
        in_specs=[pl.BlockSpec((tm, d), lambda i, e, j: (i, 0)),
                  pl.BlockSpec((1, d), lambda i, e, j: (0, 0)),
                  pl.BlockSpec((d, LANE), lambda i, e, j: (0, 0)),
                  pl.BlockSpec((1, d, tf), lambda i, e, j: (e, 0, j)),
                  pl.BlockSpec((1, d, tf), lambda i, e, j: (e, 0, j)),
                  pl.BlockSpec((1, tf, d), lambda i, e, j: (e, j, 0))],
        out_specs=pl.BlockSpec((tm, d), lambda i, e, j: (i, 0)),
        out_shape=jax.ShapeDtypeStruct((m, d), F32),
        scratch_shapes=[pltpu.VMEM((tm, d), BF16), pltpu.VMEM((tm, LANE), F32)],
        compiler_params=_cp(("parallel", "arbitrary", "arbitrary")),
        name="moe",
    )(x, gain.reshape(1, d), router_p, w1, w3, w2)


def _s5_disc_kernel(lr_ref, li_ref, ldt_ref, bre_ref, bim_ref, lbre_ref, lbim_ref, bbre_ref, bbim_ref):
    lr = lr_ref[...]
    li = li_ref[...]
    dt = jnp.exp(ldt_ref[...])
    mag = jnp.exp(lr * dt)
    lb_re = mag * jnp.cos(li * dt)
    lb_im = mag * jnp.sin(li * dt)
    den = lr * lr + li * li
    f_re = ((lb_re - 1.0) * lr + lb_im * li) / den
    f_im = (lb_im * lr - (lb_re - 1.0) * li) / den
    lbre_ref[...] = lb_re
    lbim_ref[...] = lb_im
    for h in range(bre_ref.shape[0]):
        br = bre_ref[h]
        bi = bim_ref[h]
        bbre_ref[h] = f_re * br - f_im * bi
        bbim_ref[h] = f_re * bi + f_im * br


def _s5_discretize(lam_re, lam_im, log_dt, b_re, b_im):
    g, p, h = b_re.shape
    sds = jax.ShapeDtypeStruct
    return pl.pallas_call(
        _s5_disc_kernel,
        out_shape=(sds((g, p), F32), sds((g, p), F32), sds((h, g, p), F32), sds((h, g, p), F32)),
        name="s5_disc",
    )(lam_re.astype(F32), lam_im.astype(F32), log_dt.astype(F32).reshape(g, 1),
      jnp.transpose(b_re.astype(F32), (2, 0, 1)), jnp.transpose(b_im.astype(F32), (2, 0, 1)))


def _s5_b_kernel(u_ref, wre_ref, wim_ref, ore_ref, oim_ref):
    u = u_ref[...].astype(BF16)
    ore_ref[...] = _dot(u, wre_ref[0])
    oim_ref[...] = _dot(u, wim_ref[0])


def _s5_b(u, wre, wim, tm):
    m = u.shape[0]
    nc, kc, ncol = wre.shape
    spec_o = pl.BlockSpec((tm, ncol), lambda i, c: (i, c))
    spec_w = pl.BlockSpec((1, kc, ncol), lambda i, c: (c, 0, 0))
    sds = jax.ShapeDtypeStruct((m, nc * ncol), F32)
    return pl.pallas_call(
        _s5_b_kernel,
        grid=(m // tm, nc),
        in_specs=[pl.BlockSpec((tm, kc), lambda i, c: (i, c)), spec_w, spec_w],
        out_specs=(spec_o, spec_o),
        out_shape=(sds, sds),
        compiler_params=_cp(("parallel", "arbitrary")),
        name="s5_b",
    )(u, wre, wim)


def _s5_scan_kernel(bre_ref, bim_ref, h0re_ref, h0im_ref, lre_ref, lim_ref,
                    hre_ref, him_ref, fre_ref, fim_ref, cre_ref, cim_ref):
    nb, tlen = bre_ref.shape[0], bre_ref.shape[1]

    @pl.when(pl.program_id(1) == 0)
    def _():
        cre_ref[...] = h0re_ref[...]
        cim_ref[...] = h0im_ref[...]

    lre = lre_ref[...]
    lim = lim_ref[...]

    def per_seq(b, _):
        def step(t, carry):
            hr, hi = carry
            nr = lre * hr - lim * hi + bre_ref[b, t]
            ni = lre * hi + lim * hr + bim_ref[b, t]
            hre_ref[b, t] = nr
            him_ref[b, t] = ni
            return nr, ni

        hr, hi = lax.fori_loop(0, tlen, step, (cre_ref[b], cim_ref[b]), unroll=8)
        cre_ref[b] = hr
        cim_ref[b] = hi
        return 0

    lax.fori_loop(0, nb, per_seq, 0)
    fre_ref[...] = cre_ref[...]
    fim_ref[...] = cim_ref[...]


def _s5_scan(bu_re, bu_im, h0_re, h0_im, lb_re, lb_im, nb, tb):
    bsz, length, width = bu_re.shape
    sub = width // LANE
    v4 = lambda a: a.reshape(bsz, length, sub, LANE)
    v3 = lambda a: a.astype(F32).reshape(bsz, sub, LANE)
    spec_x = pl.BlockSpec((nb, tb, sub, LANE), lambda b, t: (b, t, 0, 0))
    spec_h = pl.BlockSpec((nb, sub, LANE), lambda b, t: (b, 0, 0))
    spec_l = pl.BlockSpec((sub, LANE), lambda b, t: (0, 0))
    sds_x = jax.ShapeDtypeStruct((bsz, length, sub, LANE), F32)
    sds_h = jax.ShapeDtypeStruct((bsz, sub, LANE), F32)
    hre, him, fre, fim = pl.pallas_call(
        _s5_scan_kernel,
        grid=(bsz // nb, length // tb),
        in_specs=[spec_x, spec_x, spec_h, spec_h, spec_l, spec_l],
        out_specs=(spec_x, spec_x, spec_h, spec_h),
        out_shape=(sds_x, sds_x, sds_h, sds_h),
        scratch_shapes=[pltpu.VMEM((nb, sub, LANE), F32), pltpu.VMEM((nb, sub, LANE), F32)],
        compiler_params=_cp(("parallel", "arbitrary")),
        name="s5_scan",
    )(v4(bu_re), v4(bu_im), v3(h0_re), v3(h0_im), lb_re.reshape(sub, LANE), lb_im.reshape(sub, LANE))
    return hre.reshape(bsz, length, width), him.reshape(bsz, length, width), fre, fim


def _s5_c_kernel(hre_ref, him_ref, u_ref, wcre_ref, wcim_ref, d_ref, wglu_ref, o_ref):
    nc, kc, _ = wcre_ref.shape
    cols = []
    for c in range(nc):
        hr = hre_ref[:, c * kc:(c + 1) * kc].astype(BF16)
        hi = him_ref[:, c * kc:(c + 1) * kc].astype(BF16)
        cols.append(_dot(hr, wcre_ref[c]) - _dot(hi, wcim_ref[c]))
    y = jnp.concatenate(cols, axis=1) + d_ref[...] * u_ref[...]
    z = jax.nn.gelu(y)
    o_ref[...] = z * jax.nn.sigmoid(_dot(z.astype(BF16), wglu_ref[...]))


def _s5_c(h_re, h_im, u, wcre, wcim, d_skip, w_glu, tm):
    m, width = h_re.shape
    w = u.shape[1]
    full3 = lambda a: pl.BlockSpec(a.shape, lambda i: (0, 0, 0))
    return pl.pallas_call(
        _s5_c_kernel,
        grid=(m // tm,),
        in_specs=[pl.BlockSpec((tm, width), lambda i: (i, 0)),
                  pl.BlockSpec((tm, width), lambda i: (i, 0)),
                  pl.BlockSpec((tm, w), lambda i: (i, 0)),
                  full3(wcre), full3(wcim),
                  pl.BlockSpec((1, w), lambda i: (0, 0)),
                  pl.BlockSpec((w, w), lambda i: (0, 0))],
        out_specs=pl.BlockSpec((tm, w), lambda i: (i, 0)),
        out_shape=jax.ShapeDtypeStruct((m, w), F32),
        compiler_params=_cp(("parallel",)),
        name="s5_c",
    )(h_re, h_im, u, wcre, wcim, d_skip.astype(F32).reshape(1, w), w_glu)


def _blockdiag(w, nblk):
    g, a, b = w.shape
    w4 = w.reshape(g // nblk, nblk, a, b)
    eye = jnp.eye(nblk, dtype=w.dtype)
    return jnp.einsum('cgab,gh->cgahb', w4, eye).reshape(g // nblk, nblk * a, nblk * b)


def _rwkv_pre_kernel(p_ref, prev_ref, mu_ref, w0_ref, w2_ref, a0_ref, a2_ref, g2_ref, kk_ref, ka_ref,
                     r_o, lw_o, k_o, v_o, kk_o, kka_o, g_o, *, width, lora_w, lora_a):
    nb, tlen, c = p_ref.shape
    p = p_ref[...]
    tidx = lax.broadcasted_iota(jnp.int32, (nb, tlen, c), 1)
    prev = jnp.where(tidx == 0, prev_ref[...], pltpu.roll(p, 1, axis=1))
    xm = (p + (prev - p) * mu_ref[...]).reshape(nb * tlen, c)
    r = xm[:, :width]
    k = xm[:, width:2 * width]
    v = xm[:, 2 * width:3 * width]
    o = 3 * width
    w_lo = xm[:, o:o + lora_w]
    a_lo = xm[:, o + lora_w:o + lora_w + lora_a]
    g_lo = xm[:, o + lora_w + lora_a:]
    wraw = -_softplus(-(w0_ref[...] + _dot(jnp.tanh(w_lo).astype(BF16), w2_ref[...]))) - 0.5
    a = jax.nn.sigmoid(a0_ref[...] + _dot(a_lo.astype(BF16), a2_ref[...]))
    g = _dot(jax.nn.sigmoid(g_lo).astype(BF16), g2_ref[...])
    kkr = k * kk_ref[...]
    nrm = jnp.sqrt(_segsum64(kkr * kkr, _seg_ones()))
    kk = kkr / jnp.maximum(nrm, 1e-12)
    r_o[...] = r
    lw_o[...] = -jnp.exp(wraw)
    k_o[...] = k * (1.0 + (a - 1.0) * ka_ref[...])
    v_o[...] = v
    kk_o[...] = kk
    kka_o[...] = kk * a
    g_o[...] = g


def _rwkv_pre(p3, prev, mu, w0, w2, a0, a2, g2, k_k, k_a, nb):
    nbt, tlen, c = p3.shape
    width = w0.shape[0]
    lora_w, lora_a = w2.shape[0], a2.shape[0]
    rows = nb * tlen
    row1 = lambda a: a.astype(F32).reshape(1, -1)
    full2 = lambda a: pl.BlockSpec(a.shape, lambda i: (0, 0))
    spec_o = pl.BlockSpec((rows, width), lambda i: (i, 0))
    sds = jax.ShapeDtypeStruct((nbt * tlen, width), F32)
    args = (mu.astype(F32).reshape(1, 1, c), row1(w0), w2, row1(a0), a2, g2, row1(k_k), row1(k_a))
    return pl.pallas_call(
        functools.partial(_rwkv_pre_kernel, width=width, lora_w=lora_w, lora_a=lora_a),
        grid=(nbt // nb,),
        in_specs=[pl.BlockSpec((nb, tlen, c), lambda i: (i, 0, 0)),
                  pl.BlockSpec((nb, 1, c), lambda i: (i, 0, 0)),
                  pl.BlockSpec((1, 1, c), lambda i: (0, 0, 0))] + [full2(a) for a in args[1:]],
        out_specs=(spec_o,) * 7,
        out_shape=(sds,) * 7,
        compiler_params=_cp(("parallel",)),
        name="rwkv_pre",
    )(p3, prev, *args)


def _rwkv_scan_kernel(r_ref, lw_ref, k_ref, v_ref, kk_ref, kka_ref, s0_ref, y_ref, sl_ref, s_ref, *, tc):
    tb = r_ref.shape[1]
    npair = r_ref.shape[2] // LANE
    n2 = 2 * tc
    lane = lax.broadcasted_iota(jnp.int32, (tc, LANE), 1)
    first = lane < HEAD64
    ri = lax.broadcasted_iota(jnp.int32, (n2, n2), 0)
    ci = lax.broadcasted_iota(jnp.int32, (n2, n2), 1)
    same = (ri // tc) == (ci // tc)
    strict = same & ((ci % tc) < (ri % tc))
    incl = same & ((ci % tc) <= (ri % tc))
    eye = (ri == ci).astype(F32)
    tri = (lax.broadcasted_iota(jnp.int32, (tc, tc), 1)
           <= lax.broadcasted_iota(jnp.int32, (tc, tc), 0)).astype(F32)
    n_sq = max(int(math.ceil(math.log2(tc))) - 1, 0)
    blk_r = lax.broadcasted_iota(jnp.int32, (LANE, LANE), 0) // HEAD64
    blk_c = lax.broadcasted_iota(jnp.int32, (LANE, LANE), 1) // HEAD64

    def stack(x):
        return jnp.concatenate([jnp.where(first, x, 0.0), jnp.where(first, 0.0, x)], axis=0)

    @pl.when(pl.program_id(1) == 0)
    def _():
        for pr in range(npair):
            sa = jnp.concatenate([s0_ref[0, 2 * pr], jnp.zeros((HEAD64, HEAD64), F32)], axis=1)
            sb = jnp.concatenate([jnp.zeros((HEAD64, HEAD64), F32), s0_ref[0, 2 * pr + 1]], axis=1)
            s_ref[pr] = jnp.concatenate([sa, sb], axis=0)

    def chunk(ci_, _):
        rows = pl.ds(pl.multiple_of(ci_ * tc, tc), tc)
        for pr in range(npair):
            lanes = slice(pr * LANE, (pr + 1) * LANE)
            lw = lw_ref[0, rows, lanes]
            c = _dot(tri, lw, HIGHEST)
            c_last = c[tc - 1:tc, :]
            kk = kk_ref[0, rows, lanes]
            kka = kka_ref[0, rows, lanes]
            kx = k_ref[0, rows, lanes]
            e_neg = jnp.exp(-c)
            e_end = jnp.exp(c_last - c)
            al = stack(-kk * jnp.exp(c - lw))
            rt = stack(r_ref[0, rows, lanes] * jnp.exp(c))
            bt = stack(kka * e_neg)
            kt = stack(kx * e_neg)
            bh = stack(kka * e_end)
            kh = stack(kx * e_end)
            vs = stack(v_ref[0, rows, lanes])
            nab = jnp.where(strict, _dot_nt(al, bt, HIGHEST), 0.0)
            aak = jnp.where(strict, _dot_nt(al, kt, HIGHEST), 0.0)
            arb = jnp.where(incl, _dot_nt(rt, bt, HIGHEST), 0.0)
            ark = jnp.where(incl, _dot_nt(rt, kt, HIGHEST), 0.0)
            inv = eye + nab
            pw = nab
            for _ in range(n_sq):
                pw = _dot(pw, pw, HIGHEST)
                inv = inv + _dot(inv, pw, HIGHEST)
            s0 = s_ref[pr]
            u = _dot(inv, _dot_nt(al, s0, HIGHEST) + _dot(aak, vs, HIGHEST), HIGHEST)
            ys = _dot_nt(rt, s0, HIGHEST) + _dot(arb, u, HIGHEST) + _dot(ark, vs, HIGHEST)
            y_ref[0, rows, lanes] = ys[:tc] + ys[tc:]
            s_ref[pr] = (s0 * jnp.exp(c_last) + _dot_tn(u, bh, HIGHEST) + _dot_tn(vs, kh, HIGHEST))
        return 0

    lax.fori_loop(0, tb // tc, chunk, 0)

    for pr in range(npair):
        s = s_ref[pr]
        sl_ref[0, 2 * pr] = s[:HEAD64, :HEAD64]
        sl_ref[0, 2 * pr + 1] = s[HEAD64:, HEAD64:]


def _rwkv_scan(r, lw, k, v, kk, kka, s0, tb, tc):
    bsz, length, width = r.shape
    spec_x = pl.BlockSpec((1, tb, width), lambda b, t: (b, t, 0))
    spec_s = pl.BlockSpec((1,) + s0.shape[1:], lambda b, t: (b, 0, 0, 0))
    return pl.pallas_call(
        functools.partial(_rwkv_scan_kernel, tc=tc),
        grid=(bsz, length // tb),
        in_specs=[spec_x] * 6 + [spec_s],
        out_specs=(spec_x, spec_s),
        out_shape=(jax.ShapeDtypeStruct((bsz, length, width), F32),
                   jax.ShapeDtypeStruct(s0.shape, F32)),
        scratch_shapes=[pltpu.VMEM((width // LANE, LANE, LANE), F32)],
        compiler_params=_cp(("parallel", "arbitrary")),
        name="rwkv_scan",
    )(r, lw, k, v, kk, kka, s0.astype(F32))


def _rwkv_post_kernel(y_ref, r_ref, k_ref, v_ref, g_ref, lnw_ref, lnb_ref, rk_ref, o_ref):
    e128 = _seg_ones()
    y = y_ref[...]
    mean = _segsum64(y, e128) * (1.0 / HEAD64)
    d = y - mean
    var = _segsum64(d * d, e128) * (1.0 / HEAD64)
    yn = d * lax.rsqrt(var + RWKV_LN_EPS) * lnw_ref[...] + lnb_ref[...]
    bonus = _segsum64(r_ref[...] * k_ref[...] * rk_ref[...], e128) * v_ref[...]
    o_ref[...] = (yn + bonus) * g_ref[...]


def _rwkv_post(y, r, k, v, g, ln_w, ln_b, r_k, tm):
    m, w = y.shape
    spec_x = pl.BlockSpec((tm, w), lambda i: (i, 0))
    spec_p = pl.BlockSpec((1, w), lambda i: (0, 0))
    row1 = lambda a: a.astype(F32).reshape(1, w)
    return pl.pallas_call(
        _rwkv_post_kernel,
        grid=(m // tm,),
        in_specs=[spec_x] * 5 + [spec_p] * 3,
        out_specs=spec_x,
        out_shape=jax.ShapeDtypeStruct((m, w), F32),
        compiler_params=_cp(("parallel",)),
        name="rwkv_post",
    )(y, r, k, v, g, row1(ln_w), row1(ln_b), row1(r_k))


def _qknorm_kernel(x_ref, g_ref, o_ref):
    x = x_ref[...]
    x2 = x.reshape(-1, x.shape[-1])
    ms = _segsum64(x2 * x2, _seg_ones()) * (1.0 / HEAD64)
    o_ref[...] = (x2 * lax.rsqrt(ms + RMS_EPS) * g_ref[...]).reshape(x.shape)


def _qknorm_blocks(proj, first_blk, nblk, gain, tr):
    length = proj.shape[1]
    g = jnp.tile(gain.astype(F32), LANE // HEAD64).reshape(1, LANE)
    return pl.pallas_call(
        _qknorm_kernel,
        grid=(nblk, length // tr),
        in_specs=[pl.BlockSpec((1, tr, LANE), lambda h, i: (first_blk + h, i, 0)),
                  pl.BlockSpec((1, LANE), lambda h, i: (0, 0))],
        out_specs=pl.BlockSpec((1, tr, LANE), lambda h, i: (h, i, 0)),
        out_shape=jax.ShapeDtypeStruct((nblk, length, LANE), F32),
        compiler_params=_cp(("parallel", "parallel")),
        name="qknorm_blocks",
    )(proj, g)


def _qknorm_cols(proj, col_blk, width, gain, tr):
    m = proj.shape[0]
    g = jnp.tile(gain.astype(F32), width // HEAD64).reshape(1, width)
    return pl.pallas_call(
        _qknorm_kernel,
        grid=(m // tr,),
        in_specs=[pl.BlockSpec((tr, width), lambda i: (i, col_blk)),
                  pl.BlockSpec((1, width), lambda i: (0, 0))],
        out_specs=pl.BlockSpec((tr, width), lambda i: (i, 0)),
        out_shape=jax.ShapeDtypeStruct((m, width), F32),
        compiler_params=_cp(("parallel",)),
        name="qknorm_cols",
    )(proj, g)


def _split_bf16(x):
    hi = x.astype(BF16)
    lo = (x - hi.astype(F32)).astype(BF16)
    return hi, lo


def _sb_block(qh, kblk, vblk, tri, mask, carry):
    z = _dot_nt(qh, kblk)
    sp = _softplus(z)
    l = -sp if mask is None else jnp.where(mask, -sp, 0.0)
    hi, lo = _split_bf16(l)
    after = _dot(hi, tri) + _dot(lo, tri) + carry
    att = jnp.exp(z - sp + after)
    if mask is not None:
        att = jnp.where(mask, att, 0.0)
    return _dot(att.astype(BF16), vblk), carry + jnp.sum(l, axis=-1, keepdims=True)


def _later_tri(n):
    return (lax.broadcasted_iota(jnp.int32, (n, n), 0)
            > lax.broadcasted_iota(jnp.int32, (n, n), 1)).astype(BF16)


def _sb_prompt_kernel(q_ref, k_ref, v_ref, o_ref, *, tq, tk, pad, scale):
    qi = pl.program_id(1)
    q = q_ref[0] * scale
    lane = lax.broadcasted_iota(jnp.int32, (tq, LANE), 1)
    first = lane < HEAD64
    qa = jnp.where(first, q, 0.0).astype(BF16)
    qb = jnp.where(first, 0.0, q).astype(BF16)
    tri = _later_tri(tk)
    qpos = qi * tq + lax.broadcasted_iota(jnp.int32, (tq, tk), 0) - pad
    col = lax.broadcasted_iota(jnp.int32, (tq, tk), 1)
    per_q = tq // tk

    def body(it, carry):
        ca, cb, acca, accb = carry
        kb = qi * per_q + (per_q - 1) - it
        rows = pl.ds(pl.multiple_of(kb * tk, tk), tk)
        kblk = k_ref[0, rows, :].astype(BF16)
        vblk = v_ref[0, rows, :].astype(BF16)
        kpos = kb * tk + col - pad
        mask = (kpos >= 0) & (kpos < qpos)
        da, ca = _sb_block(qa, kblk, vblk, tri, mask, ca)
        db, cb = _sb_block(qb, kblk, vblk, tri, mask, cb)
        return ca, cb, acca + da, accb + db

    zc = jnp.zeros((tq, 1), F32)
    za = jnp.zeros((tq, LANE), F32)
    _, _, acca, accb = lax.fori_loop(0, (qi + 1) * per_q, body, (zc, zc, za, za))
    o_ref[...] = jnp.where(first, acca, accb)


def _sb_prompt(proj, length, pad, tq, tk):
    npair = 8
    return pl.pallas_call(
        functools.partial(_sb_prompt_kernel, tq=tq, tk=tk, pad=pad, scale=HEAD64 ** -0.5),
        grid=(npair, length // tq),
        in_specs=[pl.BlockSpec((1, tq, LANE), lambda p, i: (p, i, 0)),
                  pl.BlockSpec((1, length, LANE), lambda p, i: (npair + p, 0, 0)),
                  pl.BlockSpec((1, length, LANE), lambda p, i: (2 * npair + p, 0, 0))],
        out_specs=pl.BlockSpec((tq, LANE), lambda p, i: (i, p)),
        out_shape=jax.ShapeDtypeStruct((length, npair * LANE), F32),
        compiler_params=_cp(("parallel", "parallel")),
        name="sb_prompt",
    )(proj, proj, proj)


def _diff_lambda(lqk_ref, lam_init):
    lqk = lqk_ref[...]
    s1 = jnp.sum(lqk[0:1] * lqk[1:2], axis=-1, keepdims=True)
    s2 = jnp.sum(lqk[2:3] * lqk[3:4], axis=-1, keepdims=True)
    return jnp.exp(s1) - jnp.exp(s2) + lam_init


def _softmax_block(qh, kblk, vblk, bias, state):
    m, l, acc = state
    s = _dot_nt(qh, kblk) + bias
    m_new = jnp.maximum(m, jnp.max(s, axis=-1, keepdims=True))
    p = jnp.exp(s - m_new)
    corr = jnp.exp(m - m_new)
    return (m_new, corr * l + jnp.sum(p, axis=-1, keepdims=True),
            corr * acc + _dot(p.astype(BF16), vblk))


def _diff_prompt_kernel(q_ref, k_ref, v_ref, slope_ref, lqk_ref, sub_ref, o_ref, *,
                        tq, tk, pad, scale, lam_init):
    qi = pl.program_id(1)
    q = q_ref[0] * scale
    lane = lax.broadcasted_iota(jnp.int32, (tq, LANE), 1)
    first = lane < HEAD64
    q1 = jnp.where(first, q, 0.0).astype(BF16)
    q2 = jnp.where(first, 0.0, q).astype(BF16)
    slope = slope_ref[0][:, :1]
    qpos = qi * tq + lax.broadcasted_iota(jnp.int32, (tq, tk), 0) - pad
    col = lax.broadcasted_iota(jnp.int32, (tq, tk), 1)
    per_q = tq // tk

    def body(kb, carry):
        st1, st2 = carry
        rows = pl.ds(pl.multiple_of(kb * tk, tk), tk)
        kblk = k_ref[0, rows, :].astype(BF16)
        vblk = v_ref[0, rows, :].astype(BF16)
        kpos = kb * tk + col - pad
        dist = qpos - kpos
        ok = (kpos >= 0) & (dist >= 0)
        bias = jnp.where(ok, -slope * dist.astype(F32), NEG_INF)
        return _softmax_block(q1, kblk, vblk, bias, st1), _softmax_block(q2, kblk, vblk, bias, st2)

    init = (jnp.full((tq, 1), NEG_INF, F32), jnp.zeros((tq, 1), F32), jnp.zeros((tq, LANE), F32))
    (_, l1, a1), (_, l2, a2) = lax.fori_loop(0, (qi + 1) * per_q, body, (init, init))
    lam = _diff_lambda(lqk_ref, lam_init)
    o = a1 / l1 - lam * (a2 / l2)
    o_ref[...] = _rms_rows(o, sub_ref[...]) * (1.0 - lam_init)


def _diff_slopes(nheads):
    return jnp.exp2(-8.0 * jnp.arange(1, nheads + 1, dtype=F32) / nheads)


def _diff_prompt(qn, kn, proj, v_blk0, slopes, lqk, subln, length, pad, tq, tk, lam_init):
    nheads = qn.shape[0]
    return pl.pallas_call(
        functools.partial(_diff_prompt_kernel, tq=tq, tk=tk, pad=pad, scale=HEAD64 ** -0.5,
                          lam_init=lam_init),
        grid=(nheads, length // tq),
        in_specs=[pl.BlockSpec((1, tq, LANE), lambda h, i: (h, i, 0)),
                  pl.BlockSpec((1, length, LANE), lambda h, i: (h, 0, 0)),
                  pl.BlockSpec((1, length, LANE), lambda h, i: (v_blk0 + h, 0, 0)),
                  pl.BlockSpec((1, 1, LANE), lambda h, i: (h, 0, 0)),
                  pl.BlockSpec(lqk.shape, lambda h, i: (0, 0)),
                  pl.BlockSpec((1, LANE), lambda h, i: (0, 0))],
        out_specs=pl.BlockSpec((tq, LANE), lambda h, i: (i, h)),
        out_shape=jax.ShapeDtypeStruct((length, nheads * LANE), F32),
        compiler_params=_cp(("parallel", "parallel")),
        name="diff_prompt",
    )(qn, kn, proj, jnp.broadcast_to(slopes.reshape(nheads, 1, 1), (nheads, 1, LANE)), lqk,
      subln.astype(F32).reshape(1, LANE))


def _query_blockdiag(q, seg):
    tlen, w = q.shape
    nseg = w // seg
    rows = nseg * tlen
    qt = jnp.broadcast_to(q[None], (nseg, tlen, w)).reshape(rows, w)
    keep = (lax.broadcasted_iota(jnp.int32, (rows, w), 1) // seg
            == lax.broadcasted_iota(jnp.int32, (rows, w), 0) // tlen)
    return jnp.where(keep, qt, 0.0)


def _pad_rows(x, n):
    return jnp.concatenate([x, jnp.zeros((n - x.shape[0], x.shape[1]), x.dtype)], axis=0)


def _sb_sample_kernel(pt_ref, q_ref, kn_ref, vn_ref, kp_ref, vp_ref, o_ref, qbd_ref, acc_ref, car_ref, *,
                      scale):
    del pt_ref
    j = pl.program_id(1)
    tlen = q_ref.shape[1]
    page = kp_ref.shape[1]
    rows = qbd_ref.shape[0]
    tri = _later_tri(page)

    @pl.when(j == 0)
    def _():
        qbd = _query_blockdiag(q_ref[0] * scale, HEAD64)
        qbd_ref[...] = qbd.astype(BF16)
        kblk = _pad_rows(kn_ref[0], page).astype(BF16)
        vblk = _pad_rows(vn_ref[0], page).astype(BF16)
        t = lax.broadcasted_iota(jnp.int32, (rows, page), 0) % tlen
        s = lax.broadcasted_iota(jnp.int32, (rows, page), 1)
        d, c = _sb_block(qbd_ref[...], kblk, vblk, tri, s < t, jnp.zeros((rows, 1), F32))
        acc_ref[...] = d
        car_ref[...] = c

    @pl.when(j > 0)
    def _():
        d, c = _sb_block(qbd_ref[...], kp_ref[0].astype(BF16), vp_ref[0].astype(BF16), tri, None,
                         car_ref[...])
        acc_ref[...] += d
        car_ref[...] = c

    @pl.when(j == pl.num_programs(1) - 1)
    def _():
        acc = acc_ref[...]
        w = acc.shape[1]
        keep = (lax.broadcasted_iota(jnp.int32, (rows, w), 1) // HEAD64
                == lax.broadcasted_iota(jnp.int32, (rows, w), 0) // tlen)
        o_ref[0] = jnp.sum(jnp.where(keep, acc, 0.0).reshape(rows // tlen, tlen, w), axis=0)


def _sb_sample(page_table, proj3, cache_k, cache_v):
    bsz, tlen, _ = proj3.shape
    npages = page_table.shape[1]
    page, w = cache_k.shape[1], cache_k.shape[2]
    rows = (w // HEAD64) * tlen
    past = lambda b, j, pt: (pt[b, npages - jnp.maximum(j, 1)], 0, 0)
    grid_spec = pltpu.PrefetchScalarGridSpec(
        num_scalar_prefetch=1,
        grid=(bsz, npages + 1),
        in_specs=[pl.BlockSpec((1, tlen, w), lambda b, j, pt: (b, 0, 0)),
                  pl.BlockSpec((1, tlen, w), lambda b, j, pt: (b, 0, 1)),
                  pl.BlockSpec((1, tlen, w), lambda b, j, pt: (b, 0, 2)),
                  pl.BlockSpec((1, page, w), past),
                  pl.BlockSpec((1, page, w), past)],
        out_specs=pl.BlockSpec((1, tlen, w), lambda b, j, pt: (b, 0, 0)),
        scratch_shapes=[pltpu.VMEM((rows, w), BF16), pltpu.VMEM((rows, w), F32),
                        pltpu.VMEM((rows, 1), F32)])
    return pl.pallas_call(
        functools.partial(_sb_sample_kernel, scale=HEAD64 ** -0.5),
        grid_spec=grid_spec,
        out_shape=jax.ShapeDtypeStruct((bsz, tlen, w), F32),
        compiler_params=_cp(("parallel", "arbitrary")),
        name="sb_sample",
    )(page_table, proj3, proj3, proj3, cache_k, cache_v)


def _diff_sample_kernel(pt_ref, q_ref, kn_ref, vn_ref, kp_ref, vp_ref, slope_ref, lqk_ref, sub_ref, o_ref,
                        qbd_ref, acc_ref, m_ref, l_ref, *, scale, lam_init, past_len):
    del pt_ref
    j = pl.program_id(1)
    npg = pl.num_programs(1) - 1
    tlen = q_ref.shape[1]
    page = kp_ref.shape[1]
    rows = qbd_ref.shape[0]
    t = lax.broadcasted_iota(jnp.int32, (rows, page), 0) % tlen
    s = lax.broadcasted_iota(jnp.int32, (rows, page), 1)
    slope = slope_ref[...]

    @pl.when(j == 0)
    def _():
        qbd_ref[...] = _query_blockdiag(q_ref[0] * scale, HEAD64).astype(BF16)
        kblk = _pad_rows(kn_ref[0], page).astype(BF16)
        vblk = _pad_rows(vn_ref[0], page).astype(BF16)
        bias = jnp.where(s <= t, -slope * (t - s).astype(F32), NEG_INF)
        init = (jnp.full((rows, 1), NEG_INF, F32), jnp.zeros((rows, 1), F32),
                jnp.zeros(acc_ref.shape, F32))
        m_ref[...], l_ref[...], acc_ref[...] = _softmax_block(qbd_ref[...], kblk, vblk, bias, init)

    @pl.when(j > 0)
    def _():
        kpos = (npg - j) * page + s
        bias = -slope * (past_len + t - kpos).astype(F32)
        m_ref[...], l_ref[...], acc_ref[...] = _softmax_block(
            qbd_ref[...], kp_ref[0].astype(BF16), vp_ref[0].astype(BF16), bias,
            (m_ref[...], l_ref[...], acc_ref[...]))

    @pl.when(j == npg)
    def _():
        acc = acc_ref[...]
        w = acc.shape[1]
        lam = _diff_lambda(lqk_ref, lam_init)
        r = lax.broadcasted_iota(jnp.int32, (rows, 1), 0)
        coef = jnp.where((r // tlen) % 2 == 0, 1.0, -lam) / l_ref[...]
        keep = (lax.broadcasted_iota(jnp.int32, (rows, w), 1) // LANE
                == lax.broadcasted_iota(jnp.int32, (rows, w), 0) // (2 * tlen))
        y = jnp.sum(jnp.where(keep, acc * coef, 0.0).reshape(rows // tlen, tlen, w), axis=0)
        for h in range(w // LANE):
            lanes = slice(h * LANE, (h + 1) * LANE)
            o_ref[0, :, lanes] = _rms_rows(y[:, lanes], sub_ref[...]) * (1.0 - lam_init)


def _diff_sample(page_table, qn3, kn3, proj3, v_col, cache_k, cache_v, slopes, lqk, subln, lam_init):
    bsz, tlen, w = qn3.shape
    npages = page_table.shape[1]
    page = cache_k.shape[1]
    rows = (w // HEAD64) * tlen
    nheads = slopes.shape[0]
    slope_rows = jnp.broadcast_to(jnp.repeat(slopes, rows // nheads)[:, None], (rows, page))
    past = lambda b, j, pt: (pt[b, npages - jnp.maximum(j, 1)], 0, 0)
    grid_spec = pltpu.PrefetchScalarGridSpec(
        num_scalar_prefetch=1,
        grid=(bsz, npages + 1),
        in_specs=[pl.BlockSpec((1, tlen, w), lambda b, j, pt: (b, 0, 0)),
                  pl.BlockSpec((1, tlen, w), lambda b, j, pt: (b, 0, 0)),
                  pl.BlockSpec((1, tlen, w), lambda b, j, pt: (b, 0, v_col)),
                  pl.BlockSpec((1, page, w), past),
                  pl.BlockSpec((1, page, w), past),
                  pl.BlockSpec((rows, page), lambda b, j, pt: (0, 0)),
                  pl.BlockSpec(lqk.shape, lambda b, j, pt: (0, 0)),
                  pl.BlockSpec((1, LANE), lambda b, j, pt: (0, 0))],
        out_specs=pl.BlockSpec((1, tlen, w), lambda b, j, pt: (b, 0, 0)),
        scratch_shapes=[pltpu.VMEM((rows, w), BF16), pltpu.VMEM((rows, w), F32),
                        pltpu.VMEM((rows, 1), F32), pltpu.VMEM((rows, 1), F32)])
    return pl.pallas_call(
        functools.partial(_diff_sample_kernel, scale=HEAD64 ** -0.5, lam_init=lam_init,
                          past_len=npages * page),
        grid_spec=grid_spec,
        out_shape=jax.ShapeDtypeStruct((bsz, tlen, w), F32),
        compiler_params=_cp(("parallel", "arbitrary")),
        name="diff_sample",
    )(page_table, qn3, kn3, proj3, cache_k, cache_v, slope_rows, lqk, subln.astype(F32).reshape(1, LANE))


def _largest_tile(m, cap, mult=16):
    t = min(m, cap)
    while m % t or t % mult:
        t -= mult
    return t


def _layer0(x, bsz, tlen, s5_re0, s5_im0, rwkv_s0, shift_prev, wt):
    m = x.shape[0]
    tm = _largest_tile(m, 640)
    u = _proj(x, wt['norm_mix0'], wt['w_in0_u'], tm, 512)
    p = _proj(x, wt['norm_mix0'], wt['w_in0_p'], tm, 1152)

    bu_re, bu_im = _s5_b(u, wt['s5_wb_re'], wt['s5_wb_im'], tm)
    width = bu_re.shape[1]
    nb_s5, tb_s5 = (1, LANE) if bsz == 1 else (_largest_tile(bsz, 16, 1), tlen)
    h_re, h_im, f_re, f_im = _s5_scan(bu_re.reshape(bsz, tlen, width), bu_im.reshape(bsz, tlen, width),
                                      s5_re0, s5_im0, wt['s5_lb_re'], wt['s5_lb_im'], nb_s5, tb_s5)
    ya = _s5_c(h_re.reshape(m, width), h_im.reshape(m, width), u, wt['s5_wc_re'], wt['s5_wc_im'],
               wt['s5_d'], wt['s5_w_glu'], _largest_tile(m, 256))

    c = p.shape[1]
    if bsz == 1:
        tr = _largest_tile(tlen, 320)
        prev = jnp.concatenate([shift_prev.astype(F32), p[tr - 1::tr][:-1]])[:, None, :]
        p3, nb = p.reshape(tlen // tr, tr, c), 1
    else:
        prev = shift_prev.astype(F32)[:, None, :]
        p3, nb = p.reshape(bsz, tlen, c), _largest_tile(bsz, 32, 1)
    r, lw, k, v, kk, kka, g = _rwkv_pre(p3, prev, wt['rwkv_mu'], wt['rwkv_w0'], wt['rwkv_w2'],
                                        wt['rwkv_a0'], wt['rwkv_a2'], wt['rwkv_g2'], wt['rwkv_k_k'],
                                        wt['rwkv_k_a'], nb)
    w_r = r.shape[1]
    v3 = lambda a: a.reshape(bsz, tlen, w_r)
    tb, tc = (_largest_tile(tlen, 640, LANE), HEAD64) if bsz == 1 else (tlen, tlen)
    y, s_last = _rwkv_scan(v3(r), v3(lw), v3(k), v3(v), v3(kk), v3(kka), rwkv_s0, tb, tc)
    yb = _rwkv_post(y.reshape(m, w_r), r, k, v, g, wt['rwkv_ln_w'], wt['rwkv_ln_b'], wt['rwkv_r_k'],
                    _largest_tile(m, 512))

    x = _outproj(x, ya, yb, wt['w_out0_a'], wt['w_out0_b'], tm, 512)
    x = _ffn(x, wt['norm_ffn0'], wt['ffn_w1'], wt['ffn_w3'], wt['ffn_w2'], tm, 512)
    return x, f_re, f_im, s_last, p


def kernel(x_prompt, x_sample, state_s5_re, state_s5_im, state_rwkv, state_rwkv_shift, cache_sb_k, cache_sb_v, cache_diff_k, cache_diff_v, page_table, meta_tokens, norm_mix0, w_in0, s5_lambda_re, s5_lambda_im, s5_log_dt, s5_b_re, s5_b_im, s5_c_re, s5_c_im, s5_d, s5_w_glu, rwkv_mu, rwkv_w0, rwkv_w2, rwkv_a0, rwkv_a2, rwkv_g2, rwkv_k_k, rwkv_k_a, rwkv_r_k, rwkv_ln_w, rwkv_ln_b, w_out0, norm_ffn0, ffn_w1, ffn_w3, ffn_w2, norm_mix1, w_in1, diff_q_norm, diff_k_norm, diff_lambda_q1, diff_lambda_k1, diff_lambda_q2, diff_lambda_k2, diff_subln, w_out1, norm_ffn1, moe_router, moe_w1, moe_w3, moe_w2):
    d_model = x_prompt.shape[-1]
    n_groups, n_state, grp = s5_b_re.shape
    s5_width = n_groups * grp
    rwkv_width = rwkv_w0.shape[0]
    rwkv_heads = rwkv_width // HEAD64
    sb_width = cache_sb_k.shape[2] * cache_sb_k.shape[3]
    diff_heads = cache_diff_v.shape[2]
    diff_qk_width = cache_diff_k.shape[2] * cache_diff_k.shape[3] * cache_diff_k.shape[4]
    diff_v_width = diff_heads * cache_diff_v.shape[3]
    lam_init = 0.8 - 0.6 * math.exp(-0.3)
    bf = lambda a: a.astype(BF16)

    lb_re, lb_im, bb_re, bb_im = _s5_discretize(s5_lambda_re, s5_lambda_im, s5_log_dt, s5_b_re, s5_b_im)
    to_ghp = lambda a: jnp.transpose(a, (1, 0, 2))
    wt = dict(
        norm_mix0=norm_mix0.astype(F32), w_in0_u=bf(w_in0[:, :s5_width]), w_in0_p=bf(w_in0[:, s5_width:]),
        s5_lb_re=lb_re, s5_lb_im=lb_im,
        s5_wb_re=bf(_blockdiag(to_ghp(bb_re), 8)), s5_wb_im=bf(_blockdiag(to_ghp(bb_im), 8)),
        s5_wc_re=bf(_blockdiag(jnp.transpose(s5_c_re.astype(F32), (0, 2, 1)), 16)),
        s5_wc_im=bf(_blockdiag(jnp.transpose(s5_c_im.astype(F32), (0, 2, 1)), 16)),
        s5_d=s5_d.reshape(-1), s5_w_glu=bf(s5_w_glu),
        rwkv_mu=rwkv_mu, rwkv_w0=rwkv_w0, rwkv_w2=bf(rwkv_w2), rwkv_a0=rwkv_a0, rwkv_a2=bf(rwkv_a2),
        rwkv_g2=bf(rwkv_g2), rwkv_k_k=rwkv_k_k, rwkv_k_a=rwkv_k_a, rwkv_r_k=rwkv_r_k,
        rwkv_ln_w=rwkv_ln_w, rwkv_ln_b=rwkv_ln_b,
        w_out0_a=bf(w_out0[:s5_width]), w_out0_b=bf(w_out0[s5_width:]),
        norm_ffn0=norm_ffn0.astype(F32), ffn_w1=bf(ffn_w1), ffn_w3=bf(ffn_w3), ffn_w2=bf(ffn_w2),
    )
    w_in1_b = bf(w_in1)
    w_out1_a, w_out1_b = bf(w_out1[:sb_width]), bf(w_out1[sb_width:])
    moe_w1_b, moe_w3_b, moe_w2_b = bf(moe_w1), bf(moe_w3), bf(moe_w2)
    lqk = jnp.stack([diff_lambda_q1, diff_lambda_k1, diff_lambda_q2, diff_lambda_k2]).astype(F32)
    slopes = _diff_slopes(diff_heads)
    q_col = 3 * sb_width // diff_qk_width
    npair = sb_width // LANE

    bp, seq, _ = x_prompt.shape
    assert bp == 1
    real = N_META + seq
    pad = (-real) % LANE
    lp = pad + real
    xp = jnp.concatenate([jnp.zeros((pad, d_model), F32), meta_tokens.astype(F32), x_prompt[0].astype(F32)])
    tr_p = _largest_tile(lp, 640)
    tm_p = tr_p
    c_proj = w_in0.shape[1] - s5_width
    z_s5 = jnp.zeros((1, n_groups, n_state), F32)
    z_rwkv = jnp.zeros((1, rwkv_heads, HEAD64, HEAD64), F32)
    xh, p_s5_re, p_s5_im, p_rwkv, p_proj = _layer0(xp, 1, lp, z_s5, z_s5, z_rwkv,
                                                    jnp.zeros((1, c_proj), F32), wt)
    p_rwkv_shift = p_proj[lp - 1:lp]

    proj1 = _proj(xh, norm_mix1.astype(F32), w_in1_b, tm_p, 512, split=True)
    blk_dq = 3 * npair
    blk_dk = blk_dq + diff_heads
    blk_dv = blk_dk + diff_heads
    dqn = _qknorm_blocks(proj1, blk_dq, diff_heads, diff_q_norm, tr_p)
    dkn = _qknorm_blocks(proj1, blk_dk, diff_heads, diff_k_norm, tr_p)
    y_sb = _sb_prompt(proj1, lp, pad, 128, 128)
    y_d = _diff_prompt(dqn, dkn, proj1, blk_dv, slopes, lqk, diff_subln, lp, pad, 128, 128, lam_init)
    xh = _outproj(xh, y_sb, y_d, w_out1_a, w_out1_b, tm_p, 512)
    xh = _moe(xh, norm_ffn1.astype(F32), moe_router, moe_w1_b, moe_w3_b, moe_w2_b, tm_p, 256)
    y_prompt = xh[pad + N_META:][None]

    def tokens(blocks):
        return jnp.transpose(blocks[:, pad:], (1, 0, 2)).reshape(1, real, -1)

    p_sb_k = tokens(proj1[npair:2 * npair]).reshape(1, real, sb_width // HEAD64, HEAD64)
    p_sb_v = tokens(proj1[2 * npair:3 * npair]).reshape(1, real, sb_width // HEAD64, HEAD64)
    p_diff_k = tokens(dkn).reshape(1, real, diff_heads, 2, HEAD64)
    p_diff_v = tokens(proj1[blk_dv:blk_dv + diff_heads]).reshape(1, real, diff_heads, diff_v_width // diff_heads)

    db, dseq, _ = x_sample.shape
    ms = db * dseq
    xs = x_sample.astype(F32).reshape(ms, d_model)
    xs, s_s5_re, s_s5_im, s_rwkv, s_proj = _layer0(xs, db, dseq, state_s5_re, state_s5_im, state_rwkv,
                                                   state_rwkv_shift, wt)
    s_rwkv_shift = s_proj.reshape(db, dseq, c_proj)[:, -1]

    tm_s = _largest_tile(ms, 640)
    proj_s = _proj(xs, norm_mix1.astype(F32), w_in1_b, tm_s, 512)
    dqn_s = _qknorm_cols(proj_s, q_col, diff_qk_width, diff_q_norm, tm_s)
    dkn_s = _qknorm_cols(proj_s, q_col + 1, diff_qk_width, diff_k_norm, tm_s)
    n_pool, page = cache_sb_k.shape[0], cache_sb_k.shape[1]
    proj_s3 = proj_s.reshape(db, dseq, -1)
    pt = page_table.astype(jnp.int32)
    ys_sb = _sb_sample(pt, proj_s3, cache_sb_k.reshape(n_pool, page, sb_width),
                       cache_sb_v.reshape(n_pool, page, sb_width))
    v_col = (3 * sb_width + 2 * diff_qk_width) // diff_v_width
    ys_d = _diff_sample(pt, dqn_s.reshape(db, dseq, -1), dkn_s.reshape(db, dseq, -1), proj_s3, v_col,
                        cache_diff_k.reshape(n_pool, page, diff_qk_width),
                        cache_diff_v.reshape(n_pool, page, diff_v_width), slopes, lqk, diff_subln, lam_init)
    xs = _outproj(xs, ys_sb.reshape(ms, -1), ys_d.reshape(ms, -1), w_out1_a, w_out1_b, tm_s, 512)
    xs = _moe(xs, norm_ffn1.astype(F32), moe_router, moe_w1_b, moe_w3_b, moe_w2_b, tm_s, 256)
    y_sample = xs.reshape(db, dseq, d_model)

    s_sb_k = proj_s[:, sb_width:2 * sb_width].reshape(db, dseq, sb_width // HEAD64, HEAD64)
    s_sb_v = proj_s[:, 2 * sb_width:3 * sb_width].reshape(db, dseq, sb_width // HEAD64, HEAD64)
    s_diff_k = dkn_s.reshape(db, dseq, diff_heads, 2, HEAD64)
    s_diff_v = proj_s[:, v_col * diff_v_width:].reshape(db, dseq, diff_heads, diff_v_width // diff_heads)

    return (y_prompt, y_sample,
            p_s5_re.reshape(1, n_groups, n_state), p_s5_im.reshape(1, n_groups, n_state), p_rwkv, p_rwkv_shift,
            p_sb_k, p_sb_v, p_diff_k, p_diff_v,
            s_s5_re.reshape(db, n_groups, n_state), s_s5_im.reshape(db, n_groups, n_state), s_rwkv, s_rwkv_shift,
            s_sb_k, s_sb_v, s_diff_k, s_diff_v)
```

```python
import functools
import math

import jax
import jax.numpy as jnp
from jax import lax
from jax.experimental import pallas as pl
from jax.experimental.pallas import tpu as pltpu

F32 = jnp.float32
BF16 = jnp.bfloat16
HIGHEST = lax.Precision.HIGHEST

LANE = 128
HEAD64 = 64
RMS_EPS = 1e-6
RWKV_LN_EPS = 64e-5
N_META = 16
NEG_INF = -1e30
VMEM_LIMIT = 56 * 1024 * 1024


def _cp(sem, vmem=VMEM_LIMIT):
    return pltpu.CompilerParams(dimension_semantics=sem, vmem_limit_bytes=vmem)


def _dot(a, b, precision=None):
    return jnp.dot(a, b, preferred_element_type=F32, precision=precision)


def _dot_nt(a, b, precision=None):
    return lax.dot_general(a, b, (((1,), (1,)), ((), ())), preferred_element_type=F32,
                           precision=precision)


def _dot_tn(a, b, precision=None):
    return lax.dot_general(a, b, (((0,), (0,)), ((), ())), preferred_element_type=F32,
                           precision=precision)


def _rms_rows(x, gain):
    ms = jnp.mean(x * x, axis=-1, keepdims=True)
    return x * lax.rsqrt(ms + RMS_EPS) * gain


def _softplus(t):
    return jnp.maximum(t, 0.0) + jnp.log1p(jnp.exp(-jnp.abs(t)))


def _segsum64(x, e128):
    cols = [_dot(x[:, c * LANE:(c + 1) * LANE], e128, HIGHEST) for c in range(x.shape[1] // LANE)]
    return cols[0] if len(cols) == 1 else jnp.concatenate(cols, axis=1)


def _seg_ones():
    r = lax.broadcasted_iota(jnp.int32, (LANE, LANE), 0) // HEAD64
    c = lax.broadcasted_iota(jnp.int32, (LANE, LANE), 1) // HEAD64
    return (r == c).astype(F32)


def _proj_kernel(x_ref, g_ref, w_ref, o_ref, xn_ref, *, split):
    @pl.when(pl.program_id(1) == 0)
    def _():
        xn_ref[...] = _rms_rows(x_ref[...], g_ref[...]).astype(BF16)

    acc = _dot(xn_ref[...], w_ref[...])
    if split:
        for c in range(acc.shape[1] // LANE):
            o_ref[c] = acc[:, c * LANE:(c + 1) * LANE]
    else:
        o_ref[...] = acc


def _proj(x, gain, w, tm, tn, split=False):
    m, d = x.shape
    n = w.shape[1]
    if split:
        out_shape = jax.ShapeDtypeStruct((n // LANE, m, LANE), F32)
        out_spec = pl.BlockSpec((tn // LANE, tm, LANE), lambda i, j: (j, i, 0))
    else:
        out_shape = jax.ShapeDtypeStruct((m, n), F32)
        out_spec = pl.BlockSpec((tm, tn), lambda i, j: (i, j))
    return pl.pallas_call(
        functools.partial(_proj_kernel, split=split),
        grid=(m // tm, n // tn),
        in_specs=[pl.BlockSpec((tm, d), lambda i, j: (i, 0)),
                  pl.BlockSpec((1, d), lambda i, j: (0, 0)),
                  pl.BlockSpec((d, tn), lambda i, j: (0, j))],
        out_specs=out_spec,
        out_shape=out_shape,
        scratch_shapes=[pltpu.VMEM((tm, d), BF16)],
        compiler_params=_cp(("parallel", "arbitrary")),
        name="proj",
    )(x, gain.reshape(1, d), w)


def _outproj_kernel(res_ref, a_ref, b_ref, wa_ref, wb_ref, o_ref):
    o_ref[...] = (res_ref[...] + _dot(a_ref[...].astype(BF16), wa_ref[...])
                  + _dot(b_ref[...].astype(BF16), wb_ref[...]))


def _outproj(res, a, b, wa, wb, tm, tn):
    m, n = res.shape
    ka, kb = a.shape[1], b.shape[1]
    return pl.pallas_call(
        _outproj_kernel,
        grid=(m // tm, n // tn),
        in_specs=[pl.BlockSpec((tm, tn), lambda i, j: (i, j)),
                  pl.BlockSpec((tm, ka), lambda i, j: (i, 0)),
                  pl.BlockSpec((tm, kb), lambda i, j: (i, 0)),
                  pl.BlockSpec((ka, tn), lambda i, j: (0, j)),
                  pl.BlockSpec((kb, tn), lambda i, j: (0, j))],
        out_specs=pl.BlockSpec((tm, tn), lambda i, j: (i, j)),
        out_shape=jax.ShapeDtypeStruct((m, n), F32),
        compiler_params=_cp(("parallel", "arbitrary")),
        name="outproj",
    )(res, a, b, wa, wb)


def _ffn_kernel(x_ref, g_ref, w1_ref, w3_ref, w2_ref, o_ref, xn_ref):
    @pl.when(pl.program_id(1) == 0)
    def _():
        x = x_ref[...]
        xn_ref[...] = _rms_rows(x, g_ref[...]).astype(BF16)
        o_ref[...] = x

    xn = xn_ref[...]
    h1 = _dot(xn, w1_ref[...])
    h3 = _dot(xn, w3_ref[...])
    h = (h1 * jax.nn.sigmoid(h1) * h3).astype(BF16)
    o_ref[...] += _dot(h, w2_ref[...])


def _ffn(x, gain, w1, w3, w2, tm, tf):
    m, d = x.shape
    f = w1.shape[1]
    return pl.pallas_call(
        _ffn_kernel,
        grid=(m // tm, f // tf),
        in_specs=[pl.BlockSpec((tm, d), lambda i, j: (i, 0)),
                  pl.BlockSpec((1, d), lambda i, j: (0, 0)),
                  pl.BlockSpec((d, tf), lambda i, j: (0, j)),
                  pl.BlockSpec((d, tf), lambda i, j: (0, j)),
                  pl.BlockSpec((tf, d), lambda i, j: (j, 0))],
        out_specs=pl.BlockSpec((tm, d), lambda i, j: (i, 0)),
        out_shape=jax.ShapeDtypeStruct((m, d), F32),
        scratch_shapes=[pltpu.VMEM((tm, d), BF16)],
        compiler_params=_cp(("parallel", "arbitrary")),
        name="ffn",
    )(x, gain.reshape(1, d), w1, w3, w2)


def _moe_kernel(x_ref, g_ref, r_ref, w1_ref, w3_ref, w2_ref, o_ref, xn_ref, gate_ref, *, n_experts):
    e = pl.program_id(1)
    j = pl.program_id(2)
    tm = x_ref.shape[0]
    lane = lax.broadcasted_iota(jnp.int32, (tm, LANE), 1)

    @pl.when((e == 0) & (j == 0))
    def _():
        x = x_ref[...]
        xn = _rms_rows(x, g_ref[...])
        xn_ref[...] = xn.astype(BF16)
        o_ref[...] = x
        logits = jnp.where(lane < n_experts, _dot(xn, r_ref[...], HIGHEST), NEG_INF)
        v1 = jnp.max(logits, axis=-1, keepdims=True)
        i1 = jnp.min(jnp.where(logits == v1, lane, LANE), axis=-1, keepdims=True)
        rest = jnp.where(lane == i1, NEG_INF, logits)
        v2 = jnp.max(rest, axis=-1, keepdims=True)
        i2 = jnp.min(jnp.where(rest == v2, lane, LANE), axis=-1, keepdims=True)
        ex = jnp.exp(v2 - v1)
        den = 1.0 + ex
        gate_ref[...] = jnp.where(lane == i1, 1.0 / den, 0.0) + jnp.where(lane == i2, ex / den, 0.0)

    ge = jnp.sum(jnp.where(lane == e, gate_ref[...], 0.0), axis=-1, keepdims=True)
    xn = xn_ref[...]
    h1 = _dot(xn, w1_ref[0])
    h3 = _dot(xn, w3_ref[0])
    h = (h1 * jax.nn.sigmoid(h1) * h3).astype(BF16)
    o_ref[...] += ge * _dot(h, w2_ref[0])


def _moe(x, gain, router, w1, w3, w2, tm, tf):
    m, d = x.shape
    ne, _, f = w1.shape
    router_p = jnp.zeros((d, LANE), F32).at[:, :ne].set(router.astype(F32))
    return pl.pallas_call(
        functools.partial(_moe_kernel, n_experts=ne),
        grid=(m // tm, ne, f // tf),
        in_specs=[pl.BlockSpec((tm, d), lambda i, e, j: (i, 0)),
                  pl.BlockSpec((1, d), lambda i, e, j: (0, 0)),
                  pl.BlockSpec((d, LANE), lambda i, e, j: (0, 0)),
                  pl.BlockSpec((1, d, tf), lambda i, e, j: (e, 0, j)),
                  pl.BlockSpec((1, d, tf), lambda i, e, j: (e, 0, j)),
                  pl.BlockSpec((1, tf, d), lambda i, e, j: (e, j, 0))],
        out_specs=pl.BlockSpec((tm, d), lambda i, e, j: (i, 0)),
        out_shape=jax.ShapeDtypeStruct((m, d), F32),
        scratch_shapes=[pltpu.VMEM((tm, d), BF16), pltpu.VMEM((tm, LANE), F32)],
        compiler_params=_cp(("parallel", "arbitrary", "arbitrary")),
        name="moe",
    )(x, gain.reshape(1, d), router_p, w1, w3, w2)


def _s5_disc_kernel(lr_ref, li_ref, ldt_ref, bre_ref, bim_ref, lbre_ref, lbim_ref, bbre_ref, bbim_ref):
    lr = lr_ref[...]
    li = li_ref[...]
    dt = jnp.exp(ldt_ref[...])
    mag = jnp.exp(lr * dt)
    lb_re = mag * jnp.cos(li * dt)
    lb_im = mag * jnp.sin(li * dt)
    den = lr * lr + li * li
    f_re = ((lb_re - 1.0) * lr + lb_im * li) / den
    f_im = (lb_im * lr - (lb_re - 1.0) * li) / den
    lbre_ref[...] = lb_re
    lbim_ref[...] = lb_im
    for h in range(bre_ref.shape[0]):
        br = bre_ref[h]
        bi = bim_ref[h]
        bbre_ref[h] = f_re * br - f_im * bi
        bbim_ref[h] = f_re * bi + f_im * br


def _s5_discretize(lam_re, lam_im, log_dt, b_re, b_im):
    g, p, h = b_re.shape
    sds = jax.ShapeDtypeStruct
    return pl.pallas_call(
        _s5_disc_kernel,
        out_shape=(sds((g, p), F32), sds((g, p), F32), sds((h, g, p), F32), sds((h, g, p), F32)),
        name="s5_disc",
    )(lam_re.astype(F32), lam_im.astype(F32), log_dt.astype(F32).reshape(g, 1),
      jnp.transpose(b_re.astype(F32), (2, 0, 1)), jnp.transpose(b_im.astype(F32), (2, 0, 1)))


def _s5_b_kernel(u_ref, wre_ref, wim_ref, ore_ref, oim_ref):
    u = u_ref[...].astype(BF16)
    ore_ref[...] = _dot(u, wre_ref[0])
    oim_ref[...] = _dot(u, wim_ref[0])


def _s5_b(u, wre, wim, tm):
    m = u.shape[0]
    nc, kc, ncol = wre.shape
    spec_o = pl.BlockSpec((tm, ncol), lambda i, c: (i, c))
    spec_w = pl.BlockSpec((1, kc, ncol), lambda i, c: (c, 0, 0))
    sds = jax.ShapeDtypeStruct((m, nc * ncol), F32)
    return pl.pallas_call(
        _s5_b_kernel,
        grid=(m // tm, nc),
        in_specs=[pl.BlockSpec((tm, kc), lambda i, c: (i, c)), spec_w, spec_w],
        out_specs=(spec_o, spec_o),
        out_shape=(sds, sds),
        compiler_params=_cp(("parallel", "arbitrary")),
        name="s5_b",
    )(u, wre, wim)


def _s5_scan_kernel(bre_ref, bim_ref, h0re_ref, h0im_ref, lre_ref, lim_ref,
                    hre_ref, him_ref, fre_ref, fim_ref, cre_ref, cim_ref):
    nb, tlen = bre_ref.shape[0], bre_ref.shape[1]

    @pl.when(pl.program_id(1) == 0)
    def _():
        cre_ref[...] = h0re_ref[...]
        cim_ref[...] = h0im_ref[...]

    lre = lre_ref[...]
    lim = lim_ref[...]

    def per_seq(b, _):
        def step(t, carry):
            hr, hi = carry
            nr = lre * hr - lim * hi + bre_ref[b, t]
            ni = lre * hi + lim * hr + bim_ref[b, t]
            hre_ref[b, t] = nr
            him_ref[b, t] = ni
            return nr, ni

        hr, hi = lax.fori_loop(0, tlen, step, (cre_ref[b], cim_ref[b]), unroll=8)
        cre_ref[b] = hr
        cim_ref[b] = hi
        return 0

    lax.fori_loop(0, nb, per_seq, 0)
    fre_ref[...] = cre_ref[...]
    fim_ref[...] = cim_ref[...]


def _s5_scan(bu_re, bu_im, h0_re, h0_im, lb_re, lb_im, nb, tb):
    bsz, length, width = bu_re.shape
    sub = width // LANE
    v4 = lambda a: a.reshape(bsz, length, sub, LANE)
    v3 = lambda a: a.astype(F32).reshape(bsz, sub, LANE)
    spec_x = pl.BlockSpec((nb, tb, sub, LANE), lambda b, t: (b, t, 0, 0))
    spec_h = pl.BlockSpec((nb, sub, LANE), lambda b, t: (b, 0, 0))
    spec_l = pl.BlockSpec((sub, LANE), lambda b, t: (0, 0))
    sds_x = jax.ShapeDtypeStruct((bsz, length, sub, LANE), F32)
    sds_h = jax.ShapeDtypeStruct((bsz, sub, LANE), F32)
    hre, him, fre, fim = pl.pallas_call(
        _s5_scan_kernel,
        grid=(bsz // nb, length // tb),
        in_specs=[spec_x, spec_x, spec_h, spec_h, spec_l, spec_l],
        out_specs=(spec_x, spec_x, spec_h, spec_h),
        out_shape=(sds_x, sds_x, sds_h, sds_h),
        scratch_shapes=[pltpu.VMEM((nb, sub, LANE), F32), pltpu.VMEM((nb, sub, LANE), F32)],
        compiler_params=_cp(("parallel", "arbitrary")),
        name="s5_scan",
    )(v4(bu_re), v4(bu_im), v3(h0_re), v3(h0_im), lb_re.reshape(sub, LANE), lb_im.reshape(sub, LANE))
    return hre.reshape(bsz, length, width), him.reshape(bsz, length, width), fre, fim


def _s5_c_kernel(hre_ref, him_ref, u_ref, wcre_ref, wcim_ref, d_ref, wglu_ref, o_ref):
    nc, kc, _ = wcre_ref.shape
    cols = []
    for c in range(nc):
        hr = hre_ref[:, c * kc:(c + 1) * kc].astype(BF16)
        hi = him_ref[:, c * kc:(c + 1) * kc].astype(BF16)
        cols.append(_dot(hr, wcre_ref[c]) - _dot(hi, wcim_ref[c]))
    y = jnp.concatenate(cols, axis=1) + d_ref[...] * u_ref[...]
    z = jax.nn.gelu(y)
    o_ref[...] = z * jax.nn.sigmoid(_dot(z.astype(BF16), wglu_ref[...]))


def _s5_c(h_re, h_im, u, wcre, wcim, d_skip, w_glu, tm):
    m, width = h_re.shape
    w = u.shape[1]
    full3 = lambda a: pl.BlockSpec(a.shape, lambda i: (0, 0, 0))
    return pl.pallas_call(
        _s5_c_kernel,
        grid=(m // tm,),
        in_specs=[pl.BlockSpec((tm, width), lambda i: (i, 0)),
                  pl.BlockSpec((tm, width), lambda i: (i, 0)),
                  pl.BlockSpec((tm, w), lambda i: (i, 0)),
                  full3(wcre), full3(wcim),
                  pl.BlockSpec((1, w), lambda i: (0, 0)),
                  pl.BlockSpec((w, w), lambda i: (0, 0))],
        out_specs=pl.BlockSpec((tm, w), lambda i: (i, 0)),
        out_shape=jax.ShapeDtypeStruct((m, w), F32),
        compiler_params=_cp(("parallel",)),
        name="s5_c",
    )(h_re, h_im, u, wcre, wcim, d_skip.astype(F32).reshape(1, w), w_glu)


def _blockdiag(w, nblk):
    g, a, b = w.shape
    w4 = w.reshape(g // nblk, nblk, a, b)
    eye = jnp.eye(nblk, dtype=w.dtype)
    return jnp.einsum('cgab,gh->cgahb', w4, eye).reshape(g // nblk, nblk * a, nblk * b)


def _rwkv_pre_kernel(p_ref, prev_ref, mu_ref, w0_ref, w2_ref, a0_ref, a2_ref, g2_ref, kk_ref, ka_ref,
                     r_o, lw_o, k_o, v_o, kk_o, kka_o, g_o, *, width, lora_w, lora_a, n_valid):
    nb, tlen, c = p_ref.shape
    p = p_ref[...]
    tidx = lax.broadcasted_iota(jnp.int32, (nb, tlen, c), 1)
    prev = jnp.where(tidx == 0, prev_ref[...], pltpu.roll(p, 1, axis=1))
    xm = (p + (prev - p) * mu_ref[...]).reshape(nb * tlen, c)
    r = xm[:, :width]
    k = xm[:, width:2 * width]
    v = xm[:, 2 * width:3 * width]
    o = 3 * width
    w_lo = xm[:, o:o + lora_w]
    a_lo = xm[:, o + lora_w:o + lora_w + lora_a]
    g_lo = xm[:, o + lora_w + lora_a:]
    wraw = -_softplus(-(w0_ref[...] + _dot(jnp.tanh(w_lo).astype(BF16), w2_ref[...]))) - 0.5
    a = jax.nn.sigmoid(a0_ref[...] + _dot(a_lo.astype(BF16), a2_ref[...]))
    g = _dot(jax.nn.sigmoid(g_lo).astype(BF16), g2_ref[...])
    kkr = k * kk_ref[...]
    nrm = jnp.sqrt(_segsum64(kkr * kkr, _seg_ones()))
    kk = kkr / jnp.maximum(nrm, 1e-12)
    row = pl.program_id(0) * (nb * tlen) + lax.broadcasted_iota(jnp.int32, (nb * tlen, 1), 0)
    keep = (row < n_valid).astype(F32)
    r_o[...] = r
    lw_o[...] = -jnp.exp(wraw) * keep
    k_o[...] = k * (1.0 + (a - 1.0) * ka_ref[...]) * keep
    v_o[...] = v * keep
    kk_o[...] = kk * keep
    kka_o[...] = kk * a * keep
    g_o[...] = g


def _rwkv_pre(p3, prev, mu, w0, w2, a0, a2, g2, k_k, k_a, nb, n_valid):
    nbt, tlen, c = p3.shape
    width = w0.shape[0]
    lora_w, lora_a = w2.shape[0], a2.shape[0]
    rows = nb * tlen
    row1 = lambda a: a.astype(F32).reshape(1, -1)
    full2 = lambda a: pl.BlockSpec(a.shape, lambda i: (0, 0))
    spec_o = pl.BlockSpec((rows, width), lambda i: (i, 0))
    sds = jax.ShapeDtypeStruct((nbt * tlen, width), F32)
    args = (mu.astype(F32).reshape(1, 1, c), row1(w0), w2, row1(a0), a2, g2, row1(k_k), row1(k_a))
    return pl.pallas_call(
        functools.partial(_rwkv_pre_kernel, width=width, lora_w=lora_w, lora_a=lora_a, n_valid=n_valid),
        grid=(nbt // nb,),
        in_specs=[pl.BlockSpec((nb, tlen, c), lambda i: (i, 0, 0)),
                  pl.BlockSpec((nb, 1, c), lambda i: (i, 0, 0)),
                  pl.BlockSpec((1, 1, c), lambda i: (0, 0, 0))] + [full2(a) for a in args[1:]],
        out_specs=(spec_o,) * 7,
        out_shape=(sds,) * 7,
        compiler_params=_cp(("parallel",)),
        name="rwkv_pre",
    )(p3, prev, *args)


def _rwkv_scan_kernel(r_ref, lw_ref, k_ref, v_ref, kk_ref, kka_ref, s0_ref, y_ref, sl_ref, s_ref, *, tc):
    tb = r_ref.shape[1]
    npair = r_ref.shape[2] // LANE
    n2 = 2 * tc
    lane = lax.broadcasted_iota(jnp.int32, (tc, LANE), 1)
    first = lane < HEAD64
    ri = lax.broadcasted_iota(jnp.int32, (n2, n2), 0)
    ci = lax.broadcasted_iota(jnp.int32, (n2, n2), 1)
    same = (ri // tc) == (ci // tc)
    strict = same & ((ci % tc) < (ri % tc))
    incl = same & ((ci % tc) <= (ri % tc))
    eye = (ri == ci).astype(F32)
    tri = (lax.broadcasted_iota(jnp.int32, (tc, tc), 1)
           <= lax.broadcasted_iota(jnp.int32, (tc, tc), 0)).astype(F32)
    n_sq = max(int(math.ceil(math.log2(tc))) - 1, 0)
    blk_r = lax.broadcasted_iota(jnp.int32, (LANE, LANE), 0) // HEAD64
    blk_c = lax.broadcasted_iota(jnp.int32, (LANE, LANE), 1) // HEAD64

    def stack(x):
        return jnp.concatenate([jnp.where(first, x, 0.0), jnp.where(first, 0.0, x)], axis=0)

    @pl.when(pl.program_id(1) == 0)
    def _():
        for pr in range(npair):
            sa = jnp.concatenate([s0_ref[0, 2 * pr], jnp.zeros((HEAD64, HEAD64), F32)], axis=1)
            sb = jnp.concatenate([jnp.zeros((HEAD64, HEAD64), F32), s0_ref[0, 2 * pr + 1]], axis=1)
            s_ref[pr] = jnp.concatenate([sa, sb], axis=0)

    def chunk(ci_, _):
        rows = pl.ds(pl.multiple_of(ci_ * tc, tc), tc)
        for pr in range(npair):
            lanes = slice(pr * LANE, (pr + 1) * LANE)
            lw = lw_ref[0, rows, lanes]
            c = _dot(tri, lw, HIGHEST)
            c_last = c[tc - 1:tc, :]
            kk = kk_ref[0, rows, lanes]
            kka = kka_ref[0, rows, lanes]
            kx = k_ref[0, rows, lanes]
            e_neg = jnp.exp(-c)
            e_end = jnp.exp(c_last - c)
            al = stack(-kk * jnp.exp(c - lw))
            rt = stack(r_ref[0, rows, lanes] * jnp.exp(c))
            bt = stack(kka * e_neg)
            kt = stack(kx * e_neg)
            bh = stack(kka * e_end)
            kh = stack(kx * e_end)
            vs = stack(v_ref[0, rows, lanes])
            nab = jnp.where(strict, _dot_nt(al, bt, HIGHEST), 0.0)
            aak = jnp.where(strict, _dot_nt(al, kt, HIGHEST), 0.0)
            arb = jnp.where(incl, _dot_nt(rt, bt, HIGHEST), 0.0)
            ark = jnp.where(incl, _dot_nt(rt, kt, HIGHEST), 0.0)
            inv = eye + nab
            pw = nab
            for _ in range(n_sq):
                pw = _dot(pw, pw, HIGHEST)
                inv = inv + _dot(inv, pw, HIGHEST)
            s0 = s_ref[pr]
            u = _dot(inv, _dot_nt(al, s0, HIGHEST) + _dot(aak, vs, HIGHEST), HIGHEST)
            ys = _dot_nt(rt, s0, HIGHEST) + _dot(arb, u, HIGHEST) + _dot(ark, vs, HIGHEST)
            y_ref[0, rows, lanes] = ys[:tc] + ys[tc:]
            s_ref[pr] = (s0 * jnp.exp(c_last) + _dot_tn(u, bh, HIGHEST) + _dot_tn(vs, kh, HIGHEST))
        return 0

    lax.fori_loop(0, tb // tc, chunk, 0)

    for pr in range(npair):
        s = s_ref[pr]
        sl_ref[0, 2 * pr] = s[:HEAD64, :HEAD64]
        sl_ref[0, 2 * pr + 1] = s[HEAD64:, HEAD64:]


def _rwkv_scan(r, lw, k, v, kk, kka, s0, tb, tc):
    bsz, length, width = r.shape
    spec_x = pl.BlockSpec((1, tb, width), lambda b, t: (b, t, 0))
    spec_s = pl.BlockSpec((1,) + s0.shape[1:], lambda b, t: (b, 0, 0, 0))
    return pl.pallas_call(
        functools.partial(_rwkv_scan_kernel, tc=tc),
        grid=(bsz, length // tb),
        in_specs=[spec_x] * 6 + [spec_s],
        out_specs=(spec_x, spec_s),
        out_shape=(jax.ShapeDtypeStruct((bsz, length, width), F32),
                   jax.ShapeDtypeStruct(s0.shape, F32)),
        scratch_shapes=[pltpu.VMEM((width // LANE, LANE, LANE), F32)],
        compiler_params=_cp(("parallel", "arbitrary")),
        name="rwkv_scan",
    )(r, lw, k, v, kk, kka, s0.astype(F32))


def _rwkv_steps_kernel(r_ref, lw_ref, k_ref, v_ref, kk_ref, kka_ref, s0_ref, y_ref, sl_ref):
    tlen = r_ref.shape[0]
    n = r_ref.shape[2]
    sl_ref[...] = s0_ref[...]
    for t in range(tlen):
        w = jnp.exp(lw_ref[t, 0])
        kk = kk_ref[t, 0]
        kka = kka_ref[t, 0]
        kx = k_ref[t, 0]
        rx = r_ref[t, 0]

        def row(i, _):
            s = sl_ref[0, i]
            sa = -jnp.sum(s * kk, axis=0, keepdims=True)
            s = s * w + sa * kka + v_ref[t, 0, pl.ds(i, 1), :] * kx
            sl_ref[0, i] = s
            y_ref[t, 0, pl.ds(i, 1), :] = jnp.sum(s * rx, axis=0, keepdims=True)
            return 0

        lax.fori_loop(0, n, row, 0)


def _rwkv_steps(r, lw, k, v, kk, kka, s0):
    tlen, nh, n, bsz = r.shape
    spec_x = pl.BlockSpec((tlen, 1, n, bsz), lambda h: (0, h, 0, 0))
    spec_s = pl.BlockSpec((1, n, n, bsz), lambda h: (h, 0, 0, 0))
    return pl.pallas_call(
        _rwkv_steps_kernel,
        grid=(nh,),
        in_specs=[spec_x] * 6 + [spec_s],
        out_specs=(spec_x, spec_s),
        out_shape=(jax.ShapeDtypeStruct(r.shape, F32), jax.ShapeDtypeStruct(s0.shape, F32)),
        compiler_params=_cp(("parallel",)),
        name="rwkv_steps",
    )(r, lw, k, v, kk, kka, s0)


def _rwkv_post_kernel(y_ref, r_ref, k_ref, v_ref, g_ref, lnw_ref, lnb_ref, rk_ref, o_ref):
    e128 = _seg_ones()
    y = y_ref[...]
    mean = _segsum64(y, e128) * (1.0 / HEAD64)
    d = y - mean
    var = _segsum64(d * d, e128) * (1.0 / HEAD64)
    yn = d * lax.rsqrt(var + RWKV_LN_EPS) * lnw_ref[...] + lnb_ref[...]
    bonus = _segsum64(r_ref[...] * k_ref[...] * rk_ref[...], e128) * v_ref[...]
    o_ref[...] = (yn + bonus) * g_ref[...]


def _rwkv_post(y, r, k, v, g, ln_w, ln_b, r_k, tm):
    m, w = y.shape
    spec_x = pl.BlockSpec((tm, w), lambda i: (i, 0))
    spec_p = pl.BlockSpec((1, w), lambda i: (0, 0))
    row1 = lambda a: a.astype(F32).reshape(1, w)
    return pl.pallas_call(
        _rwkv_post_kernel,
        grid=(m // tm,),
        in_specs=[spec_x] * 5 + [spec_p] * 3,
        out_specs=spec_x,
        out_shape=jax.ShapeDtypeStruct((m, w), F32),
        compiler_params=_cp(("parallel",)),
        name="rwkv_post",
    )(y, r, k, v, g, row1(ln_w), row1(ln_b), row1(r_k))


def _qknorm_kernel(x_ref, g_ref, o_ref):
    x = x_ref[...]
    x2 = x.reshape(-1, x.shape[-1])
    ms = _segsum64(x2 * x2, _seg_ones()) * (1.0 / HEAD64)
    o_ref[...] = (x2 * lax.rsqrt(ms + RMS_EPS) * g_ref[...]).reshape(x.shape)


def _qknorm_blocks(proj, first_blk, nblk, gain, tr):
    length = proj.shape[1]
    g = jnp.tile(gain.astype(F32), LANE // HEAD64).reshape(1, LANE)
    return pl.pallas_call(
        _qknorm_kernel,
        grid=(nblk, length // tr),
        in_specs=[pl.BlockSpec((1, tr, LANE), lambda h, i: (first_blk + h, i, 0)),
                  pl.BlockSpec((1, LANE), lambda h, i: (0, 0))],
        out_specs=pl.BlockSpec((1, tr, LANE), lambda h, i: (h, i, 0)),
        out_shape=jax.ShapeDtypeStruct((nblk, length, LANE), F32),
        compiler_params=_cp(("parallel", "parallel")),
        name="qknorm_blocks",
    )(proj, g)


def _qknorm_cols(proj, col_blk, width, gain, tr):
    m = proj.shape[0]
    g = jnp.tile(gain.astype(F32), width // HEAD64).reshape(1, width)
    return pl.pallas_call(
        _qknorm_kernel,
        grid=(m // tr,),
        in_specs=[pl.BlockSpec((tr, width), lambda i: (i, col_blk)),
                  pl.BlockSpec((1, width), lambda i: (0, 0))],
        out_specs=pl.BlockSpec((tr, width), lambda i: (i, 0)),
        out_shape=jax.ShapeDtypeStruct((m, width), F32),
        compiler_params=_cp(("parallel",)),
        name="qknorm_cols",
    )(proj, g)


def _split_bf16(x):
    hi = x.astype(BF16)
    lo = (x - hi.astype(F32)).astype(BF16)
    return hi, lo


def _sb_weights(z, tri, mask, carry):
    sp = jnp.maximum(z, 0.0) + jnp.log(1.0 + jnp.exp(-jnp.abs(z)))
    n = sp if mask is None else jnp.where(mask, sp, 0.0)
    sub = tri.shape[0]
    parts = []
    for b in range(z.shape[1] // sub - 1, -1, -1):
        nb = n[:, b * sub:(b + 1) * sub]
        hi, lo = _split_bf16(nb)
        parts.append(carry - (_dot(hi, tri) + _dot(lo, tri)))
        carry = carry - jnp.sum(nb, axis=-1, keepdims=True)
    log_later = parts[0] if len(parts) == 1 else jnp.concatenate(parts[::-1], axis=1)
    att = jnp.exp(z - sp + log_later)
    if mask is not None:
        att = jnp.where(mask, att, 0.0)
    return att, carry


def _sb_block(qh, kblk, vblk, tri, mask, carry):
    att, carry = _sb_weights(_dot_nt(qh, kblk), tri, mask, carry)
    return _dot(att.astype(BF16), vblk), carry


def _later_tri(n):
    return (lax.broadcasted_iota(jnp.int32, (n, n), 0)
            > lax.broadcasted_iota(jnp.int32, (n, n), 1)).astype(BF16)


def _stack_halves(q):
    first = lax.broadcasted_iota(jnp.int32, q.shape, 1) < HEAD64
    return jnp.concatenate([jnp.where(first, q, 0.0), jnp.where(first, 0.0, q)], axis=0)


def _key_minus_query(tq, tk):
    row = lax.broadcasted_iota(jnp.int32, (2 * tq, tk), 0)
    row = jnp.where(row >= tq, row - tq, row)
    return lax.broadcasted_iota(jnp.int32, (2 * tq, tk), 1) - row


def _sb_prompt_kernel(q_ref, k_ref, v_ref, o_ref, *, tq, tk, scale):
    qi = pl.program_id(1)
    qs = _stack_halves(q_ref[0] * scale).astype(BF16)
    tri = _later_tri(LANE)
    kmq = _key_minus_query(tq, tk)

    def block(kb, carry, masked):
        c, acc = carry
        rows = pl.ds(pl.multiple_of(kb * tk, tk), tk)
        kblk = k_ref[0, rows, :].astype(BF16)
        vblk = v_ref[0, rows, :].astype(BF16)
        mask = (kmq < qi * tq - kb * tk) if masked else None
        d, c = _sb_block(qs, kblk, vblk, tri, mask, c)
        return c, acc + d

    carry = (jnp.zeros((2 * tq, 1), F32), jnp.zeros((2 * tq, LANE), F32))
    n_full = (qi * tq) // tk
    last = ((qi + 1) * tq - 1) // tk
    carry = lax.fori_loop(0, last + 1 - n_full, lambda it, cr: block(last - it, cr, True), carry)
    carry = lax.fori_loop(0, n_full, lambda it, cr: block(n_full - 1 - it, cr, False), carry)
    acc = carry[1]
    first = lax.broadcasted_iota(jnp.int32, (tq, LANE), 1) < HEAD64
    o_ref[...] = jnp.where(first, acc[:tq], acc[tq:])


def _sb_prompt(proj, length, tq, tk):
    npair = 8
    return pl.pallas_call(
        functools.partial(_sb_prompt_kernel, tq=tq, tk=tk, scale=HEAD64 ** -0.5),
        grid=(npair, length // tq),
        in_specs=[pl.BlockSpec((1, tq, LANE), lambda p, i: (p, i, 0)),
                  pl.BlockSpec((1, length, LANE), lambda p, i: (npair + p, 0, 0)),
                  pl.BlockSpec((1, length, LANE), lambda p, i: (2 * npair + p, 0, 0))],
        out_specs=pl.BlockSpec((tq, LANE), lambda p, i: (i, p)),
        out_shape=jax.ShapeDtypeStruct((length, npair * LANE), F32),
        compiler_params=_cp(("parallel", "parallel")),
        name="sb_prompt",
    )(proj, proj, proj)


def _diff_lambda(lqk_ref, lam_init):
    lqk = lqk_ref[...]
    s1 = jnp.sum(lqk[0:1] * lqk[1:2], axis=-1, keepdims=True)
    s2 = jnp.sum(lqk[2:3] * lqk[3:4], axis=-1, keepdims=True)
    return jnp.exp(s1) - jnp.exp(s2) + lam_init


def _softmax_block(qh, kblk, vblk, bias, state):
    m, l, acc = state
    s = _dot_nt(qh, kblk) + bias
    m_new = jnp.maximum(m, jnp.max(s, axis=-1, keepdims=True))
    p = jnp.exp(s - m_new)
    corr = jnp.exp(m - m_new)
    return (m_new, corr * l + jnp.sum(p, axis=-1, keepdims=True),
            corr * acc + _dot(p.astype(BF16), vblk))


def _diff_prompt_kernel(q_ref, k_ref, v_ref, slope_ref, lqk_ref, sub_ref, o_ref, *,
                        tq, tk, scale, lam_init):
    qi = pl.program_id(1)
    qs = _stack_halves(q_ref[0] * scale).astype(BF16)
    slope = slope_ref[0][:, :1]
    kmq = _key_minus_query(tq, tk)
    alibi = slope * kmq.astype(F32)

    def block(kb, state, masked):
        rows = pl.ds(pl.multiple_of(kb * tk, tk), tk)
        kblk = k_ref[0, rows, :].astype(BF16)
        vblk = v_ref[0, rows, :].astype(BF16)
        off = kb * tk - qi * tq
        bias = alibi + slope * off.astype(F32)
        if masked:
            bias = jnp.where(kmq <= -off, bias, NEG_INF)
        return _softmax_block(qs, kblk, vblk, bias, state)

    state = (jnp.full((2 * tq, 1), NEG_INF, F32), jnp.zeros((2 * tq, 1), F32),
             jnp.zeros((2 * tq, LANE), F32))
    n_full = (qi * tq) // tk
    last = ((qi + 1) * tq - 1) // tk
    state = lax.fori_loop(0, n_full, lambda kb, st: block(kb, st, False), state)
    state = lax.fori_loop(n_full, last + 1, lambda kb, st: block(kb, st, True), state)
    _, l, acc = state
    lam = _diff_lambda(lqk_ref, lam_init)
    o = acc[:tq] / l[:tq] - lam * (acc[tq:] / l[tq:])
    o_ref[...] = _rms_rows(o, sub_ref[...]) * (1.0 - lam_init)


def _diff_slopes(nheads):
    return jnp.exp2(-8.0 * jnp.arange(1, nheads + 1, dtype=F32) / nheads)


def _diff_prompt(qn, kn, proj, v_blk0, slopes, lqk, subln, length, tq, tk, lam_init):
    nheads = qn.shape[0]
    return pl.pallas_call(
        functools.partial(_diff_prompt_kernel, tq=tq, tk=tk, scale=HEAD64 ** -0.5, lam_init=lam_init),
        grid=(nheads, length // tq),
        in_specs=[pl.BlockSpec((1, tq, LANE), lambda h, i: (h, i, 0)),
                  pl.BlockSpec((1, length, LANE), lambda h, i: (h, 0, 0)),
                  pl.BlockSpec((1, length, LANE), lambda h, i: (v_blk0 + h, 0, 0)),
                  pl.BlockSpec((1, 1, LANE), lambda h, i: (h, 0, 0)),
                  pl.BlockSpec(lqk.shape, lambda h, i: (0, 0)),
                  pl.BlockSpec((1, LANE), lambda h, i: (0, 0))],
        out_specs=pl.BlockSpec((tq, LANE), lambda h, i: (i, h)),
        out_shape=jax.ShapeDtypeStruct((length, nheads * LANE), F32),
        compiler_params=_cp(("parallel", "parallel")),
        name="diff_prompt",
    )(qn, kn, proj, jnp.broadcast_to(slopes.reshape(nheads, 1, 1), (nheads, 1, LANE)), lqk,
      subln.astype(F32).reshape(1, LANE))


def _page_specs(page_table, pages_per_step, block):
    npages = page_table.shape[1]
    zeros = (0,) * (len(block) - 1)

    def spec(i):
        return pl.BlockSpec(block, lambda b, j, pt: (pt[b, npages - 1 - (j * pages_per_step + i)],) + zeros)

    return [spec(i) for i in range(pages_per_step)]


def _sb_sample_kernel(pt_ref, q_ref, knt_ref, vnt_ref, *refs, pages_per_step, scale):
    del pt_ref
    k_refs, v_refs = refs[:pages_per_step], refs[pages_per_step:2 * pages_per_step]
    o_ref, kpad_ref, vpad_ref, acc_ref, car_ref = refs[2 * pages_per_step:]
    j = pl.program_id(1)
    nh, tlen, _ = q_ref.shape[1:]
    tnew = knt_ref.shape[3]
    page = kpad_ref.shape[2]
    rows = nh * tlen
    tri = _later_tri(page)
    q = (q_ref[0] * scale).astype(BF16)

    def visit(kt, vt, mask, state):
        car, acc = state
        z = jnp.einsum('htd,hds->hts', q, kt.astype(BF16), preferred_element_type=F32)
        att, car = _sb_weights(z.reshape(rows, page), tri, mask, car)
        return car, acc + jnp.einsum('hts,hes->hte', att.reshape(nh, tlen, page).astype(BF16),
                                     vt.astype(BF16), preferred_element_type=F32)

    @pl.when(j == 0)
    def _():
        kpad_ref[...] = jnp.zeros(kpad_ref.shape, F32)
        vpad_ref[...] = jnp.zeros(vpad_ref.shape, F32)
        kpad_ref[:, :, :tnew] = knt_ref[0]
        vpad_ref[:, :, :tnew] = vnt_ref[0]
        t = lax.broadcasted_iota(jnp.int32, (rows, page), 0) % tlen
        s = lax.broadcasted_iota(jnp.int32, (rows, page), 1)
        car_ref[...], acc_ref[...] = visit(kpad_ref[...], vpad_ref[...], s < t,
                                           (jnp.zeros(car_ref.shape, F32), jnp.zeros(acc_ref.shape, F32)))

    state = (car_ref[...], acc_ref[...])
    for i in range(pages_per_step):
        state = visit(k_refs[i][0], v_refs[i][0], None, state)
    car_ref[...], acc_ref[...] = state

    @pl.when(j == pl.num_programs(1) - 1)
    def _():
        o_ref[0] = acc_ref[...]


def _sb_sample(page_table, q4, knt, vnt, cache_kt, cache_vt, pages_per_step):
    bsz, nh, tlen, hd = q4.shape
    npages = page_table.shape[1]
    page = cache_kt.shape[3]
    seq = lambda a: pl.BlockSpec((1,) + a.shape[1:], lambda b, j, pt: (b, 0, 0, 0))
    pages = _page_specs(page_table, pages_per_step, (1, nh, hd, page))
    grid_spec = pltpu.PrefetchScalarGridSpec(
        num_scalar_prefetch=1,
        grid=(bsz, npages // pages_per_step),
        in_specs=[seq(q4), seq(knt), seq(vnt)] + pages + pages,
        out_specs=seq(q4),
        scratch_shapes=[pltpu.VMEM((nh, hd, page), F32), pltpu.VMEM((nh, hd, page), F32),
                        pltpu.VMEM((nh, tlen, hd), F32), pltpu.VMEM((nh * tlen, 1), F32)])
    return pl.pallas_call(
        functools.partial(_sb_sample_kernel, pages_per_step=pages_per_step, scale=HEAD64 ** -0.5),
        grid_spec=grid_spec,
        out_shape=jax.ShapeDtypeStruct(q4.shape, F32),
        compiler_params=_cp(("parallel", "arbitrary")),
        name="sb_sample",
    )(page_table, q4, knt, vnt, *([cache_kt] * pages_per_step), *([cache_vt] * pages_per_step))


def _diff_sample_kernel(pt_ref, q_ref, knt_ref, vn_ref, slope_ref, lqk_ref, sub_ref, *refs,
                        pages_per_step, scale, lam_init, past_len):
    del pt_ref
    k_refs, v_refs = refs[:pages_per_step], refs[pages_per_step:2 * pages_per_step]
    o_ref, kpad_ref, vpad_ref, acc_ref, m_ref, l_ref = refs[2 * pages_per_step:]
    j = pl.program_id(1)
    nmap, tlen, _ = q_ref.shape[1:]
    nh = nmap // 2
    tnew = knt_ref.shape[3]
    page = kpad_ref.shape[2]
    npages = pl.num_programs(1) * pages_per_step
    rows = nmap * tlen
    t = lax.broadcasted_iota(jnp.int32, (rows, page), 0) % tlen
    s = lax.broadcasted_iota(jnp.int32, (rows, page), 1)
    slope = slope_ref[...]
    q = (q_ref[0] * scale).astype(BF16)

    wide = page * nh
    spread = (lax.broadcasted_iota(jnp.int32, (page, wide), 1) // nh
              == lax.broadcasted_iota(jnp.int32, (page, wide), 0)).astype(BF16)
    own_head = (lax.broadcasted_iota(jnp.int32, (rows, wide), 1) % nh
                == lax.broadcasted_iota(jnp.int32, (rows, wide), 0) // (2 * tlen))

    def visit(kt, v2, bias, state):
        m_old, l_old, acc = state
        sc = jnp.einsum('rtd,rds->rts', q, kt.astype(BF16), preferred_element_type=F32)
        sc = sc.reshape(rows, page) + bias
        m_new = jnp.maximum(m_old, jnp.max(sc, axis=-1, keepdims=True))
        p = jnp.exp(sc - m_new)
        corr = jnp.exp(m_old - m_new)
        pw = jnp.where(own_head, _dot(p.astype(BF16), spread), 0.0).astype(BF16)
        return (m_new, corr * l_old + jnp.sum(p, axis=-1, keepdims=True),
                corr * acc + _dot(pw, v2.reshape(wide, v2.shape[-1]).astype(BF16)))

    @pl.when(j == 0)
    def _():
        kpad_ref[...] = jnp.zeros(kpad_ref.shape, F32)
        vpad_ref[...] = jnp.zeros(vpad_ref.shape, F32)
        kpad_ref[:, :, :tnew] = knt_ref[0]
        vpad_ref[:tnew] = vn_ref[0]
        bias = jnp.where(s <= t, -slope * (t - s).astype(F32), NEG_INF)
        init = (jnp.full(m_ref.shape, NEG_INF, F32), jnp.zeros(l_ref.shape, F32),
                jnp.zeros(acc_ref.shape, F32))
        m_ref[...], l_ref[...], acc_ref[...] = visit(kpad_ref[...], vpad_ref[...], bias, init)

    state = (m_ref[...], l_ref[...], acc_ref[...])
    for i in range(pages_per_step):
        kpos = (npages - 1 - (j * pages_per_step + i)) * page + s
        bias = -slope * (past_len + t - kpos).astype(F32)
        state = visit(k_refs[i][0], v_refs[i][0], bias, state)
    m_ref[...], l_ref[...], acc_ref[...] = state

    @pl.when(j == pl.num_programs(1) - 1)
    def _():
        lam = _diff_lambda(lqk_ref, lam_init)
        o3 = (acc_ref[...] / l_ref[...]).reshape(nh, 2 * tlen, acc_ref.shape[1])
        for h in range(nh):
            o = o3[h, :tlen] - lam * o3[h, tlen:]
            o_ref[0, h] = _rms_rows(o, sub_ref[...]) * (1.0 - lam_init)


def _diff_sample(page_table, q4, knt, vn4, cache_kt, cache_v, slopes, lqk, subln, lam_init, pages_per_step):
    bsz, nmap, tlen, hd = q4.shape
    npages = page_table.shape[1]
    page = cache_kt.shape[3]
    nheads, dv = cache_v.shape[2], cache_v.shape[3]
    rows = nmap * tlen
    slope_rows = jnp.broadcast_to(jnp.repeat(slopes, rows // nheads)[:, None], (rows, page))
    seq = lambda a: pl.BlockSpec((1,) + a.shape[1:], lambda b, j, pt: (b, 0, 0, 0))
    const2 = lambda shape: pl.BlockSpec(shape, lambda b, j, pt: (0, 0))
    grid_spec = pltpu.PrefetchScalarGridSpec(
        num_scalar_prefetch=1,
        grid=(bsz, npages // pages_per_step),
        in_specs=([seq(q4), seq(knt), seq(vn4), const2((rows, page)), const2(lqk.shape), const2((1, dv))]
                  + _page_specs(page_table, pages_per_step, (1, nmap, hd, page))
                  + _page_specs(page_table, pages_per_step, (1, page, nheads, dv))),
        out_specs=pl.BlockSpec((1, nheads, tlen, dv), lambda b, j, pt: (b, 0, 0, 0)),
        scratch_shapes=[pltpu.VMEM((nmap, hd, page), F32), pltpu.VMEM((page, nheads, dv), F32),
                        pltpu.VMEM((rows, dv), F32),
                        pltpu.VMEM((rows, 1), F32), pltpu.VMEM((rows, 1), F32)])
    return pl.pallas_call(
        functools.partial(_diff_sample_kernel, pages_per_step=pages_per_step, scale=HEAD64 ** -0.5,
                          lam_init=lam_init, past_len=npages * page),
        grid_spec=grid_spec,
        out_shape=jax.ShapeDtypeStruct((bsz, nheads, tlen, dv), F32),
        compiler_params=_cp(("parallel", "arbitrary")),
        name="diff_sample",
    )(page_table, q4, knt, vn4, slope_rows, lqk, subln.astype(F32).reshape(1, dv),
      *([cache_kt] * pages_per_step), *([cache_v] * pages_per_step))


def _largest_tile(m, cap, mult=16):
    t = min(m, cap)
    while m % t or t % mult:
        t -= mult
    return t


def _layer0(x, bsz, tlen, n_valid, s5_re0, s5_im0, rwkv_s0, shift_prev, wt):
    m = x.shape[0]
    tm = _largest_tile(m, 640)
    u = _proj(x, wt['norm_mix0'], wt['w_in0_u'], tm, 512)
    p = _proj(x, wt['norm_mix0'], wt['w_in0_p'], tm, 1152)

    bu_re, bu_im = _s5_b(u, wt['s5_wb_re'], wt['s5_wb_im'], tm)
    width = bu_re.shape[1]
    nb_s5, tb_s5 = (1, LANE) if bsz == 1 else (_largest_tile(bsz, 16, 1), tlen)
    h_re, h_im, f_re, f_im = _s5_scan(bu_re.reshape(bsz, tlen, width), bu_im.reshape(bsz, tlen, width),
                                      s5_re0, s5_im0, wt['s5_lb_re'], wt['s5_lb_im'], nb_s5, tb_s5)
    if n_valid < tlen:
        f_re, f_im = h_re[:, n_valid - 1], h_im[:, n_valid - 1]
    ya = _s5_c(h_re.reshape(m, width), h_im.reshape(m, width), u, wt['s5_wc_re'], wt['s5_wc_im'],
               wt['s5_d'], wt['s5_w_glu'], _largest_tile(m, 256))

    c = p.shape[1]
    if bsz == 1:
        tr = _largest_tile(tlen, 320)
        prev = jnp.concatenate([shift_prev.astype(F32), p[tr - 1::tr][:-1]])[:, None, :]
        p3, nb = p.reshape(tlen // tr, tr, c), 1
    else:
        prev = shift_prev.astype(F32)[:, None, :]
        p3, nb = p.reshape(bsz, tlen, c), _largest_tile(bsz, 32, 1)
    r, lw, k, v, kk, kka, g = _rwkv_pre(p3, prev, wt['rwkv_mu'], wt['rwkv_w0'], wt['rwkv_w2'],
                                        wt['rwkv_a0'], wt['rwkv_a2'], wt['rwkv_g2'], wt['rwkv_k_k'],
                                        wt['rwkv_k_a'], nb, m if bsz > 1 else n_valid)
    w_r = r.shape[1]
    if bsz == 1:
        v3 = lambda a: a.reshape(bsz, tlen, w_r)
        y, s_last = _rwkv_scan(v3(r), v3(lw), v3(k), v3(v), v3(kk), v3(kka), rwkv_s0,
                               _largest_tile(tlen, 640, LANE), HEAD64)
        y = y.reshape(m, w_r)
    else:
        lanes_b = lambda a: jnp.transpose(a.reshape(bsz, tlen, w_r // HEAD64, HEAD64), (1, 2, 3, 0))
        y, s_last = _rwkv_steps(lanes_b(r), lanes_b(lw), lanes_b(k), lanes_b(v), lanes_b(kk), lanes_b(kka),
                                jnp.transpose(rwkv_s0.astype(F32), (1, 2, 3, 0)))
        y = jnp.transpose(y, (3, 0, 1, 2)).reshape(m, w_r)
        s_last = jnp.transpose(s_last, (3, 0, 1, 2))
    yb = _rwkv_post(y, r, k, v, g, wt['rwkv_ln_w'], wt['rwkv_ln_b'], wt['rwkv_r_k'], _largest_tile(m, 512))

    x = _outproj(x, ya, yb, wt['w_out0_a'], wt['w_out0_b'], tm, 512)
    x = _ffn(x, wt['norm_ffn0'], wt['ffn_w1'], wt['ffn_w3'], wt['ffn_w2'], tm, 512)
    return x, f_re, f_im, s_last, p


def kernel(x_prompt, x_sample, state_s5_re, state_s5_im, state_rwkv, state_rwkv_shift, cache_sb_k, cache_sb_v, cache_diff_k, cache_diff_v, page_table, meta_tokens, norm_mix0, w_in0, s5_lambda_re, s5_lambda_im, s5_log_dt, s5_b_re, s5_b_im, s5_c_re, s5_c_im, s5_d, s5_w_glu, rwkv_mu, rwkv_w0, rwkv_w2, rwkv_a0, rwkv_a2, rwkv_g2, rwkv_k_k, rwkv_k_a, rwkv_r_k, rwkv_ln_w, rwkv_ln_b, w_out0, norm_ffn0, ffn_w1, ffn_w3, ffn_w2, norm_mix1, w_in1, diff_q_norm, diff_k_norm, diff_lambda_q1, diff_lambda_k1, diff_lambda_q2, diff_lambda_k2, diff_subln, w_out1, norm_ffn1, moe_router, moe_w1, moe_w3, moe_w2):
    d_model = x_prompt.shape[-1]
    n_groups, n_state, grp = s5_b_re.shape
    s5_width = n_groups * grp
    rwkv_width = rwkv_w0.shape[0]
    rwkv_heads = rwkv_width // HEAD64
    sb_width = cache_sb_k.shape[2] * cache_sb_k.shape[3]
    diff_heads = cache_diff_v.shape[2]
    diff_qk_width = cache_diff_k.shape[2] * cache_diff_k.shape[3] * cache_diff_k.shape[4]
    diff_v_width = diff_heads * cache_diff_v.shape[3]
    lam_init = 0.8 - 0.6 * math.exp(-0.3)
    bf = lambda a: a.astype(BF16)

    lb_re, lb_im, bb_re, bb_im = _s5_discretize(s5_lambda_re, s5_lambda_im, s5_log_dt, s5_b_re, s5_b_im)
    to_ghp = lambda a: jnp.transpose(a, (1, 0, 2))
    wt = dict(
        norm_mix0=norm_mix0.astype(F32), w_in0_u=bf(w_in0[:, :s5_width]), w_in0_p=bf(w_in0[:, s5_width:]),
        s5_lb_re=lb_re, s5_lb_im=lb_im,
        s5_wb_re=bf(_blockdiag(to_ghp(bb_re), 8)), s5_wb_im=bf(_blockdiag(to_ghp(bb_im), 8)),
        s5_wc_re=bf(_blockdiag(jnp.transpose(s5_c_re.astype(F32), (0, 2, 1)), 16)),
        s5_wc_im=bf(_blockdiag(jnp.transpose(s5_c_im.astype(F32), (0, 2, 1)), 16)),
        s5_d=s5_d.reshape(-1), s5_w_glu=bf(s5_w_glu),
        rwkv_mu=rwkv_mu, rwkv_w0=rwkv_w0, rwkv_w2=bf(rwkv_w2), rwkv_a0=rwkv_a0, rwkv_a2=bf(rwkv_a2),
        rwkv_g2=bf(rwkv_g2), rwkv_k_k=rwkv_k_k, rwkv_k_a=rwkv_k_a, rwkv_r_k=rwkv_r_k,
        rwkv_ln_w=rwkv_ln_w, rwkv_ln_b=rwkv_ln_b,
        w_out0_a=bf(w_out0[:s5_width]), w_out0_b=bf(w_out0[s5_width:]),
        norm_ffn0=norm_ffn0.astype(F32), ffn_w1=bf(ffn_w1), ffn_w3=bf(ffn_w3), ffn_w2=bf(ffn_w2),
    )
    w_in1_b = bf(w_in1)
    w_out1_a, w_out1_b = bf(w_out1[:sb_width]), bf(w_out1[sb_width:])
    moe_w1_b, moe_w3_b, moe_w2_b = bf(moe_w1), bf(moe_w3), bf(moe_w2)
    lqk = jnp.stack([diff_lambda_q1, diff_lambda_k1, diff_lambda_q2, diff_lambda_k2]).astype(F32)
    slopes = _diff_slopes(diff_heads)
    q_col = 3 * sb_width // diff_qk_width
    npair = sb_width // LANE

    bp, seq, _ = x_prompt.shape
    assert bp == 1
    real = N_META + seq
    lp = real + (-real) % LANE
    xp = jnp.concatenate([meta_tokens.astype(F32), x_prompt[0].astype(F32), jnp.zeros((lp - real, d_model), F32)])
    tr_p = _largest_tile(lp, 640)
    tm_p = tr_p
    c_proj = w_in0.shape[1] - s5_width
    z_s5 = jnp.zeros((1, n_groups, n_state), F32)
    z_rwkv = jnp.zeros((1, rwkv_heads, HEAD64, HEAD64), F32)
    xh, p_s5_re, p_s5_im, p_rwkv, p_proj = _layer0(xp, 1, lp, real, z_s5, z_s5, z_rwkv,
                                                    jnp.zeros((1, c_proj), F32), wt)
    p_rwkv_shift = p_proj[real - 1:real]

    proj1 = _proj(xh, norm_mix1.astype(F32), w_in1_b, tm_p, 512, split=True)
    blk_dq = 3 * npair
    blk_dk = blk_dq + diff_heads
    blk_dv = blk_dk + diff_heads
    dqn = _qknorm_blocks(proj1, blk_dq, diff_heads, diff_q_norm, tr_p)
    dkn = _qknorm_blocks(proj1, blk_dk, diff_heads, diff_k_norm, tr_p)
    tq = _largest_tile(lp, 640, LANE)
    y_sb = _sb_prompt(proj1, lp, LANE, tq)
    y_d = _diff_prompt(dqn, dkn, proj1, blk_dv, slopes, lqk, diff_subln, lp, LANE, tq, lam_init)
    xh = _outproj(xh, y_sb, y_d, w_out1_a, w_out1_b, tm_p, 512)
    xh = _moe(xh, norm_ffn1.astype(F32), moe_router, moe_w1_b, moe_w3_b, moe_w2_b, tm_p, 256)
    y_prompt = xh[N_META:real][None]

    def tokens(blocks):
        return jnp.transpose(blocks[:, :real], (1, 0, 2)).reshape(1, real, -1)

    p_sb_k = tokens(proj1[npair:2 * npair]).reshape(1, real, sb_width // HEAD64, HEAD64)
    p_sb_v = tokens(proj1[2 * npair:3 * npair]).reshape(1, real, sb_width // HEAD64, HEAD64)
    p_diff_k = tokens(dkn).reshape(1, real, diff_heads, 2, HEAD64)
    p_diff_v = tokens(proj1[blk_dv:blk_dv + diff_heads]).reshape(1, real, diff_heads, diff_v_width // diff_heads)

    db, dseq, _ = x_sample.shape
    ms = db * dseq
    xs = x_sample.astype(F32).reshape(ms, d_model)
    xs, s_s5_re, s_s5_im, s_rwkv, s_proj = _layer0(xs, db, dseq, dseq, state_s5_re, state_s5_im, state_rwkv,
                                                   state_rwkv_shift, wt)
    s_rwkv_shift = s_proj.reshape(db, dseq, c_proj)[:, -1]

    tm_s = _largest_tile(ms, 640)
    proj_s = _proj(xs, norm_mix1.astype(F32), w_in1_b, tm_s, 512)
    dqn_s = _qknorm_cols(proj_s, q_col, diff_qk_width, diff_q_norm, tm_s)
    dkn_s = _qknorm_cols(proj_s, q_col + 1, diff_qk_width, diff_k_norm, tm_s)
    n_pool, page = cache_sb_k.shape[0], cache_sb_k.shape[1]
    pt = page_table.astype(jnp.int32)
    pages_per_step = 4 if pt.shape[1] % 4 == 0 else 1
    v_col = (3 * sb_width + 2 * diff_qk_width) // diff_v_width
    heads = lambda a: a.reshape(db, dseq, -1, HEAD64)
    q_first = lambda a: jnp.transpose(heads(a), (0, 2, 1, 3))
    t_last = lambda a: jnp.transpose(heads(a), (0, 2, 3, 1))
    slot_last = lambda c: jnp.transpose(c.reshape(n_pool, page, -1, HEAD64), (0, 2, 3, 1))
    ys_sb = _sb_sample(pt, q_first(proj_s[:, :sb_width]), t_last(proj_s[:, sb_width:2 * sb_width]),
                       t_last(proj_s[:, 2 * sb_width:3 * sb_width]), slot_last(cache_sb_k),
                       slot_last(cache_sb_v), pages_per_step)
    ys_d = _diff_sample(pt, q_first(dqn_s), t_last(dkn_s),
                        proj_s[:, v_col * diff_v_width:].reshape(db, dseq, diff_heads, -1),
                        slot_last(cache_diff_k), cache_diff_v, slopes, lqk, diff_subln, lam_init,
                        pages_per_step)
    tokens_s = lambda a: jnp.transpose(a, (0, 2, 1, 3)).reshape(ms, -1)
    xs = _outproj(xs, tokens_s(ys_sb), tokens_s(ys_d), w_out1_a, w_out1_b, tm_s, 512)
    xs = _moe(xs, norm_ffn1.astype(F32), moe_router, moe_w1_b, moe_w3_b, moe_w2_b, tm_s, 256)
    y_sample = xs.reshape(db, dseq, d_model)

    s_sb_k = proj_s[:, sb_width:2 * sb_width].reshape(db, dseq, sb_width // HEAD64, HEAD64)
    s_sb_v = proj_s[:, 2 * sb_width:3 * sb_width].reshape(db, dseq, sb_width // HEAD64, HEAD64)
    s_diff_k = dkn_s.reshape(db, dseq, diff_heads, 2, HEAD64)
    s_diff_v = proj_s[:, v_col * diff_v_width:].reshape(db, dseq, diff_heads, diff_v_width // diff_heads)

    return (y_prompt, y_sample,
            p_s5_re.reshape(1, n_groups, n_state), p_s5_im.reshape(1, n_groups, n_state), p_rwkv, p_rwkv_shift,
            p_sb_k, p_sb_v, p_diff_k, p_diff_v,
            s_s5_re.reshape(db, n_groups, n_state), s_s5_im.reshape(db, n_groups, n_state), s_rwkv, s_rwkv_shift,
            s_sb_k, s_sb_v, s_diff_k, s_diff_v)
```

```python
import functools
import math

import jax
import jax.numpy as jnp
from jax import lax
from jax.experimental import pallas as pl
from jax.experimental.pallas import tpu as pltpu

F32 = jnp.float32
BF16 = jnp.bfloat16
HIGHEST = lax.Precision.HIGHEST

LANE = 128
HEAD64 = 64
RMS_EPS = 1e-6
RWKV_LN_EPS = 64e-5
N_META = 16
NEG_INF = -1e30
LOG2E = math.log2(math.e)
VMEM_LIMIT = 56 * 1024 * 1024


def _cp(sem, vmem=VMEM_LIMIT):
    return pltpu.CompilerParams(dimension_semantics=sem, vmem_limit_bytes=vmem)


def _dot(a, b, precision=None):
    return jnp.dot(a, b, preferred_element_type=F32, precision=precision)


def _dot_nt(a, b, precision=None):
    return lax.dot_general(a, b, (((1,), (1,)), ((), ())), preferred_element_type=F32,
                           precision=precision)


def _dot_tn(a, b, precision=None):
    return lax.dot_general(a, b, (((0,), (0,)), ((), ())), preferred_element_type=F32,
                           precision=precision)


def _split_bf16(x):
    hi = x.astype(BF16)
    lo = (x - hi.astype(F32)).astype(BF16)
    return hi, lo


_NN = ((1,), (0,))
_NT = ((1,), (1,))
_TN = ((0,), (0,))


def _mm3(a, b, dims):
    dg = lambda x, y: lax.dot_general(x, y, (dims, ((), ())), preferred_element_type=F32)
    return dg(a[0], b[0]) + dg(a[0], b[1]) + dg(a[1], b[0])


def _rms_rows(x, gain):
    ms = jnp.mean(x * x, axis=-1, keepdims=True)
    return x * lax.rsqrt(ms + RMS_EPS) * gain


def _softplus(t):
    return jnp.maximum(t, 0.0) + jnp.log1p(jnp.exp(-jnp.abs(t)))


def _segsum64(x, e128):
    cols = [_dot(x[:, c * LANE:(c + 1) * LANE], e128, HIGHEST) for c in range(x.shape[1] // LANE)]
    return cols[0] if len(cols) == 1 else jnp.concatenate(cols, axis=1)


def _seg_ones():
    r = lax.broadcasted_iota(jnp.int32, (LANE, LANE), 0) // HEAD64
    c = lax.broadcasted_iota(jnp.int32, (LANE, LANE), 1) // HEAD64
    return (r == c).astype(F32)


def _proj_kernel(x_ref, g_ref, w_ref, o_ref, xn_ref, *, split):
    @pl.when(pl.program_id(1) == 0)
    def _():
        xn_ref[...] = _rms_rows(x_ref[...], g_ref[...]).astype(BF16)

    acc = _dot(xn_ref[...], w_ref[...])
    if split:
        for c in range(acc.shape[1] // LANE):
            o_ref[c] = acc[:, c * LANE:(c + 1) * LANE]
    else:
        o_ref[...] = acc


def _proj(x, gain, w, tm, tn, split=False):
    m, d = x.shape
    n = w.shape[1]
    if split:
        out_shape = jax.ShapeDtypeStruct((n // LANE, m, LANE), F32)
        out_spec = pl.BlockSpec((tn // LANE, tm, LANE), lambda i, j: (j, i, 0))
    else:
        out_shape = jax.ShapeDtypeStruct((m, n), F32)
        out_spec = pl.BlockSpec((tm, tn), lambda i, j: (i, j))
    return pl.pallas_call(
        functools.partial(_proj_kernel, split=split),
        grid=(m // tm, n // tn),
        in_specs=[pl.BlockSpec((tm, d), lambda i, j: (i, 0)),
                  pl.BlockSpec((1, d), lambda i, j: (0, 0)),
                  pl.BlockSpec((d, tn), lambda i, j: (0, j))],
        out_specs=out_spec,
        out_shape=out_shape,
        scratch_shapes=[pltpu.VMEM((tm, d), BF16)],
        compiler_params=_cp(("parallel", "arbitrary")),
        name="proj",
    )(x, gain.reshape(1, d), w)


def _outproj_kernel(res_ref, a_ref, b_ref, wa_ref, wb_ref, o_ref):
    o_ref[...] = (res_ref[...] + _dot(a_ref[...].astype(BF16), wa_ref[...])
                  + _dot(b_ref[...].astype(BF16), wb_ref[...]))


def _outproj(res, a, b, wa, wb, tm, tn):
    m, n = res.shape
    ka, kb = a.shape[1], b.shape[1]
    return pl.pallas_call(
        _outproj_kernel,
        grid=(m // tm, n // tn),
        in_specs=[pl.BlockSpec((tm, tn), lambda i, j: (i, j)),
                  pl.BlockSpec((tm, ka), lambda i, j: (i, 0)),
                  pl.BlockSpec((tm, kb), lambda i, j: (i, 0)),
                  pl.BlockSpec((ka, tn), lambda i, j: (0, j)),
                  pl.BlockSpec((kb, tn), lambda i, j: (0, j))],
        out_specs=pl.BlockSpec((tm, tn), lambda i, j: (i, j)),
        out_shape=jax.ShapeDtypeStruct((m, n), F32),
        compiler_params=_cp(("parallel", "arbitrary")),
        name="outproj",
    )(res, a, b, wa, wb)


def _ffn_kernel(x_ref, g_ref, w1_ref, w3_ref, w2_ref, o_ref, xn_ref):
    @pl.when(pl.program_id(1) == 0)
    def _():
        x = x_ref[...]
        xn_ref[...] = _rms_rows(x, g_ref[...]).astype(BF16)
        o_ref[...] = x

    xn = xn_ref[...]
    h1 = _dot(xn, w1_ref[...])
    h3 = _dot(xn, w3_ref[...])
    h = (h1 * jax.nn.sigmoid(h1) * h3).astype(BF16)
    o_ref[...] += _dot(h, w2_ref[...])


def _ffn(x, gain, w1, w3, w2, tm, tf):
    m, d = x.shape
    f = w1.shape[1]
    return pl.pallas_call(
        _ffn_kernel,
        grid=(m // tm, f // tf),
        in_specs=[pl.BlockSpec((tm, d), lambda i, j: (i, 0)),
                  pl.BlockSpec((1, d), lambda i, j: (0, 0)),
                  pl.BlockSpec((d, tf), lambda i, j: (0, j)),
                  pl.BlockSpec((d, tf), lambda i, j: (0, j)),
                  pl.BlockSpec((tf, d), lambda i, j: (j, 0))],
        out_specs=pl.BlockSpec((tm, d), lambda i, j: (i, 0)),
        out_shape=jax.ShapeDtypeStruct((m, d), F32),
        scratch_shapes=[pltpu.VMEM((tm, d), BF16)],
        compiler_params=_cp(("parallel", "arbitrary")),
        name="ffn",
    )(x, gain.reshape(1, d), w1, w3, w2)


def _moe_kernel(x_ref, g_ref, r_ref, w1_ref, w3_ref, w2_ref, o_ref, xn_ref, gate_ref, *, n_experts):
    e = pl.program_id(1)
    j = pl.program_id(2)
    tm = x_ref.shape[0]
    lane = lax.broadcasted_iota(jnp.int32, (tm, LANE), 1)

    @pl.when((e == 0) & (j == 0))
    def _():
        x = x_ref[...]
        xn = _rms_rows(x, g_ref[...])
        xn_ref[...] = xn.astype(BF16)
        o_ref[...] = x
        logits = jnp.where(lane < n_experts, _dot(xn, r_ref[...], HIGHEST), NEG_INF)
        v1 = jnp.max(logits, axis=-1, keepdims=True)
        i1 = jnp.min(jnp.where(logits == v1, lane, LANE), axis=-1, keepdims=True)
        rest = jnp.where(lane == i1, NEG_INF, logits)
        v2 = jnp.max(rest, axis=-1, keepdims=True)
        i2 = jnp.min(jnp.where(rest == v2, lane, LANE), axis=-1, keepdims=True)
        ex = jnp.exp(v2 - v1)
        den = 1.0 + ex
        gate_ref[...] = jnp.where(lane == i1, 1.0 / den, 0.0) + jnp.where(lane == i2, ex / den, 0.0)

    ge = jnp.sum(jnp.where(lane == e, gate_ref[...], 0.0), axis=-1, keepdims=True)
    xn = xn_ref[...]
    h1 = _dot(xn, w1_ref[0])
    h3 = _dot(xn, w3_ref[0])
    h = (h1 * jax.nn.sigmoid(h1) * h3).astype(BF16)
    o_ref[...] += ge * _dot(h, w2_ref[0])


def _moe(x, gain, router, w1, w3, w2, tm, tf):
    m, d = x.shape
    ne, _, f = w1.shape
    router_p = jnp.zeros((d, LANE), F32).at[:, :ne].set(router.astype(F32))
    return pl.pallas_call(
        functools.partial(_moe_kernel, n_experts=ne),
        grid=(m // tm, ne, f // tf),
        in_specs=[pl.BlockSpec((tm, d), lambda i, e, j: (i, 0)),
                  pl.BlockSpec((1, d), lambda i, e, j: (0, 0)),
                  pl.BlockSpec((d, LANE), lambda i, e, j: (0, 0)),
                  pl.BlockSpec((1, d, tf), lambda i, e, j: (e, 0, j)),
                  pl.BlockSpec((1, d, tf), lambda i, e, j: (e, 0, j)),
                  pl.BlockSpec((1, tf, d), lambda i, e, j: (e, j, 0))],
        out_specs=pl.BlockSpec((tm, d), lambda i, e, j: (i, 0)),
        out_shape=jax.ShapeDtypeStruct((m, d), F32),
        scratch_shapes=[pltpu.VMEM((tm, d), BF16), pltpu.VMEM((tm, LANE), F32)],
        compiler_params=_cp(("parallel", "arbitrary", "arbitrary")),
        name="moe",
    )(x, gain.reshape(1, d), router_p, w1, w3, w2)


def _s5_disc_kernel(lr_ref, li_ref, ldt_ref, bre_ref, bim_ref, lbre_ref, lbim_ref, bbre_ref, bbim_ref):
    lr = lr_ref[...]
    li = li_ref[...]
    dt = jnp.exp(ldt_ref[...])
    mag = jnp.exp(lr * dt)
    lb_re = mag * jnp.cos(li * dt)
    lb_im = mag * jnp.sin(li * dt)
    den = lr * lr + li * li
    f_re = ((lb_re - 1.0) * lr + lb_im * li) / den
    f_im = (lb_im * lr - (lb_re - 1.0) * li) / den
    lbre_ref[...] = lb_re
    lbim_ref[...] = lb_im
    for h in range(bre_ref.shape[0]):
        br = bre_ref[h]
        bi = bim_ref[h]
        bbre_ref[h] = f_re * br - f_im * bi
        bbim_ref[h] = f_re * bi + f_im * br


def _s5_discretize(lam_re, lam_im, log_dt, b_re, b_im):
    g, p, h = b_re.shape
    sds = jax.ShapeDtypeStruct
    return pl.pallas_call(
        _s5_disc_kernel,
        out_shape=(sds((g, p), F32), sds((g, p), F32), sds((h, g, p), F32), sds((h, g, p), F32)),
        name="s5_disc",
    )(lam_re.astype(F32), lam_im.astype(F32), log_dt.astype(F32).reshape(g, 1),
      jnp.transpose(b_re.astype(F32), (2, 0, 1)), jnp.transpose(b_im.astype(F32), (2, 0, 1)))


def _s5_b_kernel(u_ref, wre_ref, wim_ref, ore_ref, oim_ref):
    u = u_ref[...].astype(BF16)
    ore_ref[...] = _dot(u, wre_ref[0])
    oim_ref[...] = _dot(u, wim_ref[0])


def _s5_b(u, wre, wim, tm):
    m = u.shape[0]
    nc, kc, ncol = wre.shape
    spec_o = pl.BlockSpec((tm, ncol), lambda i, c: (i, c))
    spec_w = pl.BlockSpec((1, kc, ncol), lambda i, c: (c, 0, 0))
    sds = jax.ShapeDtypeStruct((m, nc * ncol), F32)
    return pl.pallas_call(
        _s5_b_kernel,
        grid=(m // tm, nc),
        in_specs=[pl.BlockSpec((tm, kc), lambda i, c: (i, c)), spec_w, spec_w],
        out_specs=(spec_o, spec_o),
        out_shape=(sds, sds),
        compiler_params=_cp(("parallel", "arbitrary")),
        name="s5_b",
    )(u, wre, wim)


def _s5_scan_kernel(bre_ref, bim_ref, h0re_ref, h0im_ref, lre_ref, lim_ref,
                    hre_ref, him_ref, fre_ref, fim_ref, cre_ref, cim_ref):
    nb, tlen = bre_ref.shape[0], bre_ref.shape[1]

    @pl.when(pl.program_id(1) == 0)
    def _():
        cre_ref[...] = h0re_ref[...]
        cim_ref[...] = h0im_ref[...]

    lre = lre_ref[...]
    lim = lim_ref[...]

    def per_seq(b, _):
        def step(t, carry):
            hr, hi = carry
            nr = lre * hr - lim * hi + bre_ref[b, t]
            ni = lre * hi + lim * hr + bim_ref[b, t]
            hre_ref[b, t] = nr
            him_ref[b, t] = ni
            return nr, ni

        hr, hi = lax.fori_loop(0, tlen, step, (cre_ref[b], cim_ref[b]), unroll=8)
        cre_ref[b] = hr
        cim_ref[b] = hi
        return 0

    lax.fori_loop(0, nb, per_seq, 0)
    fre_ref[...] = cre_ref[...]
    fim_ref[...] = cim_ref[...]


def _s5_scan(bu_re, bu_im, h0_re, h0_im, lb_re, lb_im, nb, tb):
    bsz, length, width = bu_re.shape
    sub = width // LANE
    v4 = lambda a: a.reshape(bsz, length, sub, LANE)
    v3 = lambda a: a.astype(F32).reshape(bsz, sub, LANE)
    spec_x = pl.BlockSpec((nb, tb, sub, LANE), lambda b, t: (b, t, 0, 0))
    spec_h = pl.BlockSpec((nb, sub, LANE), lambda b, t: (b, 0, 0))
    spec_l = pl.BlockSpec((sub, LANE), lambda b, t: (0, 0))
    sds_x = jax.ShapeDtypeStruct((bsz, length, sub, LANE), F32)
    sds_h = jax.ShapeDtypeStruct((bsz, sub, LANE), F32)
    hre, him, fre, fim = pl.pallas_call(
        _s5_scan_kernel,
        grid=(bsz // nb, length // tb),
        in_specs=[spec_x, spec_x, spec_h, spec_h, spec_l, spec_l],
        out_specs=(spec_x, spec_x, spec_h, spec_h),
        out_shape=(sds_x, sds_x, sds_h, sds_h),
        scratch_shapes=[pltpu.VMEM((nb, sub, LANE), F32), pltpu.VMEM((nb, sub, LANE), F32)],
        compiler_params=_cp(("parallel", "arbitrary")),
        name="s5_scan",
    )(v4(bu_re), v4(bu_im), v3(h0_re), v3(h0_im), lb_re.reshape(sub, LANE), lb_im.reshape(sub, LANE))
    return hre.reshape(bsz, length, width), him.reshape(bsz, length, width), fre, fim


def _s5_c_kernel(hre_ref, him_ref, u_ref, wcre_ref, wcim_ref, d_ref, wglu_ref, o_ref):
    nc, kc, _ = wcre_ref.shape
    cols = []
    for c in range(nc):
        hr = hre_ref[:, c * kc:(c + 1) * kc].astype(BF16)
        hi = him_ref[:, c * kc:(c + 1) * kc].astype(BF16)
        cols.append(_dot(hr, wcre_ref[c]) - _dot(hi, wcim_ref[c]))
    y = jnp.concatenate(cols, axis=1) + d_ref[...] * u_ref[...]
    z = jax.nn.gelu(y)
    o_ref[...] = z * jax.nn.sigmoid(_dot(z.astype(BF16), wglu_ref[...]))


def _s5_c(h_re, h_im, u, wcre, wcim, d_skip, w_glu, tm):
    m, width = h_re.shape
    w = u.shape[1]
    full3 = lambda a: pl.BlockSpec(a.shape, lambda i: (0, 0, 0))
    return pl.pallas_call(
        _s5_c_kernel,
        grid=(m // tm,),
        in_specs=[pl.BlockSpec((tm, width), lambda i: (i, 0)),
                  pl.BlockSpec((tm, width), lambda i: (i, 0)),
                  pl.BlockSpec((tm, w), lambda i: (i, 0)),
                  full3(wcre), full3(wcim),
                  pl.BlockSpec((1, w), lambda i: (0, 0)),
                  pl.BlockSpec((w, w), lambda i: (0, 0))],
        out_specs=pl.BlockSpec((tm, w), lambda i: (i, 0)),
        out_shape=jax.ShapeDtypeStruct((m, w), F32),
        compiler_params=_cp(("parallel",)),
        name="s5_c",
    )(h_re, h_im, u, wcre, wcim, d_skip.astype(F32).reshape(1, w), w_glu)


def _blockdiag(w, nblk):
    g, a, b = w.shape
    w4 = w.reshape(g // nblk, nblk, a, b)
    eye = jnp.eye(nblk, dtype=w.dtype)
    return jnp.einsum('cgab,gh->cgahb', w4, eye).reshape(g // nblk, nblk * a, nblk * b)


def _rwkv_pre_kernel(p_ref, prev_ref, mu_ref, w0_ref, w2_ref, a0_ref, a2_ref, g2_ref, kk_ref, ka_ref,
                     r_o, lw_o, k_o, v_o, kk_o, kka_o, g_o, *, width, lora_w, lora_a, n_valid):
    nb, tlen, c = p_ref.shape
    p = p_ref[...]
    tidx = lax.broadcasted_iota(jnp.int32, (nb, tlen, c), 1)
    prev = jnp.where(tidx == 0, prev_ref[...], pltpu.roll(p, 1, axis=1))
    xm = (p + (prev - p) * mu_ref[...]).reshape(nb * tlen, c)
    r = xm[:, :width]
    k = xm[:, width:2 * width]
    v = xm[:, 2 * width:3 * width]
    o = 3 * width
    w_lo = xm[:, o:o + lora_w]
    a_lo = xm[:, o + lora_w:o + lora_w + lora_a]
    g_lo = xm[:, o + lora_w + lora_a:]
    wraw = -_softplus(-(w0_ref[...] + _dot(jnp.tanh(w_lo).astype(BF16), w2_ref[...]))) - 0.5
    a = jax.nn.sigmoid(a0_ref[...] + _dot(a_lo.astype(BF16), a2_ref[...]))
    g = _dot(jax.nn.sigmoid(g_lo).astype(BF16), g2_ref[...])
    kkr = k * kk_ref[...]
    nrm = jnp.sqrt(_segsum64(kkr * kkr, _seg_ones()))
    kk = kkr / jnp.maximum(nrm, 1e-12)
    row = pl.program_id(0) * (nb * tlen) + lax.broadcasted_iota(jnp.int32, (nb * tlen, 1), 0)
    keep = (row < n_valid).astype(F32)
    r_o[...] = r
    lw_o[...] = -jnp.exp(wraw) * keep
    k_o[...] = k * (1.0 + (a - 1.0) * ka_ref[...]) * keep
    v_o[...] = v * keep
    kk_o[...] = kk * keep
    kka_o[...] = kk * a * keep
    g_o[...] = g


def _rwkv_pre(p3, prev, mu, w0, w2, a0, a2, g2, k_k, k_a, nb, n_valid):
    nbt, tlen, c = p3.shape
    width = w0.shape[0]
    lora_w, lora_a = w2.shape[0], a2.shape[0]
    rows = nb * tlen
    row1 = lambda a: a.astype(F32).reshape(1, -1)
    full2 = lambda a: pl.BlockSpec(a.shape, lambda i: (0, 0))
    spec_o = pl.BlockSpec((rows, width), lambda i: (i, 0))
    sds = jax.ShapeDtypeStruct((nbt * tlen, width), F32)
    args = (mu.astype(F32).reshape(1, 1, c), row1(w0), w2, row1(a0), a2, g2, row1(k_k), row1(k_a))
    return pl.pallas_call(
        functools.partial(_rwkv_pre_kernel, width=width, lora_w=lora_w, lora_a=lora_a, n_valid=n_valid),
        grid=(nbt // nb,),
        in_specs=[pl.BlockSpec((nb, tlen, c), lambda i: (i, 0, 0)),
                  pl.BlockSpec((nb, 1, c), lambda i: (i, 0, 0)),
                  pl.BlockSpec((1, 1, c), lambda i: (0, 0, 0))] + [full2(a) for a in args[1:]],
        out_specs=(spec_o,) * 7,
        out_shape=(sds,) * 7,
        compiler_params=_cp(("parallel",)),
        name="rwkv_pre",
    )(p3, prev, *args)


def _rwkv_scan_kernel(r_ref, lw_ref, k_ref, v_ref, kk_ref, kka_ref, s0_ref, y_ref, sl_ref, s_ref, *, tc):
    tb = r_ref.shape[1]
    npair = r_ref.shape[2] // LANE
    n2 = 2 * tc
    lane = lax.broadcasted_iota(jnp.int32, (tc, LANE), 1)
    first = lane < HEAD64
    ri = lax.broadcasted_iota(jnp.int32, (n2, n2), 0)
    ci = lax.broadcasted_iota(jnp.int32, (n2, n2), 1)
    same = (ri // tc) == (ci // tc)
    strict = same & ((ci % tc) < (ri % tc))
    incl = same & ((ci % tc) <= (ri % tc))
    eye = (ri == ci).astype(F32)
    trow = lax.broadcasted_iota(jnp.int32, (tc, LANE), 0)
    shifts = [1 << b for b in range(max(tc - 1, 0).bit_length())]
    n_sq = max(int(math.ceil(math.log2(tc))) - 1, 0)

    def stack(x):
        return jnp.concatenate([jnp.where(first, x, 0.0), jnp.where(first, 0.0, x)], axis=0)

    @pl.when(pl.program_id(1) == 0)
    def _():
        for pr in range(npair):
            sa = jnp.concatenate([s0_ref[0, 2 * pr], jnp.zeros((HEAD64, HEAD64), F32)], axis=1)
            sb = jnp.concatenate([jnp.zeros((HEAD64, HEAD64), F32), s0_ref[0, 2 * pr + 1]], axis=1)
            s_ref[pr] = jnp.concatenate([sa, sb], axis=0)

    def prepare(rows, pr):
        lanes = slice(pr * LANE, (pr + 1) * LANE)
        lw = lw_ref[0, rows, lanes]
        c = lw
        for sh in shifts:
            c = c + jnp.where(trow >= sh, pltpu.roll(c, sh, axis=0), 0.0)
        c_last = c[tc - 1:tc, :]
        kk = kk_ref[0, rows, lanes]
        kka = kka_ref[0, rows, lanes]
        kx = k_ref[0, rows, lanes]
        e_neg = jnp.exp(-c)
        e_end = jnp.exp(c_last - c)
        ops = (-kk * jnp.exp(c - lw), r_ref[0, rows, lanes] * jnp.exp(c), kka * e_neg, kx * e_neg,
               kka * e_end, kx * e_end, v_ref[0, rows, lanes])
        return tuple(_split_bf16(stack(x)) for x in ops) + (jnp.exp(c_last),)

    def chunk(ci_, _):
        rows = pl.ds(pl.multiple_of(ci_ * tc, tc), tc)
        prs = range(npair)
        al, rt, bt, kt, bh, kh, vs, w_end = zip(*[prepare(rows, pr) for pr in prs])
        nab = [jnp.where(strict, _mm3(al[p], bt[p], _NT), 0.0) for p in prs]
        aak = [jnp.where(strict, _mm3(al[p], kt[p], _NT), 0.0) for p in prs]
        arb = [jnp.where(incl, _mm3(rt[p], bt[p], _NT), 0.0) for p in prs]
        ark = [jnp.where(incl, _mm3(rt[p], kt[p], _NT), 0.0) for p in prs]
        inv = [eye + nab[p] for p in prs]
        pw = nab
        for _ in range(n_sq):
            pw_s = [_split_bf16(x) for x in pw]
            pw = [_mm3(pw_s[p], pw_s[p], _NN) for p in prs]
            inv = [inv[p] + _mm3(_split_bf16(inv[p]), _split_bf16(pw[p]), _NN) for p in prs]
        s0 = [s_ref[p] for p in prs]
        s0_s = [_split_bf16(x) for x in s0]
        rhs = [_mm3(al[p], s0_s[p], _NT) + _mm3(_split_bf16(aak[p]), vs[p], _NN) for p in prs]
        u = [_split_bf16(_mm3(_split_bf16(inv[p]), _split_bf16(rhs[p]), _NN)) for p in prs]
        for p in prs:
            ys = (_mm3(rt[p], s0_s[p], _NT) + _mm3(_split_bf16(arb[p]), u[p], _NN)
                  + _mm3(_split_bf16(ark[p]), vs[p], _NN))
            y_ref[0, rows, p * LANE:(p + 1) * LANE] = ys[:tc] + ys[tc:]
            s_ref[p] = s0[p] * w_end[p] + _mm3(u[p], bh[p], _TN) + _mm3(vs[p], kh[p], _TN)
        return 0

    lax.fori_loop(0, tb // tc, chunk, 0)

    for pr in range(npair):
        s = s_ref[pr]
        sl_ref[0, 2 * pr] = s[:HEAD64, :HEAD64]
        sl_ref[0, 2 * pr + 1] = s[HEAD64:, HEAD64:]


def _rwkv_scan(r, lw, k, v, kk, kka, s0, tb, tc):
    bsz, length, width = r.shape
    spec_x = pl.BlockSpec((1, tb, width), lambda b, t: (b, t, 0))
    spec_s = pl.BlockSpec((1,) + s0.shape[1:], lambda b, t: (b, 0, 0, 0))
    return pl.pallas_call(
        functools.partial(_rwkv_scan_kernel, tc=tc),
        grid=(bsz, length // tb),
        in_specs=[spec_x] * 6 + [spec_s],
        out_specs=(spec_x, spec_s),
        out_shape=(jax.ShapeDtypeStruct((bsz, length, width), F32),
                   jax.ShapeDtypeStruct(s0.shape, F32)),
        scratch_shapes=[pltpu.VMEM((width // LANE, LANE, LANE), F32)],
        compiler_params=_cp(("parallel", "arbitrary")),
        name="rwkv_scan",
    )(r, lw, k, v, kk, kka, s0.astype(F32))


def _rwkv_steps_kernel(r_ref, lw_ref, k_ref, v_ref, kk_ref, kka_ref, s0_ref, y_ref, sl_ref):
    tlen = r_ref.shape[0]
    n = r_ref.shape[2]
    sl_ref[...] = s0_ref[...]
    for t in range(tlen):
        w = jnp.exp(lw_ref[t, 0])
        kk = kk_ref[t, 0]
        kka = kka_ref[t, 0]
        kx = k_ref[t, 0]
        rx = r_ref[t, 0]

        def row(i, _):
            s = sl_ref[0, i]
            sa = -jnp.sum(s * kk, axis=0, keepdims=True)
            s = s * w + sa * kka + v_ref[t, 0, pl.ds(i, 1), :] * kx
            sl_ref[0, i] = s
            y_ref[t, 0, pl.ds(i, 1), :] = jnp.sum(s * rx, axis=0, keepdims=True)
            return 0

        lax.fori_loop(0, n, row, 0)


def _rwkv_steps(r, lw, k, v, kk, kka, s0):
    tlen, nh, n, bsz = r.shape
    spec_x = pl.BlockSpec((tlen, 1, n, bsz), lambda h: (0, h, 0, 0))
    spec_s = pl.BlockSpec((1, n, n, bsz), lambda h: (h, 0, 0, 0))
    return pl.pallas_call(
        _rwkv_steps_kernel,
        grid=(nh,),
        in_specs=[spec_x] * 6 + [spec_s],
        out_specs=(spec_x, spec_s),
        out_shape=(jax.ShapeDtypeStruct(r.shape, F32), jax.ShapeDtypeStruct(s0.shape, F32)),
        compiler_params=_cp(("parallel",)),
        name="rwkv_steps",
    )(r, lw, k, v, kk, kka, s0)


def _rwkv_post_kernel(y_ref, r_ref, k_ref, v_ref, g_ref, lnw_ref, lnb_ref, rk_ref, o_ref):
    e128 = _seg_ones()
    y = y_ref[...]
    mean = _segsum64(y, e128) * (1.0 / HEAD64)
    d = y - mean
    var = _segsum64(d * d, e128) * (1.0 / HEAD64)
    yn = d * lax.rsqrt(var + RWKV_LN_EPS) * lnw_ref[...] + lnb_ref[...]
    bonus = _segsum64(r_ref[...] * k_ref[...] * rk_ref[...], e128) * v_ref[...]
    o_ref[...] = (yn + bonus) * g_ref[...]


def _rwkv_post(y, r, k, v, g, ln_w, ln_b, r_k, tm):
    m, w = y.shape
    spec_x = pl.BlockSpec((tm, w), lambda i: (i, 0))
    spec_p = pl.BlockSpec((1, w), lambda i: (0, 0))
    row1 = lambda a: a.astype(F32).reshape(1, w)
    return pl.pallas_call(
        _rwkv_post_kernel,
        grid=(m // tm,),
        in_specs=[spec_x] * 5 + [spec_p] * 3,
        out_specs=spec_x,
        out_shape=jax.ShapeDtypeStruct((m, w), F32),
        compiler_params=_cp(("parallel",)),
        name="rwkv_post",
    )(y, r, k, v, g, row1(ln_w), row1(ln_b), row1(r_k))


def _qknorm_kernel(x_ref, g_ref, o_ref):
    x = x_ref[...]
    x2 = x.reshape(-1, x.shape[-1])
    ms = _segsum64(x2 * x2, _seg_ones()) * (1.0 / HEAD64)
    o_ref[...] = (x2 * lax.rsqrt(ms + RMS_EPS) * g_ref[...]).reshape(x.shape)


def _qknorm_blocks(proj, first_blk, nblk, gain, tr):
    length = proj.shape[1]
    g = jnp.tile(gain.astype(F32), LANE // HEAD64).reshape(1, LANE)
    return pl.pallas_call(
        _qknorm_kernel,
        grid=(nblk, length // tr),
        in_specs=[pl.BlockSpec((1, tr, LANE), lambda h, i: (first_blk + h, i, 0)),
                  pl.BlockSpec((1, LANE), lambda h, i: (0, 0))],
        out_specs=pl.BlockSpec((1, tr, LANE), lambda h, i: (h, i, 0)),
        out_shape=jax.ShapeDtypeStruct((nblk, length, LANE), F32),
        compiler_params=_cp(("parallel", "parallel")),
        name="qknorm_blocks",
    )(proj, g)


def _qknorm_cols(proj, col_blk, width, gain, tr):
    m = proj.shape[0]
    g = jnp.tile(gain.astype(F32), width // HEAD64).reshape(1, width)
    return pl.pallas_call(
        _qknorm_kernel,
        grid=(m // tr,),
        in_specs=[pl.BlockSpec((tr, width), lambda i: (i, col_blk)),
                  pl.BlockSpec((1, width), lambda i: (0, 0))],
        out_specs=pl.BlockSpec((tr, width), lambda i: (i, 0)),
        out_shape=jax.ShapeDtypeStruct((m, width), F32),
        compiler_params=_cp(("parallel",)),
        name="qknorm_cols",
    )(proj, g)


def _sb_weights(z2, tri, mask, carry):
    return _sb_finish(z2, _sb_prepare(z2, tri, mask), mask, carry)


def _sb_prepare(z2, tri, mask):
    sign = jnp.uint32(0x80000000)
    neg_abs = lax.bitcast_convert_type(lax.bitcast_convert_type(z2, jnp.uint32) | sign, F32)
    n = jnp.maximum(z2, 0.0) + jnp.log2(1.0 + jnp.exp2(neg_abs))
    if mask is not None:
        n = jnp.where(mask, n, 0.0)
    sub = tri.shape[0]
    out = []
    for b in range(z2.shape[1] // sub - 1, -1, -1):
        nb = n[:, b * sub:(b + 1) * sub]
        hi, lo = _split_bf16(nb)
        out.append((_dot(hi, tri) + _dot(lo, tri), jnp.sum(nb, axis=-1, keepdims=True)))
    return out


def _sb_finish(z2, prepared, mask, carry):
    parts = []
    for rest, total in prepared:
        parts.append(carry - rest)
        carry = carry - total
    log2_rest = parts[0] if len(parts) == 1 else jnp.concatenate(parts[::-1], axis=1)
    att = jnp.exp2(z2 + log2_rest)
    if mask is not None:
        att = jnp.where(mask, att, 0.0)
    return att, carry


def _sb_block(qh, kblk, vblk, tri, mask, carry):
    att, carry = _sb_weights(_dot_nt(qh, kblk), tri, mask, carry)
    return _dot(att.astype(BF16), vblk), carry


def _later_tri(n):
    return (lax.broadcasted_iota(jnp.int32, (n, n), 0)
            >= lax.broadcasted_iota(jnp.int32, (n, n), 1)).astype(BF16)


def _stack_halves(q):
    first = lax.broadcasted_iota(jnp.int32, q.shape, 1) < HEAD64
    return jnp.concatenate([jnp.where(first, q, 0.0), jnp.where(first, 0.0, q)], axis=0)


def _key_minus_query(tq, tk):
    row = lax.broadcasted_iota(jnp.int32, (2 * tq, tk), 0)
    row = jnp.where(row >= tq, row - tq, row)
    return lax.broadcasted_iota(jnp.int32, (2 * tq, tk), 1) - row


def _sb_prompt_kernel(q_ref, k_ref, v_ref, o_ref, *, tq, tk, scale):
    qi = pl.program_id(1)
    qs = _stack_halves(q_ref[0] * scale).astype(BF16)
    tri = _later_tri(LANE)
    kmq = _key_minus_query(tq, tk)

    def block(kb, carry, masked):
        c, acc = carry
        rows = pl.ds(pl.multiple_of(kb * tk, tk), tk)
        kblk = k_ref[0, rows, :].astype(BF16)
        vblk = v_ref[0, rows, :].astype(BF16)
        mask = (kmq < qi * tq - kb * tk) if masked else None
        d, c = _sb_block(qs, kblk, vblk, tri, mask, c)
        return c, acc + d

    carry = (jnp.zeros((2 * tq, 1), F32), jnp.zeros((2 * tq, LANE), F32))
    n_full = (qi * tq) // tk
    last = ((qi + 1) * tq - 1) // tk
    carry = lax.fori_loop(0, last + 1 - n_full, lambda it, cr: block(last - it, cr, True), carry)
    odd = n_full % 2
    carry = lax.fori_loop(0, odd, lambda it, cr: block(n_full - 1, cr, False), carry)

    def two_blocks(it, cr):
        c, acc = cr
        kbs = (n_full - odd - 1 - 2 * it, n_full - odd - 2 - 2 * it)
        rows = [pl.ds(pl.multiple_of(kb * tk, tk), tk) for kb in kbs]
        zs = [_dot_nt(qs, k_ref[0, r, :].astype(BF16)) for r in rows]
        prepared = [_sb_prepare(z, tri, None) for z in zs]
        for z, prep, r in zip(zs, prepared, rows):
            att, c = _sb_finish(z, prep, None, c)
            acc = acc + _dot(att.astype(BF16), v_ref[0, r, :].astype(BF16))
        return c, acc

    carry = lax.fori_loop(0, n_full // 2, two_blocks, carry)
    acc = carry[1]
    first = lax.broadcasted_iota(jnp.int32, (tq, LANE), 1) < HEAD64
    o_ref[...] = jnp.where(first, acc[:tq], acc[tq:])


def _sb_prompt(proj, length, tq, tk):
    npair = 8
    return pl.pallas_call(
        functools.partial(_sb_prompt_kernel, tq=tq, tk=tk, scale=HEAD64 ** -0.5 * LOG2E),
        grid=(npair, length // tq),
        in_specs=[pl.BlockSpec((1, tq, LANE), lambda p, i: (p, i, 0)),
                  pl.BlockSpec((1, length, LANE), lambda p, i: (npair + p, 0, 0)),
                  pl.BlockSpec((1, length, LANE), lambda p, i: (2 * npair + p, 0, 0))],
        out_specs=pl.BlockSpec((tq, LANE), lambda p, i: (i, p)),
        out_shape=jax.ShapeDtypeStruct((length, npair * LANE), F32),
        compiler_params=_cp(("parallel", "parallel")),
        name="sb_prompt",
    )(proj, proj, proj)


def _diff_lambda(lqk_ref, lam_init):
    lqk = lqk_ref[...]
    s1 = jnp.sum(lqk[0:1] * lqk[1:2], axis=-1, keepdims=True)
    s2 = jnp.sum(lqk[2:3] * lqk[3:4], axis=-1, keepdims=True)
    return jnp.exp(s1) - jnp.exp(s2) + lam_init


def _softmax_block(qh, kblk, vblk, bias, state):
    m, l, acc = state
    s = _dot_nt(qh, kblk) + bias
    m_new = jnp.maximum(m, jnp.max(s, axis=-1, keepdims=True))
    p = jnp.exp2(s - m_new)
    corr = jnp.exp2(m - m_new)
    return (m_new, corr * l + jnp.sum(p, axis=-1, keepdims=True),
            corr * acc + _dot(p.astype(BF16), vblk))


def _diff_prompt_kernel(q_ref, k_ref, v_ref, slope_ref, lqk_ref, sub_ref, o_ref, *,
                        tq, tk, scale, lam_init):
    qi = pl.program_id(1)
    qs = _stack_halves(q_ref[0] * scale).astype(BF16)
    slope = slope_ref[0][:, :1] * LOG2E
    kmq = _key_minus_query(tq, tk)
    alibi = slope * kmq.astype(F32)

    def block(kb, state, masked):
        rows = pl.ds(pl.multiple_of(kb * tk, tk), tk)
        kblk = k_ref[0, rows, :].astype(BF16)
        vblk = v_ref[0, rows, :].astype(BF16)
        off = kb * tk - qi * tq
        bias = alibi + slope * off.astype(F32)
        if masked:
            bias = jnp.where(kmq <= -off, bias, NEG_INF)
        return _softmax_block(qs, kblk, vblk, bias, state)

    state = (jnp.full((2 * tq, 1), NEG_INF, F32), jnp.zeros((2 * tq, 1), F32),
             jnp.zeros((2 * tq, LANE), F32))
    n_full = (qi * tq) // tk
    last = ((qi + 1) * tq - 1) // tk
    def two_blocks(i, st):
        m_old, l_old, acc = st
        rows = [pl.ds(pl.multiple_of((2 * i + b) * tk, tk), tk) for b in range(2)]
        ss = [_dot_nt(qs, k_ref[0, r, :].astype(BF16)) + alibi
              + slope * ((2 * i + b) * tk - qi * tq).astype(F32) for b, r in enumerate(rows)]
        m_new = m_old
        for s in ss:
            m_new = jnp.maximum(m_new, jnp.max(s, axis=-1, keepdims=True))
        corr = jnp.exp2(m_old - m_new)
        l_new = corr * l_old
        acc = corr * acc
        for s, r in zip(ss, rows):
            p = jnp.exp2(s - m_new)
            l_new = l_new + jnp.sum(p, axis=-1, keepdims=True)
            acc = acc + _dot(p.astype(BF16), v_ref[0, r, :].astype(BF16))
        return m_new, l_new, acc

    state = lax.fori_loop(0, n_full // 2, two_blocks, state)
    state = lax.fori_loop(n_full // 2 * 2, n_full, lambda kb, st: block(kb, st, False), state)
    state = lax.fori_loop(n_full, last + 1, lambda kb, st: block(kb, st, True), state)
    _, l, acc = state
    lam = _diff_lambda(lqk_ref, lam_init)
    o = acc[:tq] / l[:tq] - lam * (acc[tq:] / l[tq:])
    o_ref[...] = _rms_rows(o, sub_ref[...]) * (1.0 - lam_init)


def _diff_slopes(nheads):
    return jnp.exp2(-8.0 * jnp.arange(1, nheads + 1, dtype=F32) / nheads)


def _diff_prompt(qn, kn, proj, v_blk0, slopes, lqk, subln, length, tq, tk, lam_init):
    nheads = qn.shape[0]
    return pl.pallas_call(
        functools.partial(_diff_prompt_kernel, tq=tq, tk=tk, scale=HEAD64 ** -0.5 * LOG2E, lam_init=lam_init),
        grid=(nheads, length // tq),
        in_specs=[pl.BlockSpec((1, tq, LANE), lambda h, i: (h, i, 0)),
                  pl.BlockSpec((1, length, LANE), lambda h, i: (h, 0, 0)),
                  pl.BlockSpec((1, length, LANE), lambda h, i: (v_blk0 + h, 0, 0)),
                  pl.BlockSpec((1, 1, LANE), lambda h, i: (h, 0, 0)),
                  pl.BlockSpec(lqk.shape, lambda h, i: (0, 0)),
                  pl.BlockSpec((1, LANE), lambda h, i: (0, 0))],
        out_specs=pl.BlockSpec((tq, LANE), lambda h, i: (i, h)),
        out_shape=jax.ShapeDtypeStruct((length, nheads * LANE), F32),
        compiler_params=_cp(("parallel", "parallel")),
        name="diff_prompt",
    )(qn, kn, proj, jnp.broadcast_to(slopes.reshape(nheads, 1, 1), (nheads, 1, LANE)), lqk,
      subln.astype(F32).reshape(1, LANE))


def _page_specs(page_table, pages_per_step, block):
    npages = page_table.shape[1]
    zeros = (0,) * (len(block) - 1)

    def spec(i):
        return pl.BlockSpec(block, lambda b, j, pt: (pt[b, npages - 1 - (j * pages_per_step + i)],) + zeros)

    return [spec(i) for i in range(pages_per_step)]


def _sb_sample_kernel(pt_ref, q_ref, knt_ref, vnt_ref, *refs, pages_per_step, scale):
    del pt_ref
    k_refs, v_refs = refs[:pages_per_step], refs[pages_per_step:2 * pages_per_step]
    o_ref, kpad_ref, vpad_ref, acc_ref, car_ref = refs[2 * pages_per_step:]
    j = pl.program_id(1)
    nh, tlen, _ = q_ref.shape[1:]
    tnew = knt_ref.shape[3]
    page = kpad_ref.shape[2]
    rows = nh * tlen
    tri = _later_tri(page)
    q = (q_ref[0] * scale).astype(BF16)

    def visit(kt, vt, mask, state):
        car, acc = state
        z = jnp.einsum('htd,hds->hts', q, kt.astype(BF16), preferred_element_type=F32)
        att, car = _sb_weights(z.reshape(rows, page), tri, mask, car)
        return car, acc + jnp.einsum('hts,hes->hte', att.reshape(nh, tlen, page).astype(BF16),
                                     vt.astype(BF16), preferred_element_type=F32)

    @pl.when(j == 0)
    def _():
        kpad_ref[...] = jnp.zeros(kpad_ref.shape, F32)
        vpad_ref[...] = jnp.zeros(vpad_ref.shape, F32)
        kpad_ref[:, :, :tnew] = knt_ref[0]
        vpad_ref[:, :, :tnew] = vnt_ref[0]
        t = lax.broadcasted_iota(jnp.int32, (rows, page), 0) % tlen
        s = lax.broadcasted_iota(jnp.int32, (rows, page), 1)
        car_ref[...], acc_ref[...] = visit(kpad_ref[...], vpad_ref[...], s < t,
                                           (jnp.zeros(car_ref.shape, F32), jnp.zeros(acc_ref.shape, F32)))

    zs = [jnp.einsum('htd,hds->hts', q, k_ref[0].astype(BF16), preferred_element_type=F32).reshape(rows, page)
          for k_ref in k_refs]
    prepared = [_sb_prepare(z, tri, None) for z in zs]
    car = car_ref[...]
    acc = acc_ref[...]
    for z, prep, v_ref in zip(zs, prepared, v_refs):
        att, car = _sb_finish(z, prep, None, car)
        acc = acc + jnp.einsum('hts,hes->hte', att.reshape(nh, tlen, page).astype(BF16),
                               v_ref[0].astype(BF16), preferred_element_type=F32)
    car_ref[...] = car
    acc_ref[...] = acc

    @pl.when(j == pl.num_programs(1) - 1)
    def _():
        o_ref[0] = acc_ref[...]


def _sb_sample(page_table, q4, knt, vnt, cache_kt, cache_vt, pages_per_step):
    bsz, nh, tlen, hd = q4.shape
    npages = page_table.shape[1]
    page = cache_kt.shape[3]
    seq = lambda a: pl.BlockSpec((1,) + a.shape[1:], lambda b, j, pt: (b, 0, 0, 0))
    pages = _page_specs(page_table, pages_per_step, (1, nh, hd, page))
    grid_spec = pltpu.PrefetchScalarGridSpec(
        num_scalar_prefetch=1,
        grid=(bsz, npages // pages_per_step),
        in_specs=[seq(q4), seq(knt), seq(vnt)] + pages + pages,
        out_specs=seq(q4),
        scratch_shapes=[pltpu.VMEM((nh, hd, page), F32), pltpu.VMEM((nh, hd, page), F32),
                        pltpu.VMEM((nh, tlen, hd), F32), pltpu.VMEM((nh * tlen, 1), F32)])
    return pl.pallas_call(
        functools.partial(_sb_sample_kernel, pages_per_step=pages_per_step, scale=HEAD64 ** -0.5 * LOG2E),
        grid_spec=grid_spec,
        out_shape=jax.ShapeDtypeStruct(q4.shape, F32),
        compiler_params=_cp(("parallel", "arbitrary")),
        name="sb_sample",
    )(page_table, q4, knt, vnt, *([cache_kt] * pages_per_step), *([cache_vt] * pages_per_step))


def _diff_sample_kernel(pt_ref, q_ref, knt_ref, vn_ref, slope_ref, lqk_ref, sub_ref, *refs,
                        pages_per_step, scale, lam_init, past_len):
    del pt_ref
    k_refs, v_refs = refs[:pages_per_step], refs[pages_per_step:2 * pages_per_step]
    o_ref, kpad_ref, vpad_ref, acc_ref, m_ref, l_ref = refs[2 * pages_per_step:]
    j = pl.program_id(1)
    nmap, tlen, _ = q_ref.shape[1:]
    nh = nmap // 2
    tnew = knt_ref.shape[3]
    page = kpad_ref.shape[2]
    npages = pl.num_programs(1) * pages_per_step
    rows = nmap * tlen
    t = lax.broadcasted_iota(jnp.int32, (rows, page), 0) % tlen
    s = lax.broadcasted_iota(jnp.int32, (rows, page), 1)
    slope = slope_ref[...] * LOG2E
    q = (q_ref[0] * scale).astype(BF16)

    wide = page * nh
    spread = (lax.broadcasted_iota(jnp.int32, (page, wide), 1) // nh
              == lax.broadcasted_iota(jnp.int32, (page, wide), 0)).astype(BF16)
    own_head = (lax.broadcasted_iota(jnp.int32, (rows, wide), 1) % nh
                == lax.broadcasted_iota(jnp.int32, (rows, wide), 0) // (2 * tlen))

    def visit(kts, v2s, biases, state):
        m_old, l_old, acc = state
        scs = [jnp.einsum('rtd,rds->rts', q, kt.astype(BF16), preferred_element_type=F32).reshape(rows, page)
               + bias for kt, bias in zip(kts, biases)]
        m_new = m_old
        for sc in scs:
            m_new = jnp.maximum(m_new, jnp.max(sc, axis=-1, keepdims=True))
        corr = jnp.exp2(m_old - m_new)
        l_new = corr * l_old
        acc = corr * acc
        for sc, v2 in zip(scs, v2s):
            p = jnp.exp2(sc - m_new)
            l_new = l_new + jnp.sum(p, axis=-1, keepdims=True)
            pw = jnp.where(own_head, _dot(p.astype(BF16), spread), 0.0).astype(BF16)
            acc = acc + _dot(pw, v2.reshape(wide, v2.shape[-1]).astype(BF16))
        return m_new, l_new, acc

    @pl.when(j == 0)
    def _():
        kpad_ref[...] = jnp.zeros(kpad_ref.shape, F32)
        vpad_ref[...] = jnp.zeros(vpad_ref.shape, F32)
        kpad_ref[:, :, :tnew] = knt_ref[0]
        vpad_ref[:tnew] = vn_ref[0]
        bias = jnp.where(s <= t, -slope * (t - s).astype(F32), NEG_INF)
        init = (jnp.full(m_ref.shape, NEG_INF, F32), jnp.zeros(l_ref.shape, F32),
                jnp.zeros(acc_ref.shape, F32))
        m_ref[...], l_ref[...], acc_ref[...] = visit([kpad_ref[...]], [vpad_ref[...]], [bias], init)

    biases = []
    for i in range(pages_per_step):
        kpos = (npages - 1 - (j * pages_per_step + i)) * page + s
        biases.append(-slope * (past_len + t - kpos).astype(F32))
    m_ref[...], l_ref[...], acc_ref[...] = visit([r[0] for r in k_refs], [r[0] for r in v_refs], biases,
                                                 (m_ref[...], l_ref[...], acc_ref[...]))

    @pl.when(j == pl.num_programs(1) - 1)
    def _():
        lam = _diff_lambda(lqk_ref, lam_init)
        o3 = (acc_ref[...] / l_ref[...]).reshape(nh, 2 * tlen, acc_ref.shape[1])
        for h in range(nh):
            o = o3[h, :tlen] - lam * o3[h, tlen:]
            o_ref[0, h] = _rms_rows(o, sub_ref[...]) * (1.0 - lam_init)


def _diff_sample(page_table, q4, knt, vn4, cache_kt, cache_v, slopes, lqk, subln, lam_init, pages_per_step):
    bsz, nmap, tlen, hd = q4.shape
    npages = page_table.shape[1]
    page = cache_kt.shape[3]
    nheads, dv = cache_v.shape[2], cache_v.shape[3]
    rows = nmap * tlen
    slope_rows = jnp.broadcast_to(jnp.repeat(slopes, rows // nheads)[:, None], (rows, page))
    seq = lambda a: pl.BlockSpec((1,) + a.shape[1:], lambda b, j, pt: (b, 0, 0, 0))
    const2 = lambda shape: pl.BlockSpec(shape, lambda b, j, pt: (0, 0))
    grid_spec = pltpu.PrefetchScalarGridSpec(
        num_scalar_prefetch=1,
        grid=(bsz, npages // pages_per_step),
        in_specs=([seq(q4), seq(knt), seq(vn4), const2((rows, page)), const2(lqk.shape), const2((1, dv))]
                  + _page_specs(page_table, pages_per_step, (1, nmap, hd, page))
                  + _page_specs(page_table, pages_per_step, (1, page, nheads, dv))),
        out_specs=pl.BlockSpec((1, nheads, tlen, dv), lambda b, j, pt: (b, 0, 0, 0)),
        scratch_shapes=[pltpu.VMEM((nmap, hd, page), F32), pltpu.VMEM((page, nheads, dv), F32),
                        pltpu.VMEM((rows, dv), F32),
                        pltpu.VMEM((rows, 1), F32), pltpu.VMEM((rows, 1), F32)])
    return pl.pallas_call(
        functools.partial(_diff_sample_kernel, pages_per_step=pages_per_step, scale=HEAD64 ** -0.5 * LOG2E,
                          lam_init=lam_init, past_len=npages * page),
        grid_spec=grid_spec,
        out_shape=jax.ShapeDtypeStruct((bsz, nheads, tlen, dv), F32),
        compiler_params=_cp(("parallel", "arbitrary")),
        name="diff_sample",
    )(page_table, q4, knt, vn4, slope_rows, lqk, subln.astype(F32).reshape(1, dv),
      *([cache_kt] * pages_per_step), *([cache_v] * pages_per_step))


def _largest_tile(m, cap, mult=16):
    t = min(m, cap)
    while m % t or t % mult:
        t -= mult
    return t


def _layer0(x, bsz, tlen, n_valid, s5_re0, s5_im0, rwkv_s0, shift_prev, wt):
    m = x.shape[0]
    tm = _largest_tile(m, 640)
    u = _proj(x, wt['norm_mix0'], wt['w_in0_u'], tm, 512)
    p = _proj(x, wt['norm_mix0'], wt['w_in0_p'], tm, 1152)

    bu_re, bu_im = _s5_b(u, wt['s5_wb_re'], wt['s5_wb_im'], tm)
    width = bu_re.shape[1]
    nb_s5, tb_s5 = (1, LANE) if bsz == 1 else (_largest_tile(bsz, 16, 1), tlen)
    h_re, h_im, f_re, f_im = _s5_scan(bu_re.reshape(bsz, tlen, width), bu_im.reshape(bsz, tlen, width),
                                      s5_re0, s5_im0, wt['s5_lb_re'], wt['s5_lb_im'], nb_s5, tb_s5)
    if n_valid < tlen:
        f_re, f_im = h_re[:, n_valid - 1], h_im[:, n_valid - 1]
    ya = _s5_c(h_re.reshape(m, width), h_im.reshape(m, width), u, wt['s5_wc_re'], wt['s5_wc_im'],
               wt['s5_d'], wt['s5_w_glu'], _largest_tile(m, 256))

    c = p.shape[1]
    if bsz == 1:
        tr = _largest_tile(tlen, 320)
        prev = jnp.concatenate([shift_prev.astype(F32), p[tr - 1::tr][:-1]])[:, None, :]
        p3, nb = p.reshape(tlen // tr, tr, c), 1
    else:
        prev = shift_prev.astype(F32)[:, None, :]
        p3, nb = p.reshape(bsz, tlen, c), _largest_tile(bsz, 32, 1)
    r, lw, k, v, kk, kka, g = _rwkv_pre(p3, prev, wt['rwkv_mu'], wt['rwkv_w0'], wt['rwkv_w2'],
                                        wt['rwkv_a0'], wt['rwkv_a2'], wt['rwkv_g2'], wt['rwkv_k_k'],
                                        wt['rwkv_k_a'], nb, m if bsz > 1 else n_valid)
    w_r = r.shape[1]
    if bsz == 1:
        v3 = lambda a: a.reshape(bsz, tlen, w_r)
        y, s_last = _rwkv_scan(v3(r), v3(lw), v3(k), v3(v), v3(kk), v3(kka), rwkv_s0,
                               _largest_tile(tlen, 640, LANE), HEAD64)
        y = y.reshape(m, w_r)
    else:
        lanes_b = lambda a: jnp.transpose(a.reshape(bsz, tlen, w_r // HEAD64, HEAD64), (1, 2, 3, 0))
        y, s_last = _rwkv_steps(lanes_b(r), lanes_b(lw), lanes_b(k), lanes_b(v), lanes_b(kk), lanes_b(kka),
                                jnp.transpose(rwkv_s0.astype(F32), (1, 2, 3, 0)))
        y = jnp.transpose(y, (3, 0, 1, 2)).reshape(m, w_r)
        s_last = jnp.transpose(s_last, (3, 0, 1, 2))
    yb = _rwkv_post(y, r, k, v, g, wt['rwkv_ln_w'], wt['rwkv_ln_b'], wt['rwkv_r_k'], _largest_tile(m, 512))

    x = _outproj(x, ya, yb, wt['w_out0_a'], wt['w_out0_b'], tm, 512)
    x = _ffn(x, wt['norm_ffn0'], wt['ffn_w1'], wt['ffn_w3'], wt['ffn_w2'], tm, 512)
    return x, f_re, f_im, s_last, p


def kernel(x_prompt, x_sample, state_s5_re, state_s5_im, state_rwkv, state_rwkv_shift, cache_sb_k, cache_sb_v, cache_diff_k, cache_diff_v, page_table, meta_tokens, norm_mix0, w_in0, s5_lambda_re, s5_lambda_im, s5_log_dt, s5_b_re, s5_b_im, s5_c_re, s5_c_im, s5_d, s5_w_glu, rwkv_mu, rwkv_w0, rwkv_w2, rwkv_a0, rwkv_a2, rwkv_g2, rwkv_k_k, rwkv_k_a, rwkv_r_k, rwkv_ln_w, rwkv_ln_b, w_out0, norm_ffn0, ffn_w1, ffn_w3, ffn_w2, norm_mix1, w_in1, diff_q_norm, diff_k_norm, diff_lambda_q1, diff_lambda_k1, diff_lambda_q2, diff_lambda_k2, diff_subln, w_out1, norm_ffn1, moe_router, moe_w1, moe_w3, moe_w2):
    d_model = x_prompt.shape[-1]
    n_groups, n_state, grp = s5_b_re.shape
    s5_width = n_groups * grp
    rwkv_width = rwkv_w0.shape[0]
    rwkv_heads = rwkv_width // HEAD64
    sb_width = cache_sb_k.shape[2] * cache_sb_k.shape[3]
    diff_heads = cache_diff_v.shape[2]
    diff_qk_width = cache_diff_k.shape[2] * cache_diff_k.shape[3] * cache_diff_k.shape[4]
    diff_v_width = diff_heads * cache_diff_v.shape[3]
    lam_init = 0.8 - 0.6 * math.exp(-0.3)
    bf = lambda a: a.astype(BF16)

    lb_re, lb_im, bb_re, bb_im = _s5_discretize(s5_lambda_re, s5_lambda_im, s5_log_dt, s5_b_re, s5_b_im)
    to_ghp = lambda a: jnp.transpose(a, (1, 0, 2))
    wt = dict(
        norm_mix0=norm_mix0.astype(F32), w_in0_u=bf(w_in0[:, :s5_width]), w_in0_p=bf(w_in0[:, s5_width:]),
        s5_lb_re=lb_re, s5_lb_im=lb_im,
        s5_wb_re=bf(_blockdiag(to_ghp(bb_re), 8)), s5_wb_im=bf(_blockdiag(to_ghp(bb_im), 8)),
        s5_wc_re=bf(_blockdiag(jnp.transpose(s5_c_re.astype(F32), (0, 2, 1)), 16)),
        s5_wc_im=bf(_blockdiag(jnp.transpose(s5_c_im.astype(F32), (0, 2, 1)), 16)),
        s5_d=s5_d.reshape(-1), s5_w_glu=bf(s5_w_glu),
        rwkv_mu=rwkv_mu, rwkv_w0=rwkv_w0, rwkv_w2=bf(rwkv_w2), rwkv_a0=rwkv_a0, rwkv_a2=bf(rwkv_a2),
        rwkv_g2=bf(rwkv_g2), rwkv_k_k=rwkv_k_k, rwkv_k_a=rwkv_k_a, rwkv_r_k=rwkv_r_k,
        rwkv_ln_w=rwkv_ln_w, rwkv_ln_b=rwkv_ln_b,
        w_out0_a=bf(w_out0[:s5_width]), w_out0_b=bf(w_out0[s5_width:]),
        norm_ffn0=norm_ffn0.astype(F32), ffn_w1=bf(ffn_w1), ffn_w3=bf(ffn_w3), ffn_w2=bf(ffn_w2),
    )
    w_in1_b = bf(w_in1)
    w_out1_a, w_out1_b = bf(w_out1[:sb_width]), bf(w_out1[sb_width:])
    moe_w1_b, moe_w3_b, moe_w2_b = bf(moe_w1), bf(moe_w3), bf(moe_w2)
    lqk = jnp.stack([diff_lambda_q1, diff_lambda_k1, diff_lambda_q2, diff_lambda_k2]).astype(F32)
    slopes = _diff_slopes(diff_heads)
    q_col = 3 * sb_width // diff_qk_width
    npair = sb_width // LANE

    bp, seq, _ = x_prompt.shape
    assert bp == 1
    real = N_META + seq
    lp = real + (-real) % LANE
    xp = jnp.concatenate([meta_tokens.astype(F32), x_prompt[0].astype(F32), jnp.zeros((lp - real, d_model), F32)])
    tr_p = _largest_tile(lp, 640)
    tm_p = tr_p
    c_proj = w_in0.shape[1] - s5_width
    z_s5 = jnp.zeros((1, n_groups, n_state), F32)
    z_rwkv = jnp.zeros((1, rwkv_heads, HEAD64, HEAD64), F32)
    xh, p_s5_re, p_s5_im, p_rwkv, p_proj = _layer0(xp, 1, lp, real, z_s5, z_s5, z_rwkv,
                                                    jnp.zeros((1, c_proj), F32), wt)
    p_rwkv_shift = p_proj[real - 1:real]

    proj1 = _proj(xh, norm_mix1.astype(F32), w_in1_b, tm_p, 512, split=True)
    blk_dq = 3 * npair
    blk_dk = blk_dq + diff_heads
    blk_dv = blk_dk + diff_heads
    dqn = _qknorm_blocks(proj1, blk_dq, diff_heads, diff_q_norm, tr_p)
    dkn = _qknorm_blocks(proj1, blk_dk, diff_heads, diff_k_norm, tr_p)
    tq = _largest_tile(lp, 640, LANE)
    y_sb = _sb_prompt(proj1, lp, LANE, tq)
    y_d = _diff_prompt(dqn, dkn, proj1, blk_dv, slopes, lqk, diff_subln, lp, LANE, tq, lam_init)
    xh = _outproj(xh, y_sb, y_d, w_out1_a, w_out1_b, tm_p, 512)
    xh = _moe(xh, norm_ffn1.astype(F32), moe_router, moe_w1_b, moe_w3_b, moe_w2_b, tm_p, 256)
    y_prompt = xh[N_META:real][None]

    def tokens(blocks):
        return jnp.transpose(blocks[:, :real], (1, 0, 2)).reshape(1, real, -1)

    p_sb_k = tokens(proj1[npair:2 * npair]).reshape(1, real, sb_width // HEAD64, HEAD64)
    p_sb_v = tokens(proj1[2 * npair:3 * npair]).reshape(1, real, sb_width // HEAD64, HEAD64)
    p_diff_k = tokens(dkn).reshape(1, real, diff_heads, 2, HEAD64)
    p_diff_v = tokens(proj1[blk_dv:blk_dv + diff_heads]).reshape(1, real, diff_heads, diff_v_width // diff_heads)

    db, dseq, _ = x_sample.shape
    ms = db * dseq
    xs = x_sample.astype(F32).reshape(ms, d_model)
    xs, s_s5_re, s_s5_im, s_rwkv, s_proj = _layer0(xs, db, dseq, dseq, state_s5_re, state_s5_im, state_rwkv,
                                                   state_rwkv_shift, wt)
    s_rwkv_shift = s_proj.reshape(db, dseq, c_proj)[:, -1]

    tm_s = _largest_tile(ms, 640)
    proj_s = _proj(xs, norm_mix1.astype(F32), w_in1_b, tm_s, 512)
    dqn_s = _qknorm_cols(proj_s, q_col, diff_qk_width, diff_q_norm, tm_s)
    dkn_s = _qknorm_cols(proj_s, q_col + 1, diff_qk_width, diff_k_norm, tm_s)
    n_pool, page = cache_sb_k.shape[0], cache_sb_k.shape[1]
    pt = page_table.astype(jnp.int32)
    pages_per_step = 4 if pt.shape[1] % 4 == 0 else 1
    v_col = (3 * sb_width + 2 * diff_qk_width) // diff_v_width
    heads = lambda a: a.reshape(db, dseq, -1, HEAD64)
    q_first = lambda a: jnp.transpose(heads(a), (0, 2, 1, 3))
    t_last = lambda a: jnp.transpose(heads(a), (0, 2, 3, 1))
    slot_last = lambda c: jnp.transpose(c.reshape(n_pool, page, -1, HEAD64), (0, 2, 3, 1))
    ys_sb = _sb_sample(pt, q_first(proj_s[:, :sb_width]), t_last(proj_s[:, sb_width:2 * sb_width]),
                       t_last(proj_s[:, 2 * sb_width:3 * sb_width]), slot_last(cache_sb_k),
                       slot_last(cache_sb_v), pages_per_step)
    ys_d = _diff_sample(pt, q_first(dqn_s), t_last(dkn_s),
                        proj_s[:, v_col * diff_v_width:].reshape(db, dseq, diff_heads, -1),
                        slot_last(cache_diff_k), cache_diff_v, slopes, lqk, diff_subln, lam_init,
                        pages_per_step)
    tokens_s = lambda a: jnp.transpose(a, (0, 2, 1, 3)).reshape(ms, -1)
    xs = _outproj(xs, tokens_s(ys_sb), tokens_s(ys_d), w_out1_a, w_out1_b, tm_s, 512)
    xs = _moe(xs, norm_ffn1.astype(F32), moe_router, moe_w1_b, moe_w3_b, moe_w2_b, tm_s, 256)
    y_sample = xs.reshape(db, dseq, d_model)

    s_sb_k = proj_s[:, sb_width:2 * sb_width].reshape(db, dseq, sb_width // HEAD64, HEAD64)
    s_sb_v = proj_s[:, 2 * sb_width:3 * sb_width].reshape(db, dseq, sb_width // HEAD64, HEAD64)
    s_diff_k = dkn_s.reshape(db, dseq, diff_heads, 2, HEAD64)
    s_diff_v = proj_s[:, v_col * diff_v_width:].reshape(db, dseq, diff_heads, diff_v_width // diff_heads)

    return (y_prompt, y_sample,
            p_s5_re.reshape(1, n_groups, n_state), p_s5_im.reshape(1, n_groups, n_state), p_rwkv, p_rwkv_shift,
            p_sb_k, p_sb_v, p_diff_k, p_diff_v,
            s_s5_re.reshape(db, n_groups, n_state), s_s5_im.reshape(db, n_groups, n_state), s_rwkv, s_rwkv_shift,
            s_sb_k, s_sb_v, s_diff_k, s_diff_v)
```

```python
import functools
import math

import jax
import jax.numpy as jnp
from jax import lax
from jax.experimental import pallas as pl
from jax.experimental.pallas import tpu as pltpu

F32 = jnp.float32
BF16 = jnp.bfloat16
HIGHEST = lax.Precision.HIGHEST

LANE = 128
HEAD64 = 64
RMS_EPS = 1e-6
RWKV_LN_EPS = 64e-5
N_META = 16
NEG_INF = -1e30
LOG2E = math.log2(math.e)
VMEM_LIMIT = 56 * 1024 * 1024


def _cp(sem, vmem=VMEM_LIMIT):
    return pltpu.CompilerParams(dimension_semantics=sem, vmem_limit_bytes=vmem)


def _dot(a, b, precision=None):
    return jnp.dot(a, b, preferred_element_type=F32, precision=precision)


def _dot_nt(a, b, precision=None):
    return lax.dot_general(a, b, (((1,), (1,)), ((), ())), preferred_element_type=F32,
                           precision=precision)


def _dot_tn(a, b, precision=None):
    return lax.dot_general(a, b, (((0,), (0,)), ((), ())), preferred_element_type=F32,
                           precision=precision)


def _split_bf16(x):
    hi = x.astype(BF16)
    lo = (x - hi.astype(F32)).astype(BF16)
    return hi, lo


_NN = ((1,), (0,))
_NT = ((1,), (1,))
_TN = ((0,), (0,))


def _mm3(a, b, dims):
    dg = lambda x, y: lax.dot_general(x, y, (dims, ((), ())), preferred_element_type=F32)
    return dg(a[0], b[0]) + dg(a[0], b[1]) + dg(a[1], b[0])


def _rms_rows(x, gain):
    ms = jnp.mean(x * x, axis=-1, keepdims=True)
    return x * lax.rsqrt(ms + RMS_EPS) * gain


def _softplus(t):
    return jnp.maximum(t, 0.0) + jnp.log1p(jnp.exp(-jnp.abs(t)))


def _segsum64(x, e128):
    cols = [_dot(x[:, c * LANE:(c + 1) * LANE], e128, HIGHEST) for c in range(x.shape[1] // LANE)]
    return cols[0] if len(cols) == 1 else jnp.concatenate(cols, axis=1)


def _seg_ones():
    r = lax.broadcasted_iota(jnp.int32, (LANE, LANE), 0) // HEAD64
    c = lax.broadcasted_iota(jnp.int32, (LANE, LANE), 1) // HEAD64
    return (r == c).astype(F32)


def _proj_kernel(x_ref, g_ref, w_ref, o_ref, xn_ref, *, split):
    @pl.when(pl.program_id(1) == 0)
    def _():
        xn_ref[...] = _rms_rows(x_ref[...], g_ref[...]).astype(BF16)

    acc = _dot(xn_ref[...], w_ref[...])
    if split:
        for c in range(acc.shape[1] // LANE):
            o_ref[c] = acc[:, c * LANE:(c + 1) * LANE]
    else:
        o_ref[...] = acc


def _proj(x, gain, w, tm, tn, split=False):
    m, d = x.shape
    n = w.shape[1]
    if split:
        out_shape = jax.ShapeDtypeStruct((n // LANE, m, LANE), F32)
        out_spec = pl.BlockSpec((tn // LANE, tm, LANE), lambda i, j: (j, i, 0))
    else:
        out_shape = jax.ShapeDtypeStruct((m, n), F32)
        out_spec = pl.BlockSpec((tm, tn), lambda i, j: (i, j))
    return pl.pallas_call(
        functools.partial(_proj_kernel, split=split),
        grid=(m // tm, n // tn),
        in_specs=[pl.BlockSpec((tm, d), lambda i, j: (i, 0)),
                  pl.BlockSpec((1, d), lambda i, j: (0, 0)),
                  pl.BlockSpec((d, tn), lambda i, j: (0, j))],
        out_specs=out_spec,
        out_shape=out_shape,
        scratch_shapes=[pltpu.VMEM((tm, d), BF16)],
        compiler_params=_cp(("parallel", "arbitrary")),
        name="proj",
    )(x, gain.reshape(1, d), w)


def _outproj_kernel(res_ref, a_ref, b_ref, wa_ref, wb_ref, o_ref):
    o_ref[...] = (res_ref[...] + _dot(a_ref[...].astype(BF16), wa_ref[...])
                  + _dot(b_ref[...].astype(BF16), wb_ref[...]))


def _outproj(res, a, b, wa, wb, tm, tn):
    m, n = res.shape
    ka, kb = a.shape[1], b.shape[1]
    return pl.pallas_call(
        _outproj_kernel,
        grid=(m // tm, n // tn),
        in_specs=[pl.BlockSpec((tm, tn), lambda i, j: (i, j)),
                  pl.BlockSpec((tm, ka), lambda i, j: (i, 0)),
                  pl.BlockSpec((tm, kb), lambda i, j: (i, 0)),
                  pl.BlockSpec((ka, tn), lambda i, j: (0, j)),
                  pl.BlockSpec((kb, tn), lambda i, j: (0, j))],
        out_specs=pl.BlockSpec((tm, tn), lambda i, j: (i, j)),
        out_shape=jax.ShapeDtypeStruct((m, n), F32),
        compiler_params=_cp(("parallel", "arbitrary")),
        name="outproj",
    )(res, a, b, wa, wb)


def _ffn_kernel(x_ref, g_ref, w1_ref, w3_ref, w2_ref, o_ref, xn_ref):
    @pl.when(pl.program_id(1) == 0)
    def _():
        x = x_ref[...]
        xn_ref[...] = _rms_rows(x, g_ref[...]).astype(BF16)
        o_ref[...] = x

    xn = xn_ref[...]
    h1 = _dot(xn, w1_ref[...])
    h3 = _dot(xn, w3_ref[...])
    h = (h1 * jax.nn.sigmoid(h1) * h3).astype(BF16)
    o_ref[...] += _dot(h, w2_ref[...])


def _ffn(x, gain, w1, w3, w2, tm, tf):
    m, d = x.shape
    f = w1.shape[1]
    return pl.pallas_call(
        _ffn_kernel,
        grid=(m // tm, f // tf),
        in_specs=[pl.BlockSpec((tm, d), lambda i, j: (i, 0)),
                  pl.BlockSpec((1, d), lambda i, j: (0, 0)),
                  pl.BlockSpec((d, tf), lambda i, j: (0, j)),
                  pl.BlockSpec((d, tf), lambda i, j: (0, j)),
                  pl.BlockSpec((tf, d), lambda i, j: (j, 0))],
        out_specs=pl.BlockSpec((tm, d), lambda i, j: (i, 0)),
        out_shape=jax.ShapeDtypeStruct((m, d), F32),
        scratch_shapes=[pltpu.VMEM((tm, d), BF16)],
        compiler_params=_cp(("parallel", "arbitrary")),
        name="ffn",
    )(x, gain.reshape(1, d), w1, w3, w2)


def _moe_route_kernel(x_ref, g_ref, r_ref, xn_ref, gate_ref, *, n_experts):
    tm = x_ref.shape[0]
    lane = lax.broadcasted_iota(jnp.int32, (tm, LANE), 1)
    xn = _rms_rows(x_ref[...], g_ref[...])
    xn_ref[...] = xn.astype(BF16)
    logits = jnp.where(lane < n_experts, _dot(xn, r_ref[...], HIGHEST), NEG_INF)
    v1 = jnp.max(logits, axis=-1, keepdims=True)
    i1 = jnp.min(jnp.where(logits == v1, lane, LANE), axis=-1, keepdims=True)
    rest = jnp.where(lane == i1, NEG_INF, logits)
    v2 = jnp.max(rest, axis=-1, keepdims=True)
    i2 = jnp.min(jnp.where(rest == v2, lane, LANE), axis=-1, keepdims=True)
    ex = jnp.exp(v2 - v1)
    den = 1.0 + ex
    gate_ref[...] = jnp.where(lane == i1, 1.0 / den, 0.0) + jnp.where(lane == i2, ex / den, 0.0)


def _moe_route(x, gain, router, tm):
    m, d = x.shape
    ne = router.shape[1]
    router_p = jnp.zeros((d, LANE), F32).at[:, :ne].set(router.astype(F32))
    return pl.pallas_call(
        functools.partial(_moe_route_kernel, n_experts=ne),
        grid=(m // tm,),
        in_specs=[pl.BlockSpec((tm, d), lambda i: (i, 0)),
                  pl.BlockSpec((1, d), lambda i: (0, 0)),
                  pl.BlockSpec((d, LANE), lambda i: (0, 0))],
        out_specs=(pl.BlockSpec((tm, d), lambda i: (i, 0)), pl.BlockSpec((tm, LANE), lambda i: (i, 0))),
        out_shape=(jax.ShapeDtypeStruct((m, d), BF16), jax.ShapeDtypeStruct((m, LANE), F32)),
        compiler_params=_cp(("parallel",)),
        name="moe_route",
    )(x, gain.reshape(1, d), router_p)


def _moe_kernel(cnt_ref, x_ref, xn_ref, gate_ref, gt_ref, w1_ref, w3_ref, w2_ref, o_ref,
                sel_ref, xa_ref, acc_ref, *, chunk):
    i = pl.program_id(0)
    e = pl.program_id(1)
    j = pl.program_id(2)
    tm = x_ref.shape[0]
    cnt = cnt_ref[i, e]
    chunks = [(k, slice(k * chunk, (k + 1) * chunk)) for k in range(sel_ref.shape[0] // chunk)]

    @pl.when((e == 0) & (j == 0))
    def _():
        o_ref[...] = x_ref[...]

    @pl.when((j == 0) & (cnt > 0))
    def _():
        routed = (gt_ref[pl.ds(e, 1), :] > 0.0).astype(F32)
        before = (lax.broadcasted_iota(jnp.int32, (tm, tm), 0)
                  < lax.broadcasted_iota(jnp.int32, (tm, tm), 1)).astype(BF16)
        rank = _dot(jnp.broadcast_to(routed, (8, tm)).astype(BF16), before)[0:1]
        slot = lax.broadcasted_iota(jnp.int32, (chunk, tm), 0).astype(F32)
        for k, rows in chunks:
            @pl.when(k * chunk < cnt)
            def _():
                sel = jnp.where(rank == slot + float(k * chunk), routed, 0.0).astype(BF16)
                sel_ref[rows] = sel
                xa_ref[rows] = _dot(sel, xn_ref[...]).astype(BF16)
                acc_ref[rows] = jnp.zeros((chunk, acc_ref.shape[1]), F32)

    for k, rows in chunks:
        @pl.when(k * chunk < cnt)
        def _():
            xa = xa_ref[rows]
            h1 = _dot(xa, w1_ref[0])
            h3 = _dot(xa, w3_ref[0])
            h = (h1 * jax.nn.sigmoid(h1) * h3).astype(BF16)
            acc_ref[rows] += _dot(h, w2_ref[0])

    @pl.when((j == pl.num_programs(2) - 1) & (cnt > 0))
    def _():
        lane = lax.broadcasted_iota(jnp.int32, (tm, LANE), 1)
        ge = jnp.sum(jnp.where(lane == e, gate_ref[...], 0.0), axis=-1, keepdims=True)
        for k, rows in chunks:
            @pl.when(k * chunk < cnt)
            def _():
                hi, lo = _split_bf16(acc_ref[rows])
                sel = sel_ref[rows]
                o_ref[...] += ge * (_dot_tn(sel, hi) + _dot_tn(sel, lo))


def _moe(x, gain, router, w1, w3, w2, tm, tf, chunk=192):
    m, d = x.shape
    ne, _, f = w1.shape
    xn, gate = _moe_route(x, gain, router, tm)
    counts = jnp.sum((gate[:, :ne] > 0.0).reshape(m // tm, tm, ne), axis=1).astype(jnp.int32)
    cap = -(-tm // chunk) * chunk
    grid_spec = pltpu.PrefetchScalarGridSpec(
        num_scalar_prefetch=1,
        grid=(m // tm, ne, f // tf),
        in_specs=[pl.BlockSpec((tm, d), lambda i, e, j, c: (i, 0)),
                  pl.BlockSpec((tm, d), lambda i, e, j, c: (i, 0)),
                  pl.BlockSpec((tm, LANE), lambda i, e, j, c: (i, 0)),
                  pl.BlockSpec((LANE, tm), lambda i, e, j, c: (0, i)),
                  pl.BlockSpec((1, d, tf), lambda i, e, j, c: (e, 0, j)),
                  pl.BlockSpec((1, d, tf), lambda i, e, j, c: (e, 0, j)),
                  pl.BlockSpec((1, tf, d), lambda i, e, j, c: (e, j, 0))],
        out_specs=pl.BlockSpec((tm, d), lambda i, e, j, c: (i, 0)),
        scratch_shapes=[pltpu.VMEM((cap, tm), BF16), pltpu.VMEM((cap, d), BF16), pltpu.VMEM((cap, d), F32)])
    return pl.pallas_call(
        functools.partial(_moe_kernel, chunk=chunk),
        grid_spec=grid_spec,
        out_shape=jax.ShapeDtypeStruct((m, d), F32),
        compiler_params=_cp(("parallel", "arbitrary", "arbitrary")),
        name="moe",
    )(counts, x, xn, gate, gate.T, w1, w3, w2)


def _s5_disc_kernel(lr_ref, li_ref, ldt_ref, bre_ref, bim_ref, lbre_ref, lbim_ref, bbre_ref, bbim_ref):
    lr = lr_ref[...]
    li = li_ref[...]
    dt = jnp.exp(ldt_ref[...])
    mag = jnp.exp(lr * dt)
    lb_re = mag * jnp.cos(li * dt)
    lb_im = mag * jnp.sin(li * dt)
    den = lr * lr + li * li
    f_re = ((lb_re - 1.0) * lr + lb_im * li) / den
    f_im = (lb_im * lr - (lb_re - 1.0) * li) / den
    lbre_ref[...] = lb_re
    lbim_ref[...] = lb_im
    for h in range(bre_ref.shape[0]):
        br = bre_ref[h]
        bi = bim_ref[h]
        bbre_ref[h] = f_re * br - f_im * bi
        bbim_ref[h] = f_re * bi + f_im * br


def _s5_discretize(lam_re, lam_im, log_dt, b_re, b_im):
    g, p, h = b_re.shape
    sds = jax.ShapeDtypeStruct
    return pl.pallas_call(
        _s5_disc_kernel,
        out_shape=(sds((g, p), F32), sds((g, p), F32), sds((h, g, p), F32), sds((h, g, p), F32)),
        name="s5_disc",
    )(lam_re.astype(F32), lam_im.astype(F32), log_dt.astype(F32).reshape(g, 1),
      jnp.transpose(b_re.astype(F32), (2, 0, 1)), jnp.transpose(b_im.astype(F32), (2, 0, 1)))


def _s5_b_kernel(u_ref, wre_ref, wim_ref, ore_ref, oim_ref):
    u = u_ref[...].astype(BF16)
    ore_ref[...] = _dot(u, wre_ref[0])
    oim_ref[...] = _dot(u, wim_ref[0])


def _s5_b(u, wre, wim, tm):
    m = u.shape[0]
    nc, kc, ncol = wre.shape
    spec_o = pl.BlockSpec((tm, ncol), lambda i, c: (i, c))
    spec_w = pl.BlockSpec((1, kc, ncol), lambda i, c: (c, 0, 0))
    sds = jax.ShapeDtypeStruct((m, nc * ncol), F32)
    return pl.pallas_call(
        _s5_b_kernel,
        grid=(m // tm, nc),
        in_specs=[pl.BlockSpec((tm, kc), lambda i, c: (i, c)), spec_w, spec_w],
        out_specs=(spec_o, spec_o),
        out_shape=(sds, sds),
        compiler_params=_cp(("parallel", "arbitrary")),
        name="s5_b",
    )(u, wre, wim)


def _s5_scan_kernel(bre_ref, bim_ref, h0re_ref, h0im_ref, lre_ref, lim_ref,
                    hre_ref, him_ref, fre_ref, fim_ref, cre_ref, cim_ref):
    nb, tlen = bre_ref.shape[0], bre_ref.shape[1]

    @pl.when(pl.program_id(1) == 0)
    def _():
        cre_ref[...] = h0re_ref[...]
        cim_ref[...] = h0im_ref[...]

    lre = lre_ref[...]
    lim = lim_ref[...]

    def per_seq(b, _):
        def step(t, carry):
            hr, hi = carry
            nr = lre * hr - lim * hi + bre_ref[b, t]
            ni = lre * hi + lim * hr + bim_ref[b, t]
            hre_ref[b, t] = nr
            him_ref[b, t] = ni
            return nr, ni

        hr, hi = lax.fori_loop(0, tlen, step, (cre_ref[b], cim_ref[b]), unroll=8)
        cre_ref[b] = hr
        cim_ref[b] = hi
        return 0

    lax.fori_loop(0, nb, per_seq, 0)
    fre_ref[...] = cre_ref[...]
    fim_ref[...] = cim_ref[...]


def _s5_scan(bu_re, bu_im, h0_re, h0_im, lb_re, lb_im, nb, tb):
    bsz, length, width = bu_re.shape
    sub = width // LANE
    v4 = lambda a: a.reshape(bsz, length, sub, LANE)
    v3 = lambda a: a.astype(F32).reshape(bsz, sub, LANE)
    spec_x = pl.BlockSpec((nb, tb, sub, LANE), lambda b, t: (b, t, 0, 0))
    spec_h = pl.BlockSpec((nb, sub, LANE), lambda b, t: (b, 0, 0))
    spec_l = pl.BlockSpec((sub, LANE), lambda b, t: (0, 0))
    sds_x = jax.ShapeDtypeStruct((bsz, length, sub, LANE), F32)
    sds_h = jax.ShapeDtypeStruct((bsz, sub, LANE), F32)
    hre, him, fre, fim = pl.pallas_call(
        _s5_scan_kernel,
        grid=(bsz // nb, length // tb),
        in_specs=[spec_x, spec_x, spec_h, spec_h, spec_l, spec_l],
        out_specs=(spec_x, spec_x, spec_h, spec_h),
        out_shape=(sds_x, sds_x, sds_h, sds_h),
        scratch_shapes=[pltpu.VMEM((nb, sub, LANE), F32), pltpu.VMEM((nb, sub, LANE), F32)],
        compiler_params=_cp(("parallel", "arbitrary")),
        name="s5_scan",
    )(v4(bu_re), v4(bu_im), v3(h0_re), v3(h0_im), lb_re.reshape(sub, LANE), lb_im.reshape(sub, LANE))
    return hre.reshape(bsz, length, width), him.reshape(bsz, length, width), fre, fim


def _s5_c_kernel(hre_ref, him_ref, u_ref, wcre_ref, wcim_ref, d_ref, wglu_ref, o_ref):
    nc, kc, _ = wcre_ref.shape
    cols = []
    for c in range(nc):
        hr = hre_ref[:, c * kc:(c + 1) * kc].astype(BF16)
        hi = him_ref[:, c * kc:(c + 1) * kc].astype(BF16)
        cols.append(_dot(hr, wcre_ref[c]) - _dot(hi, wcim_ref[c]))
    y = jnp.concatenate(cols, axis=1) + d_ref[...] * u_ref[...]
    z = jax.nn.gelu(y)
    o_ref[...] = z * jax.nn.sigmoid(_dot(z.astype(BF16), wglu_ref[...]))


def _s5_c(h_re, h_im, u, wcre, wcim, d_skip, w_glu, tm):
    m, width = h_re.shape
    w = u.shape[1]
    full3 = lambda a: pl.BlockSpec(a.shape, lambda i: (0, 0, 0))
    return pl.pallas_call(
        _s5_c_kernel,
        grid=(m // tm,),
        in_specs=[pl.BlockSpec((tm, width), lambda i: (i, 0)),
                  pl.BlockSpec((tm, width), lambda i: (i, 0)),
                  pl.BlockSpec((tm, w), lambda i: (i, 0)),
                  full3(wcre), full3(wcim),
                  pl.BlockSpec((1, w), lambda i: (0, 0)),
                  pl.BlockSpec((w, w), lambda i: (0, 0))],
        out_specs=pl.BlockSpec((tm, w), lambda i: (i, 0)),
        out_shape=jax.ShapeDtypeStruct((m, w), F32),
        compiler_params=_cp(("parallel",)),
        name="s5_c",
    )(h_re, h_im, u, wcre, wcim, d_skip.astype(F32).reshape(1, w), w_glu)


def _blockdiag(w, nblk):
    g, a, b = w.shape
    w4 = w.reshape(g // nblk, nblk, a, b)
    eye = jnp.eye(nblk, dtype=w.dtype)
    return jnp.einsum('cgab,gh->cgahb', w4, eye).reshape(g // nblk, nblk * a, nblk * b)


def _rwkv_pre_kernel(p_ref, prev_ref, mu_ref, w0_ref, w2_ref, a0_ref, a2_ref, g2_ref, kk_ref, ka_ref,
                     r_o, lw_o, k_o, v_o, kk_o, kka_o, g_o, *, width, lora_w, lora_a, n_valid):
    nb, tlen, c = p_ref.shape
    p = p_ref[...]
    tidx = lax.broadcasted_iota(jnp.int32, (nb, tlen, c), 1)
    prev = jnp.where(tidx == 0, prev_ref[...], pltpu.roll(p, 1, axis=1))
    xm = (p + (prev - p) * mu_ref[...]).reshape(nb * tlen, c)
    r = xm[:, :width]
    k = xm[:, width:2 * width]
    v = xm[:, 2 * width:3 * width]
    o = 3 * width
    w_lo = xm[:, o:o + lora_w]
    a_lo = xm[:, o + lora_w:o + lora_w + lora_a]
    g_lo = xm[:, o + lora_w + lora_a:]
    wraw = -_softplus(-(w0_ref[...] + _dot(jnp.tanh(w_lo).astype(BF16), w2_ref[...]))) - 0.5
    a = jax.nn.sigmoid(a0_ref[...] + _dot(a_lo.astype(BF16), a2_ref[...]))
    g = _dot(jax.nn.sigmoid(g_lo).astype(BF16), g2_ref[...])
    kkr = k * kk_ref[...]
    nrm = jnp.sqrt(_segsum64(kkr * kkr, _seg_ones()))
    kk = kkr / jnp.maximum(nrm, 1e-12)
    row = pl.program_id(0) * (nb * tlen) + lax.broadcasted_iota(jnp.int32, (nb * tlen, 1), 0)
    keep = (row < n_valid).astype(F32)
    r_o[...] = r
    lw_o[...] = -jnp.exp(wraw) * keep
    k_o[...] = k * (1.0 + (a - 1.0) * ka_ref[...]) * keep
    v_o[...] = v * keep
    kk_o[...] = kk * keep
    kka_o[...] = kk * a * keep
    g_o[...] = g


def _rwkv_pre(p3, prev, mu, w0, w2, a0, a2, g2, k_k, k_a, nb, n_valid):
    nbt, tlen, c = p3.shape
    width = w0.shape[0]
    lora_w, lora_a = w2.shape[0], a2.shape[0]
    rows = nb * tlen
    row1 = lambda a: a.astype(F32).reshape(1, -1)
    full2 = lambda a: pl.BlockSpec(a.shape, lambda i: (0, 0))
    spec_o = pl.BlockSpec((rows, width), lambda i: (i, 0))
    sds = jax.ShapeDtypeStruct((nbt * tlen, width), F32)
    args = (mu.astype(F32).reshape(1, 1, c), row1(w0), w2, row1(a0), a2, g2, row1(k_k), row1(k_a))
    return pl.pallas_call(
        functools.partial(_rwkv_pre_kernel, width=width, lora_w=lora_w, lora_a=lora_a, n_valid=n_valid),
        grid=(nbt // nb,),
        in_specs=[pl.BlockSpec((nb, tlen, c), lambda i: (i, 0, 0)),
                  pl.BlockSpec((nb, 1, c), lambda i: (i, 0, 0)),
                  pl.BlockSpec((1, 1, c), lambda i: (0, 0, 0))] + [full2(a) for a in args[1:]],
        out_specs=(spec_o,) * 7,
        out_shape=(sds,) * 7,
        compiler_params=_cp(("parallel",)),
        name="rwkv_pre",
    )(p3, prev, *args)


def _rwkv_scan_kernel(r_ref, lw_ref, k_ref, v_ref, kk_ref, kka_ref, s0_ref, y_ref, sl_ref, s_ref, *, tc):
    tb = r_ref.shape[1]
    npair = r_ref.shape[2] // LANE
    n2 = 2 * tc
    lane = lax.broadcasted_iota(jnp.int32, (tc, LANE), 1)
    first = lane < HEAD64
    ri = lax.broadcasted_iota(jnp.int32, (n2, n2), 0)
    ci = lax.broadcasted_iota(jnp.int32, (n2, n2), 1)
    same = (ri // tc) == (ci // tc)
    strict = same & ((ci % tc) < (ri % tc))
    incl = same & ((ci % tc) <= (ri % tc))
    eye = (ri == ci).astype(F32)
    trow = lax.broadcasted_iota(jnp.int32, (tc, LANE), 0)
    shifts = [1 << b for b in range(max(tc - 1, 0).bit_length())]
    n_sq = max(int(math.ceil(math.log2(tc))) - 1, 0)

    def stack(x):
        return jnp.concatenate([jnp.where(first, x, 0.0), jnp.where(first, 0.0, x)], axis=0)

    @pl.when(pl.program_id(1) == 0)
    def _():
        for pr in range(npair):
            sa = jnp.concatenate([s0_ref[0, 2 * pr], jnp.zeros((HEAD64, HEAD64), F32)], axis=1)
            sb = jnp.concatenate([jnp.zeros((HEAD64, HEAD64), F32), s0_ref[0, 2 * pr + 1]], axis=1)
            s_ref[pr] = jnp.concatenate([sa, sb], axis=0)

    def prepare(rows, pr):
        lanes = slice(pr * LANE, (pr + 1) * LANE)
        lw = lw_ref[0, rows, lanes]
        c = lw
        for sh in shifts:
            c = c + jnp.where(trow >= sh, pltpu.roll(c, sh, axis=0), 0.0)
        c_last = c[tc - 1:tc, :]
        kk = kk_ref[0, rows, lanes]
        kka = kka_ref[0, rows, lanes]
        kx = k_ref[0, rows, lanes]
        e_neg = jnp.exp(-c)
        e_end = jnp.exp(c_last - c)
        ops = (-kk * jnp.exp(c - lw), r_ref[0, rows, lanes] * jnp.exp(c), kka * e_neg, kx * e_neg,
               kka * e_end, kx * e_end, v_ref[0, rows, lanes])
        return tuple(_split_bf16(stack(x)) for x in ops) + (jnp.exp(c_last),)

    def chunk(ci_, _):
        rows = pl.ds(pl.multiple_of(ci_ * tc, tc), tc)
        prs = range(npair)
        al, rt, bt, kt, bh, kh, vs, w_end = zip(*[prepare(rows, pr) for pr in prs])
        nab = [jnp.where(strict, _mm3(al[p], bt[p], _NT), 0.0) for p in prs]
        aak = [jnp.where(strict, _mm3(al[p], kt[p], _NT), 0.0) for p in prs]
        arb = [jnp.where(incl, _mm3(rt[p], bt[p], _NT), 0.0) for p in prs]
        ark = [jnp.where(incl, _mm3(rt[p], kt[p], _NT), 0.0) for p in prs]
        inv = [eye + nab[p] for p in prs]
        pw = nab
        for _ in range(n_sq):
            pw_s = [_split_bf16(x) for x in pw]
            pw = [_mm3(pw_s[p], pw_s[p], _NN) for p in prs]
            inv = [inv[p] + _mm3(_split_bf16(inv[p]), _split_bf16(pw[p]), _NN) for p in prs]
        s0 = [s_ref[p] for p in prs]
        s0_s = [_split_bf16(x) for x in s0]
        rhs = [_mm3(al[p], s0_s[p], _NT) + _mm3(_split_bf16(aak[p]), vs[p], _NN) for p in prs]
        u = [_split_bf16(_mm3(_split_bf16(inv[p]), _split_bf16(rhs[p]), _NN)) for p in prs]
        for p in prs:
            ys = (_mm3(rt[p], s0_s[p], _NT) + _mm3(_split_bf16(arb[p]), u[p], _NN)
                  + _mm3(_split_bf16(ark[p]), vs[p], _NN))
            y_ref[0, rows, p * LANE:(p + 1) * LANE] = ys[:tc] + ys[tc:]
            s_ref[p] = s0[p] * w_end[p] + _mm3(u[p], bh[p], _TN) + _mm3(vs[p], kh[p], _TN)
        return 0

    lax.fori_loop(0, tb // tc, chunk, 0)

    for pr in range(npair):
        s = s_ref[pr]
        sl_ref[0, 2 * pr] = s[:HEAD64, :HEAD64]
        sl_ref[0, 2 * pr + 1] = s[HEAD64:, HEAD64:]


def _rwkv_scan(r, lw, k, v, kk, kka, s0, tb, tc):
    bsz, length, width = r.shape
    spec_x = pl.BlockSpec((1, tb, width), lambda b, t: (b, t, 0))
    spec_s = pl.BlockSpec((1,) + s0.shape[1:], lambda b, t: (b, 0, 0, 0))
    return pl.pallas_call(
        functools.partial(_rwkv_scan_kernel, tc=tc),
        grid=(bsz, length // tb),
        in_specs=[spec_x] * 6 + [spec_s],
        out_specs=(spec_x, spec_s),
        out_shape=(jax.ShapeDtypeStruct((bsz, length, width), F32),
                   jax.ShapeDtypeStruct(s0.shape, F32)),
        scratch_shapes=[pltpu.VMEM((width // LANE, LANE, LANE), F32)],
        compiler_params=_cp(("parallel", "arbitrary")),
        name="rwkv_scan",
    )(r, lw, k, v, kk, kka, s0.astype(F32))


def _rwkv_steps_kernel(r_ref, lw_ref, k_ref, v_ref, kk_ref, kka_ref, s0_ref, y_ref, sl_ref):
    tlen = r_ref.shape[0]
    n = r_ref.shape[2]
    sl_ref[...] = s0_ref[...]
    for t in range(tlen):
        w = jnp.exp(lw_ref[t, 0])
        kk = kk_ref[t, 0]
        kka = kka_ref[t, 0]
        kx = k_ref[t, 0]
        rx = r_ref[t, 0]

        def row(i, _):
            s = sl_ref[0, i]
            sa = -jnp.sum(s * kk, axis=0, keepdims=True)
            s = s * w + sa * kka + v_ref[t, 0, pl.ds(i, 1), :] * kx
            sl_ref[0, i] = s
            y_ref[t, 0, pl.ds(i, 1), :] = jnp.sum(s * rx, axis=0, keepdims=True)
            return 0

        lax.fori_loop(0, n, row, 0)


def _rwkv_steps(r, lw, k, v, kk, kka, s0):
    tlen, nh, n, bsz = r.shape
    spec_x = pl.BlockSpec((tlen, 1, n, bsz), lambda h: (0, h, 0, 0))
    spec_s = pl.BlockSpec((1, n, n, bsz), lambda h: (h, 0, 0, 0))
    return pl.pallas_call(
        _rwkv_steps_kernel,
        grid=(nh,),
        in_specs=[spec_x] * 6 + [spec_s],
        out_specs=(spec_x, spec_s),
        out_shape=(jax.ShapeDtypeStruct(r.shape, F32), jax.ShapeDtypeStruct(s0.shape, F32)),
        compiler_params=_cp(("parallel",)),
        name="rwkv_steps",
    )(r, lw, k, v, kk, kka, s0)


def _rwkv_post_kernel(y_ref, r_ref, k_ref, v_ref, g_ref, lnw_ref, lnb_ref, rk_ref, o_ref):
    e128 = _seg_ones()
    y = y_ref[...]
    mean = _segsum64(y, e128) * (1.0 / HEAD64)
    d = y - mean
    var = _segsum64(d * d, e128) * (1.0 / HEAD64)
    yn = d * lax.rsqrt(var + RWKV_LN_EPS) * lnw_ref[...] + lnb_ref[...]
    bonus = _segsum64(r_ref[...] * k_ref[...] * rk_ref[...], e128) * v_ref[...]
    o_ref[...] = (yn + bonus) * g_ref[...]


def _rwkv_post(y, r, k, v, g, ln_w, ln_b, r_k, tm):
    m, w = y.shape
    spec_x = pl.BlockSpec((tm, w), lambda i: (i, 0))
    spec_p = pl.BlockSpec((1, w), lambda i: (0, 0))
    row1 = lambda a: a.astype(F32).reshape(1, w)
    return pl.pallas_call(
        _rwkv_post_kernel,
        grid=(m // tm,),
        in_specs=[spec_x] * 5 + [spec_p] * 3,
        out_specs=spec_x,
        out_shape=jax.ShapeDtypeStruct((m, w), F32),
        compiler_params=_cp(("parallel",)),
        name="rwkv_post",
    )(y, r, k, v, g, row1(ln_w), row1(ln_b), row1(r_k))


def _qknorm_kernel(x_ref, g_ref, o_ref):
    x = x_ref[...]
    x2 = x.reshape(-1, x.shape[-1])
    ms = _segsum64(x2 * x2, _seg_ones()) * (1.0 / HEAD64)
    o_ref[...] = (x2 * lax.rsqrt(ms + RMS_EPS) * g_ref[...]).reshape(x.shape)


def _qknorm_blocks(proj, first_blk, nblk, gain, tr):
    length = proj.shape[1]
    g = jnp.tile(gain.astype(F32), LANE // HEAD64).reshape(1, LANE)
    return pl.pallas_call(
        _qknorm_kernel,
        grid=(nblk, length // tr),
        in_specs=[pl.BlockSpec((1, tr, LANE), lambda h, i: (first_blk + h, i, 0)),
                  pl.BlockSpec((1, LANE), lambda h, i: (0, 0))],
        out_specs=pl.BlockSpec((1, tr, LANE), lambda h, i: (h, i, 0)),
        out_shape=jax.ShapeDtypeStruct((nblk, length, LANE), F32),
        compiler_params=_cp(("parallel", "parallel")),
        name="qknorm_blocks",
    )(proj, g)


def _qknorm_cols(proj, col_blk, width, gain, tr):
    m = proj.shape[0]
    g = jnp.tile(gain.astype(F32), width // HEAD64).reshape(1, width)
    return pl.pallas_call(
        _qknorm_kernel,
        grid=(m // tr,),
        in_specs=[pl.BlockSpec((tr, width), lambda i: (i, col_blk)),
                  pl.BlockSpec((1, width), lambda i: (0, 0))],
        out_specs=pl.BlockSpec((tr, width), lambda i: (i, 0)),
        out_shape=jax.ShapeDtypeStruct((m, width), F32),
        compiler_params=_cp(("parallel",)),
        name="qknorm_cols",
    )(proj, g)


def _sb_weights(z2, tri, mask, carry):
    return _sb_finish(z2, _sb_prepare(z2, tri, mask), mask, carry)


def _sb_prepare(z2, tri, mask):
    sign = jnp.uint32(0x80000000)
    neg_abs = lax.bitcast_convert_type(lax.bitcast_convert_type(z2, jnp.uint32) | sign, F32)
    n = jnp.maximum(z2, 0.0) + jnp.log2(1.0 + jnp.exp2(neg_abs))
    if mask is not None:
        n = jnp.where(mask, n, 0.0)
    sub = tri.shape[0]
    out = []
    for b in range(z2.shape[1] // sub - 1, -1, -1):
        nb = n[:, b * sub:(b + 1) * sub]
        hi, lo = _split_bf16(nb)
        out.append((_dot(hi, tri) + _dot(lo, tri), jnp.sum(nb, axis=-1, keepdims=True)))
    return out


def _sb_finish(z2, prepared, mask, carry):
    parts = []
    for rest, total in prepared:
        parts.append(carry - rest)
        carry = carry - total
    log2_rest = parts[0] if len(parts) == 1 else jnp.concatenate(parts[::-1], axis=1)
    att = jnp.exp2(z2 + log2_rest)
    if mask is not None:
        att = jnp.where(mask, att, 0.0)
    return att, carry


def _sb_block(qh, kblk, vblk, tri, mask, carry):
    att, carry = _sb_weights(_dot_nt(qh, kblk), tri, mask, carry)
    return _dot(att.astype(BF16), vblk), carry


def _later_tri(n):
    return (lax.broadcasted_iota(jnp.int32, (n, n), 0)
            >= lax.broadcasted_iota(jnp.int32, (n, n), 1)).astype(BF16)


def _stack_halves(q):
    first = lax.broadcasted_iota(jnp.int32, q.shape, 1) < HEAD64
    return jnp.concatenate([jnp.where(first, q, 0.0), jnp.where(first, 0.0, q)], axis=0)


def _key_minus_query(tq, tk):
    row = lax.broadcasted_iota(jnp.int32, (2 * tq, tk), 0)
    row = jnp.where(row >= tq, row - tq, row)
    return lax.broadcasted_iota(jnp.int32, (2 * tq, tk), 1) - row


def _sb_prompt_kernel(q_ref, k_ref, v_ref, o_ref, *, tq, tk, scale):
    qi = pl.program_id(1)
    qs = _stack_halves(q_ref[0] * scale).astype(BF16)
    tri = _later_tri(LANE)
    kmq = _key_minus_query(tq, tk)

    def block(kb, carry, masked):
        c, acc = carry
        rows = pl.ds(pl.multiple_of(kb * tk, tk), tk)
        kblk = k_ref[0, rows, :].astype(BF16)
        vblk = v_ref[0, rows, :].astype(BF16)
        mask = (kmq < qi * tq - kb * tk) if masked else None
        d, c = _sb_block(qs, kblk, vblk, tri, mask, c)
        return c, acc + d

    carry = (jnp.zeros((2 * tq, 1), F32), jnp.zeros((2 * tq, LANE), F32))
    n_full = (qi * tq) // tk
    last = ((qi + 1) * tq - 1) // tk
    carry = lax.fori_loop(0, last + 1 - n_full, lambda it, cr: block(last - it, cr, True), carry)
    odd = n_full % 2
    carry = lax.fori_loop(0, odd, lambda it, cr: block(n_full - 1, cr, False), carry)

    def two_blocks(it, cr):
        c, acc = cr
        kbs = (n_full - odd - 1 - 2 * it, n_full - odd - 2 - 2 * it)
        rows = [pl.ds(pl.multiple_of(kb * tk, tk), tk) for kb in kbs]
        zs = [_dot_nt(qs, k_ref[0, r, :].astype(BF16)) for r in rows]
        prepared = [_sb_prepare(z, tri, None) for z in zs]
        for z, prep, r in zip(zs, prepared, rows):
            att, c = _sb_finish(z, prep, None, c)
            acc = acc + _dot(att.astype(BF16), v_ref[0, r, :].astype(BF16))
        return c, acc

    carry = lax.fori_loop(0, n_full // 2, two_blocks, carry)
    acc = carry[1]
    first = lax.broadcasted_iota(jnp.int32, (tq, LANE), 1) < HEAD64
    o_ref[...] = jnp.where(first, acc[:tq], acc[tq:])


def _sb_prompt(proj, length, tq, tk):
    npair = 8
    return pl.pallas_call(
        functools.partial(_sb_prompt_kernel, tq=tq, tk=tk, scale=HEAD64 ** -0.5 * LOG2E),
        grid=(npair, length // tq),
        in_specs=[pl.BlockSpec((1, tq, LANE), lambda p, i: (p, i, 0)),
                  pl.BlockSpec((1, length, LANE), lambda p, i: (npair + p, 0, 0)),
                  pl.BlockSpec((1, length, LANE), lambda p, i: (2 * npair + p, 0, 0))],
        out_specs=pl.BlockSpec((tq, LANE), lambda p, i: (i, p)),
        out_shape=jax.ShapeDtypeStruct((length, npair * LANE), F32),
        compiler_params=_cp(("parallel", "parallel")),
        name="sb_prompt",
    )(proj, proj, proj)


def _diff_lambda(lqk_ref, lam_init):
    lqk = lqk_ref[...]
    s1 = jnp.sum(lqk[0:1] * lqk[1:2], axis=-1, keepdims=True)
    s2 = jnp.sum(lqk[2:3] * lqk[3:4], axis=-1, keepdims=True)
    return jnp.exp(s1) - jnp.exp(s2) + lam_init


def _softmax_block(qh, kblk, vblk, bias, state):
    m, l, acc = state
    s = _dot_nt(qh, kblk) + bias
    m_new = jnp.maximum(m, jnp.max(s, axis=-1, keepdims=True))
    p = jnp.exp2(s - m_new)
    corr = jnp.exp2(m - m_new)
    return (m_new, corr * l + jnp.sum(p, axis=-1, keepdims=True),
            corr * acc + _dot(p.astype(BF16), vblk))


def _diff_prompt_kernel(q_ref, k_ref, v_ref, slope_ref, lqk_ref, sub_ref, o_ref, *,
                        tq, tk, scale, lam_init):
    qi = pl.program_id(1)
    qs = _stack_halves(q_ref[0] * scale).astype(BF16)
    slope = slope_ref[0][:, :1] * LOG2E
    kmq = _key_minus_query(tq, tk)
    alibi = slope * kmq.astype(F32)

    def block(kb, state, masked):
        rows = pl.ds(pl.multiple_of(kb * tk, tk), tk)
        kblk = k_ref[0, rows, :].astype(BF16)
        vblk = v_ref[0, rows, :].astype(BF16)
        off = kb * tk - qi * tq
        bias = alibi + slope * off.astype(F32)
        if masked:
            bias = jnp.where(kmq <= -off, bias, NEG_INF)
        return _softmax_block(qs, kblk, vblk, bias, state)

    state = (jnp.full((2 * tq, 1), NEG_INF, F32), jnp.zeros((2 * tq, 1), F32),
             jnp.zeros((2 * tq, LANE), F32))
    n_full = (qi * tq) // tk
    last = ((qi + 1) * tq - 1) // tk
    def two_blocks(i, st):
        m_old, l_old, acc = st
        rows = [pl.ds(pl.multiple_of((2 * i + b) * tk, tk), tk) for b in range(2)]
        ss = [_dot_nt(qs, k_ref[0, r, :].astype(BF16)) + alibi
              + slope * ((2 * i + b) * tk - qi * tq).astype(F32) for b, r in enumerate(rows)]
        m_new = m_old
        for s in ss:
            m_new = jnp.maximum(m_new, jnp.max(s, axis=-1, keepdims=True))
        corr = jnp.exp2(m_old - m_new)
        l_new = corr * l_old
        acc = corr * acc
        for s, r in zip(ss, rows):
            p = jnp.exp2(s - m_new)
            l_new = l_new + jnp.sum(p, axis=-1, keepdims=True)
            acc = acc + _dot(p.astype(BF16), v_ref[0, r, :].astype(BF16))
        return m_new, l_new, acc

    state = lax.fori_loop(0, n_full // 2, two_blocks, state)
    state = lax.fori_loop(n_full // 2 * 2, n_full, lambda kb, st: block(kb, st, False), state)
    state = lax.fori_loop(n_full, last + 1, lambda kb, st: block(kb, st, True), state)
    _, l, acc = state
    lam = _diff_lambda(lqk_ref, lam_init)
    o = acc[:tq] / l[:tq] - lam * (acc[tq:] / l[tq:])
    o_ref[...] = _rms_rows(o, sub_ref[...]) * (1.0 - lam_init)


def _diff_slopes(nheads):
    return jnp.exp2(-8.0 * jnp.arange(1, nheads + 1, dtype=F32) / nheads)


def _diff_prompt(qn, kn, proj, v_blk0, slopes, lqk, subln, length, tq, tk, lam_init):
    nheads = qn.shape[0]
    return pl.pallas_call(
        functools.partial(_diff_prompt_kernel, tq=tq, tk=tk, scale=HEAD64 ** -0.5 * LOG2E, lam_init=lam_init),
        grid=(nheads, length // tq),
        in_specs=[pl.BlockSpec((1, tq, LANE), lambda h, i: (h, i, 0)),
                  pl.BlockSpec((1, length, LANE), lambda h, i: (h, 0, 0)),
                  pl.BlockSpec((1, length, LANE), lambda h, i: (v_blk0 + h, 0, 0)),
                  pl.BlockSpec((1, 1, LANE), lambda h, i: (h, 0, 0)),
                  pl.BlockSpec(lqk.shape, lambda h, i: (0, 0)),
                  pl.BlockSpec((1, LANE), lambda h, i: (0, 0))],
        out_specs=pl.BlockSpec((tq, LANE), lambda h, i: (i, h)),
        out_shape=jax.ShapeDtypeStruct((length, nheads * LANE), F32),
        compiler_params=_cp(("parallel", "parallel")),
        name="diff_prompt",
    )(qn, kn, proj, jnp.broadcast_to(slopes.reshape(nheads, 1, 1), (nheads, 1, LANE)), lqk,
      subln.astype(F32).reshape(1, LANE))


def _page_specs(page_table, pages_per_step, block):
    npages = page_table.shape[1]
    zeros = (0,) * (len(block) - 1)

    def spec(i):
        return pl.BlockSpec(block, lambda b, j, pt: (pt[b, npages - 1 - (j * pages_per_step + i)],) + zeros)

    return [spec(i) for i in range(pages_per_step)]


def _sb_sample_kernel(pt_ref, q_ref, knt_ref, vnt_ref, *refs, pages_per_step, scale):
    del pt_ref
    k_refs, v_refs = refs[:pages_per_step], refs[pages_per_step:2 * pages_per_step]
    o_ref, kpad_ref, vpad_ref, acc_ref, car_ref = refs[2 * pages_per_step:]
    j = pl.program_id(1)
    nh, tlen, _ = q_ref.shape[1:]
    tnew = knt_ref.shape[3]
    page = kpad_ref.shape[2]
    rows = nh * tlen
    tri = _later_tri(page)
    q = (q_ref[0] * scale).astype(BF16)

    def visit(kt, vt, mask, state):
        car, acc = state
        z = jnp.einsum('htd,hds->hts', q, kt.astype(BF16), preferred_element_type=F32)
        att, car = _sb_weights(z.reshape(rows, page), tri, mask, car)
        return car, acc + jnp.einsum('hts,hes->hte', att.reshape(nh, tlen, page).astype(BF16),
                                     vt.astype(BF16), preferred_element_type=F32)

    @pl.when(j == 0)
    def _():
        kpad_ref[...] = jnp.zeros(kpad_ref.shape, F32)
        vpad_ref[...] = jnp.zeros(vpad_ref.shape, F32)
        kpad_ref[:, :, :tnew] = knt_ref[0]
        vpad_ref[:, :, :tnew] = vnt_ref[0]
        t = lax.broadcasted_iota(jnp.int32, (rows, page), 0) % tlen
        s = lax.broadcasted_iota(jnp.int32, (rows, page), 1)
        car_ref[...], acc_ref[...] = visit(kpad_ref[...], vpad_ref[...], s < t,
                                           (jnp.zeros(car_ref.shape, F32), jnp.zeros(acc_ref.shape, F32)))

    zs = [jnp.einsum('htd,hds->hts', q, k_ref[0].astype(BF16), preferred_element_type=F32).reshape(rows, page)
          for k_ref in k_refs]
    prepared = [_sb_prepare(z, tri, None) for z in zs]
    car = car_ref[...]
    acc = acc_ref[...]
    for z, prep, v_ref in zip(zs, prepared, v_refs):
        att, car = _sb_finish(z, prep, None, car)
        acc = acc + jnp.einsum('hts,hes->hte', att.reshape(nh, tlen, page).astype(BF16),
                               v_ref[0].astype(BF16), preferred_element_type=F32)
    car_ref[...] = car
    acc_ref[...] = acc

    @pl.when(j == pl.num_programs(1) - 1)
    def _():
        o_ref[0] = acc_ref[...]


def _sb_sample(page_table, q4, knt, vnt, cache_kt, cache_vt, pages_per_step):
    bsz, nh, tlen, hd = q4.shape
    npages = page_table.shape[1]
    page = cache_kt.shape[3]
    seq = lambda a: pl.BlockSpec((1,) + a.shape[1:], lambda b, j, pt: (b, 0, 0, 0))
    pages = _page_specs(page_table, pages_per_step, (1, nh, hd, page))
    grid_spec = pltpu.PrefetchScalarGridSpec(
        num_scalar_prefetch=1,
        grid=(bsz, npages // pages_per_step),
        in_specs=[seq(q4), seq(knt), seq(vnt)] + pages + pages,
        out_specs=seq(q4),
        scratch_shapes=[pltpu.VMEM((nh, hd, page), F32), pltpu.VMEM((nh, hd, page), F32),
                        pltpu.VMEM((nh, tlen, hd), F32), pltpu.VMEM((nh * tlen, 1), F32)])
    return pl.pallas_call(
        functools.partial(_sb_sample_kernel, pages_per_step=pages_per_step, scale=HEAD64 ** -0.5 * LOG2E),
        grid_spec=grid_spec,
        out_shape=jax.ShapeDtypeStruct(q4.shape, F32),
        compiler_params=_cp(("parallel", "arbitrary")),
        name="sb_sample",
    )(page_table, q4, knt, vnt, *([cache_kt] * pages_per_step), *([cache_vt] * pages_per_step))


def _diff_sample_kernel(pt_ref, q_ref, knt_ref, vn_ref, slope_ref, lqk_ref, sub_ref, *refs,
                        pages_per_step, scale, lam_init, past_len):
    del pt_ref
    k_refs, v_refs = refs[:pages_per_step], refs[pages_per_step:2 * pages_per_step]
    o_ref, kpad_ref, vpad_ref, acc_ref, m_ref, l_ref = refs[2 * pages_per_step:]
    j = pl.program_id(1)
    nmap, tlen, _ = q_ref.shape[1:]
    nh = nmap // 2
    tnew = knt_ref.shape[3]
    page = kpad_ref.shape[2]
    npages = pl.num_programs(1) * pages_per_step
    rows = nmap * tlen
    t = lax.broadcasted_iota(jnp.int32, (rows, page), 0) % tlen
    s = lax.broadcasted_iota(jnp.int32, (rows, page), 1)
    slope = slope_ref[...] * LOG2E
    q = (q_ref[0] * scale).astype(BF16)

    wide = page * nh
    spread = (lax.broadcasted_iota(jnp.int32, (page, wide), 1) // nh
              == lax.broadcasted_iota(jnp.int32, (page, wide), 0)).astype(BF16)
    own_head = (lax.broadcasted_iota(jnp.int32, (rows, wide), 1) % nh
                == lax.broadcasted_iota(jnp.int32, (rows, wide), 0) // (2 * tlen))

    def visit(kts, v2s, biases, state):
        m_old, l_old, acc = state
        scs = [jnp.einsum('rtd,rds->rts', q, kt.astype(BF16), preferred_element_type=F32).reshape(rows, page)
               + bias for kt, bias in zip(kts, biases)]
        m_new = m_old
        for sc in scs:
            m_new = jnp.maximum(m_new, jnp.max(sc, axis=-1, keepdims=True))
        corr = jnp.exp2(m_old - m_new)
        l_new = corr * l_old
        acc = corr * acc
        for sc, v2 in zip(scs, v2s):
            p = jnp.exp2(sc - m_new)
            l_new = l_new + jnp.sum(p, axis=-1, keepdims=True)
            pw = jnp.where(own_head, _dot(p.astype(BF16), spread), 0.0).astype(BF16)
            acc = acc + _dot(pw, v2.reshape(wide, v2.shape[-1]).astype(BF16))
        return m_new, l_new, acc

    @pl.when(j == 0)
    def _():
        kpad_ref[...] = jnp.zeros(kpad_ref.shape, F32)
        vpad_ref[...] = jnp.zeros(vpad_ref.shape, F32)
        kpad_ref[:, :, :tnew] = knt_ref[0]
        vpad_ref[:tnew] = vn_ref[0]
        bias = jnp.where(s <= t, -slope * (t - s).astype(F32), NEG_INF)
        init = (jnp.full(m_ref.shape, NEG_INF, F32), jnp.zeros(l_ref.shape, F32),
                jnp.zeros(acc_ref.shape, F32))
        m_ref[...], l_ref[...], acc_ref[...] = visit([kpad_ref[...]], [vpad_ref[...]], [bias], init)

    biases = []
    for i in range(pages_per_step):
        kpos = (npages - 1 - (j * pages_per_step + i)) * page + s
        biases.append(-slope * (past_len + t - kpos).astype(F32))
    m_ref[...], l_ref[...], acc_ref[...] = visit([r[0] for r in k_refs], [r[0] for r in v_refs], biases,
                                                 (m_ref[...], l_ref[...], acc_ref[...]))

    @pl.when(j == pl.num_programs(1) - 1)
    def _():
        lam = _diff_lambda(lqk_ref, lam_init)
        o3 = (acc_ref[...] / l_ref[...]).reshape(nh, 2 * tlen, acc_ref.shape[1])
        for h in range(nh):
            o = o3[h, :tlen] - lam * o3[h, tlen:]
            o_ref[0, h] = _rms_rows(o, sub_ref[...]) * (1.0 - lam_init)


def _diff_sample(page_table, q4, knt, vn4, cache_kt, cache_v, slopes, lqk, subln, lam_init, pages_per_step):
    bsz, nmap, tlen, hd = q4.shape
    npages = page_table.shape[1]
    page = cache_kt.shape[3]
    nheads, dv = cache_v.shape[2], cache_v.shape[3]
    rows = nmap * tlen
    slope_rows = jnp.broadcast_to(jnp.repeat(slopes, rows // nheads)[:, None], (rows, page))
    seq = lambda a: pl.BlockSpec((1,) + a.shape[1:], lambda b, j, pt: (b, 0, 0, 0))
    const2 = lambda shape: pl.BlockSpec(shape, lambda b, j, pt: (0, 0))
    grid_spec = pltpu.PrefetchScalarGridSpec(
        num_scalar_prefetch=1,
        grid=(bsz, npages // pages_per_step),
        in_specs=([seq(q4), seq(knt), seq(vn4), const2((rows, page)), const2(lqk.shape), const2((1, dv))]
                  + _page_specs(page_table, pages_per_step, (1, nmap, hd, page))
                  + _page_specs(page_table, pages_per_step, (1, page, nheads, dv))),
        out_specs=pl.BlockSpec((1, nheads, tlen, dv), lambda b, j, pt: (b, 0, 0, 0)),
        scratch_shapes=[pltpu.VMEM((nmap, hd, page), F32), pltpu.VMEM((page, nheads, dv), F32),
                        pltpu.VMEM((rows, dv), F32),
                        pltpu.VMEM((rows, 1), F32), pltpu.VMEM((rows, 1), F32)])
    return pl.pallas_call(
        functools.partial(_diff_sample_kernel, pages_per_step=pages_per_step, scale=HEAD64 ** -0.5 * LOG2E,
                          lam_init=lam_init, past_len=npages * page),
        grid_spec=grid_spec,
        out_shape=jax.ShapeDtypeStruct((bsz, nheads, tlen, dv), F32),
        compiler_params=_cp(("parallel", "arbitrary")),
        name="diff_sample",
    )(page_table, q4, knt, vn4, slope_rows, lqk, subln.astype(F32).reshape(1, dv),
      *([cache_kt] * pages_per_step), *([cache_v] * pages_per_step))


def _largest_tile(m, cap, mult=16):
    t = min(m, cap)
    while m % t or t % mult:
        t -= mult
    return t


def _layer0(x, bsz, tlen, n_valid, s5_re0, s5_im0, rwkv_s0, shift_prev, wt):
    m = x.shape[0]
    tm = _largest_tile(m, 640)
    u = _proj(x, wt['norm_mix0'], wt['w_in0_u'], tm, 512)
    p = _proj(x, wt['norm_mix0'], wt['w_in0_p'], tm, 1152)

    bu_re, bu_im = _s5_b(u, wt['s5_wb_re'], wt['s5_wb_im'], tm)
    width = bu_re.shape[1]
    nb_s5, tb_s5 = (1, LANE) if bsz == 1 else (_largest_tile(bsz, 16, 1), tlen)
    h_re, h_im, f_re, f_im = _s5_scan(bu_re.reshape(bsz, tlen, width), bu_im.reshape(bsz, tlen, width),
                                      s5_re0, s5_im0, wt['s5_lb_re'], wt['s5_lb_im'], nb_s5, tb_s5)
    if n_valid < tlen:
        f_re, f_im = h_re[:, n_valid - 1], h_im[:, n_valid - 1]
    ya = _s5_c(h_re.reshape(m, width), h_im.reshape(m, width), u, wt['s5_wc_re'], wt['s5_wc_im'],
               wt['s5_d'], wt['s5_w_glu'], _largest_tile(m, 256))

    c = p.shape[1]
    if bsz == 1:
        tr = _largest_tile(tlen, 320)
        prev = jnp.concatenate([shift_prev.astype(F32), p[tr - 1::tr][:-1]])[:, None, :]
        p3, nb = p.reshape(tlen // tr, tr, c), 1
    else:
        prev = shift_prev.astype(F32)[:, None, :]
        p3, nb = p.reshape(bsz, tlen, c), _largest_tile(bsz, 32, 1)
    r, lw, k, v, kk, kka, g = _rwkv_pre(p3, prev, wt['rwkv_mu'], wt['rwkv_w0'], wt['rwkv_w2'],
                                        wt['rwkv_a0'], wt['rwkv_a2'], wt['rwkv_g2'], wt['rwkv_k_k'],
                                        wt['rwkv_k_a'], nb, m if bsz > 1 else n_valid)
    w_r = r.shape[1]
    if bsz == 1:
        v3 = lambda a: a.reshape(bsz, tlen, w_r)
        y, s_last = _rwkv_scan(v3(r), v3(lw), v3(k), v3(v), v3(kk), v3(kka), rwkv_s0,
                               _largest_tile(tlen, 640, LANE), HEAD64)
        y = y.reshape(m, w_r)
    else:
        lanes_b = lambda a: jnp.transpose(a.reshape(bsz, tlen, w_r // HEAD64, HEAD64), (1, 2, 3, 0))
        y, s_last = _rwkv_steps(lanes_b(r), lanes_b(lw), lanes_b(k), lanes_b(v), lanes_b(kk), lanes_b(kka),
                                jnp.transpose(rwkv_s0.astype(F32), (1, 2, 3, 0)))
        y = jnp.transpose(y, (3, 0, 1, 2)).reshape(m, w_r)
        s_last = jnp.transpose(s_last, (3, 0, 1, 2))
    yb = _rwkv_post(y, r, k, v, g, wt['rwkv_ln_w'], wt['rwkv_ln_b'], wt['rwkv_r_k'], _largest_tile(m, 512))

    x = _outproj(x, ya, yb, wt['w_out0_a'], wt['w_out0_b'], tm, 512)
    x = _ffn(x, wt['norm_ffn0'], wt['ffn_w1'], wt['ffn_w3'], wt['ffn_w2'], tm, 512)
    return x, f_re, f_im, s_last, p


def kernel(x_prompt, x_sample, state_s5_re, state_s5_im, state_rwkv, state_rwkv_shift, cache_sb_k, cache_sb_v, cache_diff_k, cache_diff_v, page_table, meta_tokens, norm_mix0, w_in0, s5_lambda_re, s5_lambda_im, s5_log_dt, s5_b_re, s5_b_im, s5_c_re, s5_c_im, s5_d, s5_w_glu, rwkv_mu, rwkv_w0, rwkv_w2, rwkv_a0, rwkv_a2, rwkv_g2, rwkv_k_k, rwkv_k_a, rwkv_r_k, rwkv_ln_w, rwkv_ln_b, w_out0, norm_ffn0, ffn_w1, ffn_w3, ffn_w2, norm_mix1, w_in1, diff_q_norm, diff_k_norm, diff_lambda_q1, diff_lambda_k1, diff_lambda_q2, diff_lambda_k2, diff_subln, w_out1, norm_ffn1, moe_router, moe_w1, moe_w3, moe_w2):
    d_model = x_prompt.shape[-1]
    n_groups, n_state, grp = s5_b_re.shape
    s5_width = n_groups * grp
    rwkv_width = rwkv_w0.shape[0]
    rwkv_heads = rwkv_width // HEAD64
    sb_width = cache_sb_k.shape[2] * cache_sb_k.shape[3]
    diff_heads = cache_diff_v.shape[2]
    diff_qk_width = cache_diff_k.shape[2] * cache_diff_k.shape[3] * cache_diff_k.shape[4]
    diff_v_width = diff_heads * cache_diff_v.shape[3]
    lam_init = 0.8 - 0.6 * math.exp(-0.3)
    bf = lambda a: a.astype(BF16)

    lb_re, lb_im, bb_re, bb_im = _s5_discretize(s5_lambda_re, s5_lambda_im, s5_log_dt, s5_b_re, s5_b_im)
    to_ghp = lambda a: jnp.transpose(a, (1, 0, 2))
    wt = dict(
        norm_mix0=norm_mix0.astype(F32), w_in0_u=bf(w_in0[:, :s5_width]), w_in0_p=bf(w_in0[:, s5_width:]),
        s5_lb_re=lb_re, s5_lb_im=lb_im,
        s5_wb_re=bf(_blockdiag(to_ghp(bb_re), 8)), s5_wb_im=bf(_blockdiag(to_ghp(bb_im), 8)),
        s5_wc_re=bf(_blockdiag(jnp.transpose(s5_c_re.astype(F32), (0, 2, 1)), 16)),
        s5_wc_im=bf(_blockdiag(jnp.transpose(s5_c_im.astype(F32), (0, 2, 1)), 16)),
        s5_d=s5_d.reshape(-1), s5_w_glu=bf(s5_w_glu),
        rwkv_mu=rwkv_mu, rwkv_w0=rwkv_w0, rwkv_w2=bf(rwkv_w2), rwkv_a0=rwkv_a0, rwkv_a2=bf(rwkv_a2),
        rwkv_g2=bf(rwkv_g2), rwkv_k_k=rwkv_k_k, rwkv_k_a=rwkv_k_a, rwkv_r_k=rwkv_r_k,
        rwkv_ln_w=rwkv_ln_w, rwkv_ln_b=rwkv_ln_b,
        w_out0_a=bf(w_out0[:s5_width]), w_out0_b=bf(w_out0[s5_width:]),
        norm_ffn0=norm_ffn0.astype(F32), ffn_w1=bf(ffn_w1), ffn_w3=bf(ffn_w3), ffn_w2=bf(ffn_w2),
    )
    w_in1_b = bf(w_in1)
    w_out1_a, w_out1_b = bf(w_out1[:sb_width]), bf(w_out1[sb_width:])
    moe_w1_b, moe_w3_b, moe_w2_b = bf(moe_w1), bf(moe_w3), bf(moe_w2)
    lqk = jnp.stack([diff_lambda_q1, diff_lambda_k1, diff_lambda_q2, diff_lambda_k2]).astype(F32)
    slopes = _diff_slopes(diff_heads)
    q_col = 3 * sb_width // diff_qk_width
    npair = sb_width // LANE

    bp, seq, _ = x_prompt.shape
    assert bp == 1
    real = N_META + seq
    lp = real + (-real) % LANE
    xp = jnp.concatenate([meta_tokens.astype(F32), x_prompt[0].astype(F32), jnp.zeros((lp - real, d_model), F32)])
    tr_p = _largest_tile(lp, 640)
    tm_p = tr_p
    c_proj = w_in0.shape[1] - s5_width
    z_s5 = jnp.zeros((1, n_groups, n_state), F32)
    z_rwkv = jnp.zeros((1, rwkv_heads, HEAD64, HEAD64), F32)
    xh, p_s5_re, p_s5_im, p_rwkv, p_proj = _layer0(xp, 1, lp, real, z_s5, z_s5, z_rwkv,
                                                    jnp.zeros((1, c_proj), F32), wt)
    p_rwkv_shift = p_proj[real - 1:real]

    proj1 = _proj(xh, norm_mix1.astype(F32), w_in1_b, tm_p, 512, split=True)
    blk_dq = 3 * npair
    blk_dk = blk_dq + diff_heads
    blk_dv = blk_dk + diff_heads
    dqn = _qknorm_blocks(proj1, blk_dq, diff_heads, diff_q_norm, tr_p)
    dkn = _qknorm_blocks(proj1, blk_dk, diff_heads, diff_k_norm, tr_p)
    tq = _largest_tile(lp, 640, LANE)
    y_sb = _sb_prompt(proj1, lp, LANE, tq)
    y_d = _diff_prompt(dqn, dkn, proj1, blk_dv, slopes, lqk, diff_subln, lp, LANE, tq, lam_init)
    xh = _outproj(xh, y_sb, y_d, w_out1_a, w_out1_b, tm_p, 512)
    xh = _moe(xh, norm_ffn1.astype(F32), moe_router, moe_w1_b, moe_w3_b, moe_w2_b, tm_p, 256)
    y_prompt = xh[N_META:real][None]

    def tokens(blocks):
        return jnp.transpose(blocks[:, :real], (1, 0, 2)).reshape(1, real, -1)

    p_sb_k = tokens(proj1[npair:2 * npair]).reshape(1, real, sb_width // HEAD64, HEAD64)
    p_sb_v = tokens(proj1[2 * npair:3 * npair]).reshape(1, real, sb_width // HEAD64, HEAD64)
    p_diff_k = tokens(dkn).reshape(1, real, diff_heads, 2, HEAD64)
    p_diff_v = tokens(proj1[blk_dv:blk_dv + diff_heads]).reshape(1, real, diff_heads, diff_v_width // diff_heads)

    db, dseq, _ = x_sample.shape
    ms = db * dseq
    xs = x_sample.astype(F32).reshape(ms, d_model)
    xs, s_s5_re, s_s5_im, s_rwkv, s_proj = _layer0(xs, db, dseq, dseq, state_s5_re, state_s5_im, state_rwkv,
                                                   state_rwkv_shift, wt)
    s_rwkv_shift = s_proj.reshape(db, dseq, c_proj)[:, -1]

    tm_s = _largest_tile(ms, 640)
    proj_s = _proj(xs, norm_mix1.astype(F32), w_in1_b, tm_s, 512)
    dqn_s = _qknorm_cols(proj_s, q_col, diff_qk_width, diff_q_norm, tm_s)
    dkn_s = _qknorm_cols(proj_s, q_col + 1, diff_qk_width, diff_k_norm, tm_s)
    n_pool, page = cache_sb_k.shape[0], cache_sb_k.shape[1]
    pt = page_table.astype(jnp.int32)
    pages_per_step = next(n for n in (8, 4, 2, 1) if pt.shape[1] % n == 0)
    v_col = (3 * sb_width + 2 * diff_qk_width) // diff_v_width
    heads = lambda a: a.reshape(db, dseq, -1, HEAD64)
    q_first = lambda a: jnp.transpose(heads(a), (0, 2, 1, 3))
    t_last = lambda a: jnp.transpose(heads(a), (0, 2, 3, 1))
    slot_last = lambda c: jnp.transpose(c.reshape(n_pool, page, -1, HEAD64), (0, 2, 3, 1))
    ys_sb = _sb_sample(pt, q_first(proj_s[:, :sb_width]), t_last(proj_s[:, sb_width:2 * sb_width]),
                       t_last(proj_s[:, 2 * sb_width:3 * sb_width]), slot_last(cache_sb_k),
                       slot_last(cache_sb_v), pages_per_step)
    ys_d = _diff_sample(pt, q_first(dqn_s), t_last(dkn_s),
                        proj_s[:, v_col * diff_v_width:].reshape(db, dseq, diff_heads, -1),
                        slot_last(cache_diff_k), cache_diff_v, slopes, lqk, diff_subln, lam_init,
                        pages_per_step)
    tokens_s = lambda a: jnp.transpose(a, (0, 2, 1, 3)).reshape(ms, -1)
    xs = _outproj(xs, tokens_s(ys_sb), tokens_s(ys_d), w_out1_a, w_out1_b, tm_s, 512)
    xs = _moe(xs, norm_ffn1.astype(F32), moe_router, moe_w1_b, moe_w3_b, moe_w2_b, tm_s, 256)
    y_sample = xs.reshape(db, dseq, d_model)

    s_sb_k = proj_s[:, sb_width:2 * sb_width].reshape(db, dseq, sb_width // HEAD64, HEAD64)
    s_sb_v = proj_s[:, 2 * sb_width:3 * sb_width].reshape(db, dseq, sb_width // HEAD64, HEAD64)
    s_diff_k = dkn_s.reshape(db, dseq, diff_heads, 2, HEAD64)
    s_diff_v = proj_s[:, v_col * diff_v_width:].reshape(db, dseq, diff_heads, diff_v_width // diff_heads)

    return (y_prompt, y_sample,
            p_s5_re.reshape(1, n_groups, n_state), p_s5_im.reshape(1, n_groups, n_state), p_rwkv, p_rwkv_shift,
            p_sb_k, p_sb_v, p_diff_k, p_diff_v,
            s_s5_re.reshape(db, n_groups, n_state), s_s5_im.reshape(db, n_groups, n_state), s_rwkv, s_rwkv_shift,
            s_sb_k, s_sb_v, s_diff_k, s_diff_v)
```

```python
import functools
import math

import jax
import jax.numpy as jnp
from jax import lax
from jax.experimental import pallas as pl
from jax.experimental.pallas import tpu as pltpu

F32 = jnp.float32
BF16 = jnp.bfloat16
HIGHEST = lax.Precision.HIGHEST

LANE = 128
HEAD64 = 64
RMS_EPS = 1e-6
RWKV_LN_EPS = 64e-5
N_META = 16
NEG_INF = -1e30
LOG2E = math.log2(math.e)
VMEM_LIMIT = 56 * 1024 * 1024


def _cp(sem, vmem=VMEM_LIMIT):
    return pltpu.CompilerParams(dimension_semantics=sem, vmem_limit_bytes=vmem)


def _dot(a, b, precision=None):
    return jnp.dot(a, b, preferred_element_type=F32, precision=precision)


def _dot_nt(a, b, precision=None):
    return lax.dot_general(a, b, (((1,), (1,)), ((), ())), preferred_element_type=F32,
                           precision=precision)


def _dot_tn(a, b, precision=None):
    return lax.dot_general(a, b, (((0,), (0,)), ((), ())), preferred_element_type=F32,
                           precision=precision)


def _split_bf16(x):
    hi = x.astype(BF16)
    lo = (x - hi.astype(F32)).astype(BF16)
    return hi, lo


_NN = ((1,), (0,))
_NT = ((1,), (1,))
_TN = ((0,), (0,))


def _mm3(a, b, dims):
    dg = lambda x, y: lax.dot_general(x, y, (dims, ((), ())), preferred_element_type=F32)
    return dg(a[0], b[0]) + dg(a[0], b[1]) + dg(a[1], b[0])


def _rms_rows(x, gain):
    ms = jnp.mean(x * x, axis=-1, keepdims=True)
    return x * lax.rsqrt(ms + RMS_EPS) * gain


def _softplus(t):
    return jnp.maximum(t, 0.0) + jnp.log1p(jnp.exp(-jnp.abs(t)))


def _segsum64(x, e128):
    cols = [_dot(x[:, c * LANE:(c + 1) * LANE], e128, HIGHEST) for c in range(x.shape[1] // LANE)]
    return cols[0] if len(cols) == 1 else jnp.concatenate(cols, axis=1)


def _seg_ones():
    r = lax.broadcasted_iota(jnp.int32, (LANE, LANE), 0) // HEAD64
    c = lax.broadcasted_iota(jnp.int32, (LANE, LANE), 1) // HEAD64
    return (r == c).astype(F32)


def _proj_kernel(x_ref, g_ref, w_ref, o_ref, xn_ref, *, split):
    @pl.when(pl.program_id(1) == 0)
    def _():
        xn_ref[...] = _rms_rows(x_ref[...], g_ref[...]).astype(BF16)

    acc = _dot(xn_ref[...], w_ref[...])
    if split:
        for c in range(acc.shape[1] // LANE):
            o_ref[c] = acc[:, c * LANE:(c + 1) * LANE]
    else:
        o_ref[...] = acc


def _proj(x, gain, w, tm, tn, split=False):
    m, d = x.shape
    n = w.shape[1]
    if split:
        out_shape = jax.ShapeDtypeStruct((n // LANE, m, LANE), F32)
        out_spec = pl.BlockSpec((tn // LANE, tm, LANE), lambda i, j: (j, i, 0))
    else:
        out_shape = jax.ShapeDtypeStruct((m, n), F32)
        out_spec = pl.BlockSpec((tm, tn), lambda i, j: (i, j))
    return pl.pallas_call(
        functools.partial(_proj_kernel, split=split),
        grid=(m // tm, n // tn),
        in_specs=[pl.BlockSpec((tm, d), lambda i, j: (i, 0)),
                  pl.BlockSpec((1, d), lambda i, j: (0, 0)),
                  pl.BlockSpec((d, tn), lambda i, j: (0, j))],
        out_specs=out_spec,
        out_shape=out_shape,
        scratch_shapes=[pltpu.VMEM((tm, d), BF16)],
        compiler_params=_cp(("parallel", "arbitrary")),
        name="proj",
    )(x, gain.reshape(1, d), w)


def _outproj_kernel(res_ref, a_ref, b_ref, wa_ref, wb_ref, o_ref):
    o_ref[...] = (res_ref[...] + _dot(a_ref[...].astype(BF16), wa_ref[...])
                  + _dot(b_ref[...].astype(BF16), wb_ref[...]))


def _outproj(res, a, b, wa, wb, tm, tn):
    m, n = res.shape
    ka, kb = a.shape[1], b.shape[1]
    return pl.pallas_call(
        _outproj_kernel,
        grid=(m // tm, n // tn),
        in_specs=[pl.BlockSpec((tm, tn), lambda i, j: (i, j)),
                  pl.BlockSpec((tm, ka), lambda i, j: (i, 0)),
                  pl.BlockSpec((tm, kb), lambda i, j: (i, 0)),
                  pl.BlockSpec((ka, tn), lambda i, j: (0, j)),
                  pl.BlockSpec((kb, tn), lambda i, j: (0, j))],
        out_specs=pl.BlockSpec((tm, tn), lambda i, j: (i, j)),
        out_shape=jax.ShapeDtypeStruct((m, n), F32),
        compiler_params=_cp(("parallel", "arbitrary")),
        name="outproj",
    )(res, a, b, wa, wb)


def _ffn_kernel(x_ref, g_ref, w1_ref, w3_ref, w2_ref, o_ref, xn_ref):
    @pl.when(pl.program_id(1) == 0)
    def _():
        x = x_ref[...]
        xn_ref[...] = _rms_rows(x, g_ref[...]).astype(BF16)
        o_ref[...] = x

    xn = xn_ref[...]
    h1 = _dot(xn, w1_ref[...])
    h3 = _dot(xn, w3_ref[...])
    h = (h1 * jax.nn.sigmoid(h1) * h3).astype(BF16)
    o_ref[...] += _dot(h, w2_ref[...])


def _ffn(x, gain, w1, w3, w2, tm, tf):
    m, d = x.shape
    f = w1.shape[1]
    return pl.pallas_call(
        _ffn_kernel,
        grid=(m // tm, f // tf),
        in_specs=[pl.BlockSpec((tm, d), lambda i, j: (i, 0)),
                  pl.BlockSpec((1, d), lambda i, j: (0, 0)),
                  pl.BlockSpec((d, tf), lambda i, j: (0, j)),
                  pl.BlockSpec((d, tf), lambda i, j: (0, j)),
                  pl.BlockSpec((tf, d), lambda i, j: (j, 0))],
        out_specs=pl.BlockSpec((tm, d), lambda i, j: (i, 0)),
        out_shape=jax.ShapeDtypeStruct((m, d), F32),
        scratch_shapes=[pltpu.VMEM((tm, d), BF16)],
        compiler_params=_cp(("parallel", "arbitrary")),
        name="ffn",
    )(x, gain.reshape(1, d), w1, w3, w2)


def _moe_route_kernel(x_ref, g_ref, r_ref, xn_ref, gate_ref, *, n_experts):
    tm = x_ref.shape[0]
    lane = lax.broadcasted_iota(jnp.int32, (tm, LANE), 1)
    xn = _rms_rows(x_ref[...], g_ref[...])
    xn_ref[...] = xn.astype(BF16)
    logits = jnp.where(lane < n_experts, _dot(xn, r_ref[...], HIGHEST), NEG_INF)
    v1 = jnp.max(logits, axis=-1, keepdims=True)
    i1 = jnp.min(jnp.where(logits == v1, lane, LANE), axis=-1, keepdims=True)
    rest = jnp.where(lane == i1, NEG_INF, logits)
    v2 = jnp.max(rest, axis=-1, keepdims=True)
    i2 = jnp.min(jnp.where(rest == v2, lane, LANE), axis=-1, keepdims=True)
    ex = jnp.exp(v2 - v1)
    den = 1.0 + ex
    gate_ref[...] = jnp.where(lane == i1, 1.0 / den, 0.0) + jnp.where(lane == i2, ex / den, 0.0)


def _moe_route(x, gain, router, tm):
    m, d = x.shape
    ne = router.shape[1]
    router_p = jnp.zeros((d, LANE), F32).at[:, :ne].set(router.astype(F32))
    return pl.pallas_call(
        functools.partial(_moe_route_kernel, n_experts=ne),
        grid=(m // tm,),
        in_specs=[pl.BlockSpec((tm, d), lambda i: (i, 0)),
                  pl.BlockSpec((1, d), lambda i: (0, 0)),
                  pl.BlockSpec((d, LANE), lambda i: (0, 0))],
        out_specs=(pl.BlockSpec((tm, d), lambda i: (i, 0)), pl.BlockSpec((tm, LANE), lambda i: (i, 0))),
        out_shape=(jax.ShapeDtypeStruct((m, d), BF16), jax.ShapeDtypeStruct((m, LANE), F32)),
        compiler_params=_cp(("parallel",)),
        name="moe_route",
    )(x, gain.reshape(1, d), router_p)


def _moe_kernel(cnt_ref, x_ref, xn_ref, gate_ref, gt_ref, w1_ref, w3_ref, w2_ref, o_ref,
                sel_ref, xa_ref, acc_ref, *, chunk):
    i = pl.program_id(0)
    e = pl.program_id(1)
    j = pl.program_id(2)
    tm = x_ref.shape[0]
    cnt = cnt_ref[i, e]
    chunks = [(k, slice(k * chunk, (k + 1) * chunk)) for k in range(sel_ref.shape[0] // chunk)]

    @pl.when((e == 0) & (j == 0))
    def _():
        o_ref[...] = x_ref[...]

    @pl.when((j == 0) & (cnt > 0))
    def _():
        routed = (gt_ref[pl.ds(e, 1), :] > 0.0).astype(F32)
        before = (lax.broadcasted_iota(jnp.int32, (tm, tm), 0)
                  < lax.broadcasted_iota(jnp.int32, (tm, tm), 1)).astype(BF16)
        rank = _dot(jnp.broadcast_to(routed, (8, tm)).astype(BF16), before)[0:1]
        slot = lax.broadcasted_iota(jnp.int32, (chunk, tm), 0).astype(F32)
        for k, rows in chunks:
            @pl.when(k * chunk < cnt)
            def _():
                sel = jnp.where(rank == slot + float(k * chunk), routed, 0.0).astype(BF16)
                sel_ref[rows] = sel
                xa_ref[rows] = _dot(sel, xn_ref[...]).astype(BF16)
                acc_ref[rows] = jnp.zeros((chunk, acc_ref.shape[1]), F32)

    for k, rows in chunks:
        @pl.when(k * chunk < cnt)
        def _():
            xa = xa_ref[rows]
            h1 = _dot(xa, w1_ref[0])
            h3 = _dot(xa, w3_ref[0])
            h = (h1 * jax.nn.sigmoid(h1) * h3).astype(BF16)
            acc_ref[rows] += _dot(h, w2_ref[0])

    @pl.when((j == pl.num_programs(2) - 1) & (cnt > 0))
    def _():
        lane = lax.broadcasted_iota(jnp.int32, (tm, LANE), 1)
        ge = jnp.sum(jnp.where(lane == e, gate_ref[...], 0.0), axis=-1, keepdims=True)
        for k, rows in chunks:
            @pl.when(k * chunk < cnt)
            def _():
                hi, lo = _split_bf16(acc_ref[rows])
                sel = sel_ref[rows]
                o_ref[...] += ge * (_dot_tn(sel, hi) + _dot_tn(sel, lo))


def _moe(x, gain, router, w1, w3, w2, tm, tf, chunk=192):
    m, d = x.shape
    ne, _, f = w1.shape
    xn, gate = _moe_route(x, gain, router, tm)
    counts = jnp.sum((gate[:, :ne] > 0.0).reshape(m // tm, tm, ne), axis=1).astype(jnp.int32)
    cap = -(-tm // chunk) * chunk
    grid_spec = pltpu.PrefetchScalarGridSpec(
        num_scalar_prefetch=1,
        grid=(m // tm, ne, f // tf),
        in_specs=[pl.BlockSpec((tm, d), lambda i, e, j, c: (i, 0)),
                  pl.BlockSpec((tm, d), lambda i, e, j, c: (i, 0)),
                  pl.BlockSpec((tm, LANE), lambda i, e, j, c: (i, 0)),
                  pl.BlockSpec((LANE, tm), lambda i, e, j, c: (0, i)),
                  pl.BlockSpec((1, d, tf), lambda i, e, j, c: (e, 0, j)),
                  pl.BlockSpec((1, d, tf), lambda i, e, j, c: (e, 0, j)),
                  pl.BlockSpec((1, tf, d), lambda i, e, j, c: (e, j, 0))],
        out_specs=pl.BlockSpec((tm, d), lambda i, e, j, c: (i, 0)),
        scratch_shapes=[pltpu.VMEM((cap, tm), BF16), pltpu.VMEM((cap, d), BF16), pltpu.VMEM((cap, d), F32)])
    return pl.pallas_call(
        functools.partial(_moe_kernel, chunk=chunk),
        grid_spec=grid_spec,
        out_shape=jax.ShapeDtypeStruct((m, d), F32),
        compiler_params=_cp(("parallel", "arbitrary", "arbitrary")),
        name="moe",
    )(counts, x, xn, gate, gate.T, w1, w3, w2)


def _s5_disc_kernel(lr_ref, li_ref, ldt_ref, bre_ref, bim_ref, lbre_ref, lbim_ref, bbre_ref, bbim_ref):
    lr = lr_ref[...]
    li = li_ref[...]
    dt = jnp.exp(ldt_ref[...])
    mag = jnp.exp(lr * dt)
    lb_re = mag * jnp.cos(li * dt)
    lb_im = mag * jnp.sin(li * dt)
    den = lr * lr + li * li
    f_re = ((lb_re - 1.0) * lr + lb_im * li) / den
    f_im = (lb_im * lr - (lb_re - 1.0) * li) / den
    lbre_ref[...] = lb_re
    lbim_ref[...] = lb_im
    for h in range(bre_ref.shape[0]):
        br = bre_ref[h]
        bi = bim_ref[h]
        bbre_ref[h] = f_re * br - f_im * bi
        bbim_ref[h] = f_re * bi + f_im * br


def _s5_discretize(lam_re, lam_im, log_dt, b_re, b_im):
    g, p, h = b_re.shape
    sds = jax.ShapeDtypeStruct
    return pl.pallas_call(
        _s5_disc_kernel,
        out_shape=(sds((g, p), F32), sds((g, p), F32), sds((h, g, p), F32), sds((h, g, p), F32)),
        name="s5_disc",
    )(lam_re.astype(F32), lam_im.astype(F32), log_dt.astype(F32).reshape(g, 1),
      jnp.transpose(b_re.astype(F32), (2, 0, 1)), jnp.transpose(b_im.astype(F32), (2, 0, 1)))


def _s5_b_kernel(u_ref, wre_ref, wim_ref, ore_ref, oim_ref):
    u = u_ref[...].astype(BF16)
    ore_ref[...] = _dot(u, wre_ref[0])
    oim_ref[...] = _dot(u, wim_ref[0])


def _s5_b(u, wre, wim, tm):
    m = u.shape[0]
    nc, kc, ncol = wre.shape
    spec_o = pl.BlockSpec((tm, ncol), lambda i, c: (i, c))
    spec_w = pl.BlockSpec((1, kc, ncol), lambda i, c: (c, 0, 0))
    sds = jax.ShapeDtypeStruct((m, nc * ncol), F32)
    return pl.pallas_call(
        _s5_b_kernel,
        grid=(m // tm, nc),
        in_specs=[pl.BlockSpec((tm, kc), lambda i, c: (i, c)), spec_w, spec_w],
        out_specs=(spec_o, spec_o),
        out_shape=(sds, sds),
        compiler_params=_cp(("parallel", "arbitrary")),
        name="s5_b",
    )(u, wre, wim)


def _s5_scan_kernel(bre_ref, bim_ref, h0re_ref, h0im_ref, lre_ref, lim_ref,
                    hre_ref, him_ref, fre_ref, fim_ref, cre_ref, cim_ref):
    nb, tlen = bre_ref.shape[0], bre_ref.shape[1]

    @pl.when(pl.program_id(1) == 0)
    def _():
        cre_ref[...] = h0re_ref[...]
        cim_ref[...] = h0im_ref[...]

    lre = lre_ref[...]
    lim = lim_ref[...]

    def per_seq(b, _):
        def step(t, carry):
            hr, hi = carry
            nr = lre * hr - lim * hi + bre_ref[b, t]
            ni = lre * hi + lim * hr + bim_ref[b, t]
            hre_ref[b, t] = nr
            him_ref[b, t] = ni
            return nr, ni

        hr, hi = lax.fori_loop(0, tlen, step, (cre_ref[b], cim_ref[b]), unroll=8)
        cre_ref[b] = hr
        cim_ref[b] = hi
        return 0

    lax.fori_loop(0, nb, per_seq, 0)
    fre_ref[...] = cre_ref[...]
    fim_ref[...] = cim_ref[...]


def _s5_scan(bu_re, bu_im, h0_re, h0_im, lb_re, lb_im, nb, tb):
    bsz, length, width = bu_re.shape
    sub = width // LANE
    v4 = lambda a: a.reshape(bsz, length, sub, LANE)
    v3 = lambda a: a.astype(F32).reshape(bsz, sub, LANE)
    spec_x = pl.BlockSpec((nb, tb, sub, LANE), lambda b, t: (b, t, 0, 0))
    spec_h = pl.BlockSpec((nb, sub, LANE), lambda b, t: (b, 0, 0))
    spec_l = pl.BlockSpec((sub, LANE), lambda b, t: (0, 0))
    sds_x = jax.ShapeDtypeStruct((bsz, length, sub, LANE), F32)
    sds_h = jax.ShapeDtypeStruct((bsz, sub, LANE), F32)
    hre, him, fre, fim = pl.pallas_call(
        _s5_scan_kernel,
        grid=(bsz // nb, length // tb),
        in_specs=[spec_x, spec_x, spec_h, spec_h, spec_l, spec_l],
        out_specs=(spec_x, spec_x, spec_h, spec_h),
        out_shape=(sds_x, sds_x, sds_h, sds_h),
        scratch_shapes=[pltpu.VMEM((nb, sub, LANE), F32), pltpu.VMEM((nb, sub, LANE), F32)],
        compiler_params=_cp(("parallel", "arbitrary")),
        name="s5_scan",
    )(v4(bu_re), v4(bu_im), v3(h0_re), v3(h0_im), lb_re.reshape(sub, LANE), lb_im.reshape(sub, LANE))
    return hre.reshape(bsz, length, width), him.reshape(bsz, length, width), fre, fim


def _s5_c_kernel(hre_ref, him_ref, u_ref, wcre_ref, wcim_ref, d_ref, wglu_ref, o_ref):
    nc, kc, _ = wcre_ref.shape
    cols = []
    for c in range(nc):
        hr = hre_ref[:, c * kc:(c + 1) * kc].astype(BF16)
        hi = him_ref[:, c * kc:(c + 1) * kc].astype(BF16)
        cols.append(_dot(hr, wcre_ref[c]) - _dot(hi, wcim_ref[c]))
    y = jnp.concatenate(cols, axis=1) + d_ref[...] * u_ref[...]
    z = jax.nn.gelu(y)
    o_ref[...] = z * jax.nn.sigmoid(_dot(z.astype(BF16), wglu_ref[...]))


def _s5_c(h_re, h_im, u, wcre, wcim, d_skip, w_glu, tm):
    m, width = h_re.shape
    w = u.shape[1]
    full3 = lambda a: pl.BlockSpec(a.shape, lambda i: (0, 0, 0))
    return pl.pallas_call(
        _s5_c_kernel,
        grid=(m // tm,),
        in_specs=[pl.BlockSpec((tm, width), lambda i: (i, 0)),
                  pl.BlockSpec((tm, width), lambda i: (i, 0)),
                  pl.BlockSpec((tm, w), lambda i: (i, 0)),
                  full3(wcre), full3(wcim),
                  pl.BlockSpec((1, w), lambda i: (0, 0)),
                  pl.BlockSpec((w, w), lambda i: (0, 0))],
        out_specs=pl.BlockSpec((tm, w), lambda i: (i, 0)),
        out_shape=jax.ShapeDtypeStruct((m, w), F32),
        compiler_params=_cp(("parallel",)),
        name="s5_c",
    )(h_re, h_im, u, wcre, wcim, d_skip.astype(F32).reshape(1, w), w_glu)


def _blockdiag(w, nblk):
    g, a, b = w.shape
    w4 = w.reshape(g // nblk, nblk, a, b)
    eye = jnp.eye(nblk, dtype=w.dtype)
    return jnp.einsum('cgab,gh->cgahb', w4, eye).reshape(g // nblk, nblk * a, nblk * b)


def _rwkv_pre_kernel(p_ref, prev_ref, mu_ref, w0_ref, w2_ref, a0_ref, a2_ref, g2_ref, kk_ref, ka_ref,
                     r_o, lw_o, k_o, v_o, kk_o, kka_o, g_o, *, width, lora_w, lora_a, n_valid):
    nb, tlen, c = p_ref.shape
    p = p_ref[...]
    tidx = lax.broadcasted_iota(jnp.int32, (nb, tlen, c), 1)
    prev = jnp.where(tidx == 0, prev_ref[...], pltpu.roll(p, 1, axis=1))
    xm = (p + (prev - p) * mu_ref[...]).reshape(nb * tlen, c)
    r = xm[:, :width]
    k = xm[:, width:2 * width]
    v = xm[:, 2 * width:3 * width]
    o = 3 * width
    w_lo = xm[:, o:o + lora_w]
    a_lo = xm[:, o + lora_w:o + lora_w + lora_a]
    g_lo = xm[:, o + lora_w + lora_a:]
    wraw = -_softplus(-(w0_ref[...] + _dot(jnp.tanh(w_lo).astype(BF16), w2_ref[...]))) - 0.5
    a = jax.nn.sigmoid(a0_ref[...] + _dot(a_lo.astype(BF16), a2_ref[...]))
    g = _dot(jax.nn.sigmoid(g_lo).astype(BF16), g2_ref[...])
    kkr = k * kk_ref[...]
    nrm = jnp.sqrt(_segsum64(kkr * kkr, _seg_ones()))
    kk = kkr / jnp.maximum(nrm, 1e-12)
    row = pl.program_id(0) * (nb * tlen) + lax.broadcasted_iota(jnp.int32, (nb * tlen, 1), 0)
    keep = (row < n_valid).astype(F32)
    r_o[...] = r
    lw_o[...] = -jnp.exp(wraw) * keep
    k_o[...] = k * (1.0 + (a - 1.0) * ka_ref[...]) * keep
    v_o[...] = v * keep
    kk_o[...] = kk * keep
    kka_o[...] = kk * a * keep
    g_o[...] = g


def _rwkv_pre(p3, prev, mu, w0, w2, a0, a2, g2, k_k, k_a, nb, n_valid):
    nbt, tlen, c = p3.shape
    width = w0.shape[0]
    lora_w, lora_a = w2.shape[0], a2.shape[0]
    rows = nb * tlen
    row1 = lambda a: a.astype(F32).reshape(1, -1)
    full2 = lambda a: pl.BlockSpec(a.shape, lambda i: (0, 0))
    spec_o = pl.BlockSpec((rows, width), lambda i: (i, 0))
    sds = jax.ShapeDtypeStruct((nbt * tlen, width), F32)
    args = (mu.astype(F32).reshape(1, 1, c), row1(w0), w2, row1(a0), a2, g2, row1(k_k), row1(k_a))
    return pl.pallas_call(
        functools.partial(_rwkv_pre_kernel, width=width, lora_w=lora_w, lora_a=lora_a, n_valid=n_valid),
        grid=(nbt // nb,),
        in_specs=[pl.BlockSpec((nb, tlen, c), lambda i: (i, 0, 0)),
                  pl.BlockSpec((nb, 1, c), lambda i: (i, 0, 0)),
                  pl.BlockSpec((1, 1, c), lambda i: (0, 0, 0))] + [full2(a) for a in args[1:]],
        out_specs=(spec_o,) * 7,
        out_shape=(sds,) * 7,
        compiler_params=_cp(("parallel",)),
        name="rwkv_pre",
    )(p3, prev, *args)


def _rwkv_scan_kernel(r_ref, lw_ref, k_ref, v_ref, kk_ref, kka_ref, s0_ref, y_ref, sl_ref, s_ref, *, tc):
    tb = r_ref.shape[1]
    npair = r_ref.shape[2] // LANE
    n2 = 2 * tc
    lane = lax.broadcasted_iota(jnp.int32, (tc, LANE), 1)
    first = lane < HEAD64
    ri = lax.broadcasted_iota(jnp.int32, (n2, n2), 0)
    ci = lax.broadcasted_iota(jnp.int32, (n2, n2), 1)
    same = (ri // tc) == (ci // tc)
    strict = same & ((ci % tc) < (ri % tc))
    incl = same & ((ci % tc) <= (ri % tc))
    eye = (ri == ci).astype(F32)
    trow = lax.broadcasted_iota(jnp.int32, (tc, LANE), 0)
    shifts = [1 << b for b in range(max(tc - 1, 0).bit_length())]
    n_sq = max(int(math.ceil(math.log2(tc))) - 1, 0)

    def stack(x):
        return jnp.concatenate([jnp.where(first, x, 0.0), jnp.where(first, 0.0, x)], axis=0)

    @pl.when(pl.program_id(1) == 0)
    def _():
        for pr in range(npair):
            sa = jnp.concatenate([s0_ref[0, 2 * pr], jnp.zeros((HEAD64, HEAD64), F32)], axis=1)
            sb = jnp.concatenate([jnp.zeros((HEAD64, HEAD64), F32), s0_ref[0, 2 * pr + 1]], axis=1)
            s_ref[pr] = jnp.concatenate([sa, sb], axis=0)

    def prepare(rows, pr):
        lanes = slice(pr * LANE, (pr + 1) * LANE)
        lw = lw_ref[0, rows, lanes]
        c = lw
        for sh in shifts:
            c = c + jnp.where(trow >= sh, pltpu.roll(c, sh, axis=0), 0.0)
        c_last = c[tc - 1:tc, :]
        kk = kk_ref[0, rows, lanes]
        kka = kka_ref[0, rows, lanes]
        kx = k_ref[0, rows, lanes]
        e_neg = jnp.exp(-c)
        e_end = jnp.exp(c_last - c)
        ops = (-kk * jnp.exp(c - lw), r_ref[0, rows, lanes] * jnp.exp(c), kka * e_neg, kx * e_neg,
               kka * e_end, kx * e_end, v_ref[0, rows, lanes])
        return tuple(_split_bf16(stack(x)) for x in ops) + (jnp.exp(c_last),)

    def chunk(ci_, _):
        rows = pl.ds(pl.multiple_of(ci_ * tc, tc), tc)
        prs = range(npair)
        al, rt, bt, kt, bh, kh, vs, w_end = zip(*[prepare(rows, pr) for pr in prs])
        nab = [jnp.where(strict, _mm3(al[p], bt[p], _NT), 0.0) for p in prs]
        aak = [jnp.where(strict, _mm3(al[p], kt[p], _NT), 0.0) for p in prs]
        arb = [jnp.where(incl, _mm3(rt[p], bt[p], _NT), 0.0) for p in prs]
        ark = [jnp.where(incl, _mm3(rt[p], kt[p], _NT), 0.0) for p in prs]
        inv = [eye + nab[p] for p in prs]
        pw = nab
        for _ in range(n_sq):
            pw_s = [_split_bf16(x) for x in pw]
            pw = [_mm3(pw_s[p], pw_s[p], _NN) for p in prs]
            inv = [inv[p] + _mm3(_split_bf16(inv[p]), _split_bf16(pw[p]), _NN) for p in prs]
        s0 = [s_ref[p] for p in prs]
        s0_s = [_split_bf16(x) for x in s0]
        rhs = [_mm3(al[p], s0_s[p], _NT) + _mm3(_split_bf16(aak[p]), vs[p], _NN) for p in prs]
        u = [_split_bf16(_mm3(_split_bf16(inv[p]), _split_bf16(rhs[p]), _NN)) for p in prs]
        for p in prs:
            ys = (_mm3(rt[p], s0_s[p], _NT) + _mm3(_split_bf16(arb[p]), u[p], _NN)
                  + _mm3(_split_bf16(ark[p]), vs[p], _NN))
            y_ref[0, rows, p * LANE:(p + 1) * LANE] = ys[:tc] + ys[tc:]
            s_ref[p] = s0[p] * w_end[p] + _mm3(u[p], bh[p], _TN) + _mm3(vs[p], kh[p], _TN)
        return 0

    lax.fori_loop(0, tb // tc, chunk, 0)

    for pr in range(npair):
        s = s_ref[pr]
        sl_ref[0, 2 * pr] = s[:HEAD64, :HEAD64]
        sl_ref[0, 2 * pr + 1] = s[HEAD64:, HEAD64:]


def _rwkv_scan(r, lw, k, v, kk, kka, s0, tb, tc):
    bsz, length, width = r.shape
    spec_x = pl.BlockSpec((1, tb, width), lambda b, t: (b, t, 0))
    spec_s = pl.BlockSpec((1,) + s0.shape[1:], lambda b, t: (b, 0, 0, 0))
    return pl.pallas_call(
        functools.partial(_rwkv_scan_kernel, tc=tc),
        grid=(bsz, length // tb),
        in_specs=[spec_x] * 6 + [spec_s],
        out_specs=(spec_x, spec_s),
        out_shape=(jax.ShapeDtypeStruct((bsz, length, width), F32),
                   jax.ShapeDtypeStruct(s0.shape, F32)),
        scratch_shapes=[pltpu.VMEM((width // LANE, LANE, LANE), F32)],
        compiler_params=_cp(("parallel", "arbitrary")),
        name="rwkv_scan",
    )(r, lw, k, v, kk, kka, s0.astype(F32))


def _rwkv_steps_kernel(r_ref, lw_ref, k_ref, v_ref, kk_ref, kka_ref, s0_ref, y_ref, sl_ref):
    tlen = r_ref.shape[0]
    n = r_ref.shape[2]
    sl_ref[...] = s0_ref[...]
    for t in range(tlen):
        w = jnp.exp(lw_ref[t, 0])
        kk = kk_ref[t, 0]
        kka = kka_ref[t, 0]
        kx = k_ref[t, 0]
        rx = r_ref[t, 0]

        def row(i, _):
            s = sl_ref[0, i]
            sa = -jnp.sum(s * kk, axis=0, keepdims=True)
            s = s * w + sa * kka + v_ref[t, 0, pl.ds(i, 1), :] * kx
            sl_ref[0, i] = s
            y_ref[t, 0, pl.ds(i, 1), :] = jnp.sum(s * rx, axis=0, keepdims=True)
            return 0

        lax.fori_loop(0, n, row, 0, unroll=4)


def _rwkv_steps(r, lw, k, v, kk, kka, s0):
    tlen, nh, n, bsz = r.shape
    spec_x = pl.BlockSpec((tlen, 1, n, bsz), lambda h: (0, h, 0, 0))
    spec_s = pl.BlockSpec((1, n, n, bsz), lambda h: (h, 0, 0, 0))
    return pl.pallas_call(
        _rwkv_steps_kernel,
        grid=(nh,),
        in_specs=[spec_x] * 6 + [spec_s],
        out_specs=(spec_x, spec_s),
        out_shape=(jax.ShapeDtypeStruct(r.shape, F32), jax.ShapeDtypeStruct(s0.shape, F32)),
        compiler_params=_cp(("parallel",)),
        name="rwkv_steps",
    )(r, lw, k, v, kk, kka, s0)


def _rwkv_post_kernel(y_ref, r_ref, k_ref, v_ref, g_ref, lnw_ref, lnb_ref, rk_ref, o_ref):
    e128 = _seg_ones()
    y = y_ref[...]
    mean = _segsum64(y, e128) * (1.0 / HEAD64)
    d = y - mean
    var = _segsum64(d * d, e128) * (1.0 / HEAD64)
    yn = d * lax.rsqrt(var + RWKV_LN_EPS) * lnw_ref[...] + lnb_ref[...]
    bonus = _segsum64(r_ref[...] * k_ref[...] * rk_ref[...], e128) * v_ref[...]
    o_ref[...] = (yn + bonus) * g_ref[...]


def _rwkv_post(y, r, k, v, g, ln_w, ln_b, r_k, tm):
    m, w = y.shape
    spec_x = pl.BlockSpec((tm, w), lambda i: (i, 0))
    spec_p = pl.BlockSpec((1, w), lambda i: (0, 0))
    row1 = lambda a: a.astype(F32).reshape(1, w)
    return pl.pallas_call(
        _rwkv_post_kernel,
        grid=(m // tm,),
        in_specs=[spec_x] * 5 + [spec_p] * 3,
        out_specs=spec_x,
        out_shape=jax.ShapeDtypeStruct((m, w), F32),
        compiler_params=_cp(("parallel",)),
        name="rwkv_post",
    )(y, r, k, v, g, row1(ln_w), row1(ln_b), row1(r_k))


def _qknorm_kernel(x_ref, g_ref, o_ref):
    x = x_ref[...]
    x2 = x.reshape(-1, x.shape[-1])
    ms = _segsum64(x2 * x2, _seg_ones()) * (1.0 / HEAD64)
    o_ref[...] = (x2 * lax.rsqrt(ms + RMS_EPS) * g_ref[...]).reshape(x.shape)


def _qknorm_blocks(proj, first_blk, nblk, gain, tr):
    length = proj.shape[1]
    g = jnp.tile(gain.astype(F32), LANE // HEAD64).reshape(1, LANE)
    return pl.pallas_call(
        _qknorm_kernel,
        grid=(nblk, length // tr),
        in_specs=[pl.BlockSpec((1, tr, LANE), lambda h, i: (first_blk + h, i, 0)),
                  pl.BlockSpec((1, LANE), lambda h, i: (0, 0))],
        out_specs=pl.BlockSpec((1, tr, LANE), lambda h, i: (h, i, 0)),
        out_shape=jax.ShapeDtypeStruct((nblk, length, LANE), F32),
        compiler_params=_cp(("parallel", "parallel")),
        name="qknorm_blocks",
    )(proj, g)


def _qknorm_cols(proj, col_blk, width, gain, tr):
    m = proj.shape[0]
    g = jnp.tile(gain.astype(F32), width // HEAD64).reshape(1, width)
    return pl.pallas_call(
        _qknorm_kernel,
        grid=(m // tr,),
        in_specs=[pl.BlockSpec((tr, width), lambda i: (i, col_blk)),
                  pl.BlockSpec((1, width), lambda i: (0, 0))],
        out_specs=pl.BlockSpec((tr, width), lambda i: (i, 0)),
        out_shape=jax.ShapeDtypeStruct((m, width), F32),
        compiler_params=_cp(("parallel",)),
        name="qknorm_cols",
    )(proj, g)


def _sb_weights(z2, tri, mask, carry):
    return _sb_finish(z2, _sb_prepare(z2, tri, mask), mask, carry)


def _sb_prepare(z2, tri, mask):
    sign = jnp.uint32(0x80000000)
    neg_abs = lax.bitcast_convert_type(lax.bitcast_convert_type(z2, jnp.uint32) | sign, F32)
    n = jnp.maximum(z2, 0.0) + jnp.log2(1.0 + jnp.exp2(neg_abs))
    if mask is not None:
        n = jnp.where(mask, n, 0.0)
    sub = tri.shape[0]
    out = []
    for b in range(z2.shape[1] // sub - 1, -1, -1):
        nb = n[:, b * sub:(b + 1) * sub]
        out.append((_dot(nb.astype(BF16), tri), jnp.sum(nb, axis=-1, keepdims=True)))
    return out


def _sb_finish(z2, prepared, mask, carry):
    parts = []
    for rest, total in prepared:
        parts.append(carry - rest)
        carry = carry - total
    log2_rest = parts[0] if len(parts) == 1 else jnp.concatenate(parts[::-1], axis=1)
    att = jnp.exp2(z2 + log2_rest)
    if mask is not None:
        att = jnp.where(mask, att, 0.0)
    return att, carry


def _sb_block(qh, kblk, vblk, tri, mask, carry):
    att, carry = _sb_weights(_dot_nt(qh, kblk), tri, mask, carry)
    return _dot(att.astype(BF16), vblk), carry


def _later_tri(n):
    return (lax.broadcasted_iota(jnp.int32, (n, n), 0)
            >= lax.broadcasted_iota(jnp.int32, (n, n), 1)).astype(BF16)


def _stack_halves(q):
    first = lax.broadcasted_iota(jnp.int32, q.shape, 1) < HEAD64
    return jnp.concatenate([jnp.where(first, q, 0.0), jnp.where(first, 0.0, q)], axis=0)


def _key_minus_query(tq, tk):
    row = lax.broadcasted_iota(jnp.int32, (2 * tq, tk), 0)
    row = jnp.where(row >= tq, row - tq, row)
    return lax.broadcasted_iota(jnp.int32, (2 * tq, tk), 1) - row


def _sb_prompt_kernel(q_ref, k_ref, v_ref, o_ref, *, tq, tk, scale):
    qi = pl.program_id(1)
    qs = _stack_halves(q_ref[0] * scale).astype(BF16)
    tri = _later_tri(LANE)
    kmq = _key_minus_query(tq, tk)

    def block(kb, carry, masked):
        c, acc = carry
        rows = pl.ds(pl.multiple_of(kb * tk, tk), tk)
        kblk = k_ref[0, rows, :].astype(BF16)
        vblk = v_ref[0, rows, :].astype(BF16)
        mask = (kmq < qi * tq - kb * tk) if masked else None
        d, c = _sb_block(qs, kblk, vblk, tri, mask, c)
        return c, acc + d

    carry = (jnp.zeros((2 * tq, 1), F32), jnp.zeros((2 * tq, LANE), F32))
    n_full = (qi * tq) // tk
    last = ((qi + 1) * tq - 1) // tk
    carry = lax.fori_loop(0, last + 1 - n_full, lambda it, cr: block(last - it, cr, True), carry)
    def group(top, size, cr):
        c, acc = cr
        rows = [pl.ds(pl.multiple_of((top - s) * tk, tk), tk) for s in range(size)]
        zs = [_dot_nt(qs, k_ref[0, r, :].astype(BF16)) for r in rows]
        prepared = [_sb_prepare(z, tri, None) for z in zs]
        for z, prep, r in zip(zs, prepared, rows):
            att, c = _sb_finish(z, prep, None, c)
            acc = acc + _dot(att.astype(BF16), v_ref[0, r, :].astype(BF16))
        return c, acc

    top = n_full - 1
    for size in (1, 2):
        has = (n_full // size) % 2
        carry = lax.fori_loop(0, has, lambda it, cr, top=top, size=size: group(top, size, cr), carry)
        top = top - has * size
    carry = lax.fori_loop(0, n_full // 4, lambda it, cr, top=top: group(top - 4 * it, 4, cr), carry)
    acc = carry[1]
    first = lax.broadcasted_iota(jnp.int32, (tq, LANE), 1) < HEAD64
    o_ref[...] = jnp.where(first, acc[:tq], acc[tq:])


def _sb_prompt(proj, length, tq, tk):
    npair = 8
    return pl.pallas_call(
        functools.partial(_sb_prompt_kernel, tq=tq, tk=tk, scale=HEAD64 ** -0.5 * LOG2E),
        grid=(npair, length // tq),
        in_specs=[pl.BlockSpec((1, tq, LANE), lambda p, i: (p, i, 0)),
                  pl.BlockSpec((1, length, LANE), lambda p, i: (npair + p, 0, 0)),
                  pl.BlockSpec((1, length, LANE), lambda p, i: (2 * npair + p, 0, 0))],
        out_specs=pl.BlockSpec((tq, LANE), lambda p, i: (i, p)),
        out_shape=jax.ShapeDtypeStruct((length, npair * LANE), F32),
        compiler_params=_cp(("parallel", "parallel")),
        name="sb_prompt",
    )(proj, proj, proj)


def _diff_lambda(lqk_ref, lam_init):
    lqk = lqk_ref[...]
    s1 = jnp.sum(lqk[0:1] * lqk[1:2], axis=-1, keepdims=True)
    s2 = jnp.sum(lqk[2:3] * lqk[3:4], axis=-1, keepdims=True)
    return jnp.exp(s1) - jnp.exp(s2) + lam_init


def _softmax_block(qh, kblk, vblk, bias, state):
    m, l, acc = state
    s = _dot_nt(qh, kblk) + bias
    m_new = jnp.maximum(m, jnp.max(s, axis=-1, keepdims=True))
    p = jnp.exp2(s - m_new)
    corr = jnp.exp2(m - m_new)
    return (m_new, corr * l + jnp.sum(p, axis=-1, keepdims=True),
            corr * acc + _dot(p.astype(BF16), vblk))


def _diff_prompt_kernel(q_ref, k_ref, v_ref, slope_ref, lqk_ref, sub_ref, o_ref, *,
                        tq, tk, scale, lam_init):
    qi = pl.program_id(1)
    qs = _stack_halves(q_ref[0] * scale).astype(BF16)
    slope = slope_ref[0][:, :1] * LOG2E
    kmq = _key_minus_query(tq, tk)
    alibi = slope * kmq.astype(F32)

    def block(kb, state, masked):
        rows = pl.ds(pl.multiple_of(kb * tk, tk), tk)
        kblk = k_ref[0, rows, :].astype(BF16)
        vblk = v_ref[0, rows, :].astype(BF16)
        off = kb * tk - qi * tq
        bias = alibi + slope * off.astype(F32)
        if masked:
            bias = jnp.where(kmq <= -off, bias, NEG_INF)
        return _softmax_block(qs, kblk, vblk, bias, state)

    state = (jnp.full((2 * tq, 1), NEG_INF, F32), jnp.zeros((2 * tq, 1), F32),
             jnp.zeros((2 * tq, LANE), F32))
    n_full = (qi * tq) // tk
    last = ((qi + 1) * tq - 1) // tk
    def group(kb0, size, st):
        m_old, l_old, acc = st
        rows = [pl.ds(pl.multiple_of((kb0 + b) * tk, tk), tk) for b in range(size)]
        ss = [_dot_nt(qs, k_ref[0, r, :].astype(BF16)) + alibi
              + slope * ((kb0 + b) * tk - qi * tq).astype(F32) for b, r in enumerate(rows)]
        m_new = m_old
        for s in ss:
            m_new = jnp.maximum(m_new, jnp.max(s, axis=-1, keepdims=True))
        corr = jnp.exp2(m_old - m_new)
        l_new = corr * l_old
        acc = corr * acc
        for s, r in zip(ss, rows):
            p = jnp.exp2(s - m_new)
            l_new = l_new + jnp.sum(p, axis=-1, keepdims=True)
            acc = acc + _dot(p.astype(BF16), v_ref[0, r, :].astype(BF16))
        return m_new, l_new, acc

    state = lax.fori_loop(0, n_full // 4, lambda it, st: group(4 * it, 4, st), state)
    done = n_full // 4 * 4
    for size in (2, 1):
        has = (n_full // size) % 2
        state = lax.fori_loop(0, has, lambda it, st, done=done, size=size: group(done, size, st), state)
        done = done + has * size
    state = lax.fori_loop(n_full, last + 1, lambda kb, st: block(kb, st, True), state)
    _, l, acc = state
    lam = _diff_lambda(lqk_ref, lam_init)
    o = acc[:tq] / l[:tq] - lam * (acc[tq:] / l[tq:])
    o_ref[...] = _rms_rows(o, sub_ref[...]) * (1.0 - lam_init)


def _diff_slopes(nheads):
    return jnp.exp2(-8.0 * jnp.arange(1, nheads + 1, dtype=F32) / nheads)


def _diff_prompt(qn, kn, proj, v_blk0, slopes, lqk, subln, length, tq, tk, lam_init):
    nheads = qn.shape[0]
    return pl.pallas_call(
        functools.partial(_diff_prompt_kernel, tq=tq, tk=tk, scale=HEAD64 ** -0.5 * LOG2E, lam_init=lam_init),
        grid=(nheads, length // tq),
        in_specs=[pl.BlockSpec((1, tq, LANE), lambda h, i: (h, i, 0)),
                  pl.BlockSpec((1, length, LANE), lambda h, i: (h, 0, 0)),
                  pl.BlockSpec((1, length, LANE), lambda h, i: (v_blk0 + h, 0, 0)),
                  pl.BlockSpec((1, 1, LANE), lambda h, i: (h, 0, 0)),
                  pl.BlockSpec(lqk.shape, lambda h, i: (0, 0)),
                  pl.BlockSpec((1, LANE), lambda h, i: (0, 0))],
        out_specs=pl.BlockSpec((tq, LANE), lambda h, i: (i, h)),
        out_shape=jax.ShapeDtypeStruct((length, nheads * LANE), F32),
        compiler_params=_cp(("parallel", "parallel")),
        name="diff_prompt",
    )(qn, kn, proj, jnp.broadcast_to(slopes.reshape(nheads, 1, 1), (nheads, 1, LANE)), lqk,
      subln.astype(F32).reshape(1, LANE))


def _page_specs(page_table, pages_per_step, block):
    npages = page_table.shape[1]
    zeros = (0,) * (len(block) - 1)

    def spec(i):
        return pl.BlockSpec(block, lambda b, j, pt: (pt[b, npages - 1 - (j * pages_per_step + i)],) + zeros)

    return [spec(i) for i in range(pages_per_step)]


def _sb_sample_kernel(pt_ref, q_ref, knt_ref, vnt_ref, *refs, pages_per_step, scale):
    del pt_ref
    k_refs, v_refs = refs[:pages_per_step], refs[pages_per_step:2 * pages_per_step]
    o_ref, kpad_ref, vpad_ref, acc_ref, car_ref = refs[2 * pages_per_step:]
    j = pl.program_id(1)
    nh, tlen, _ = q_ref.shape[1:]
    tnew = knt_ref.shape[3]
    page = kpad_ref.shape[2]
    rows = nh * tlen
    tri = _later_tri(page)
    q = (q_ref[0] * scale).astype(BF16)

    def visit(kt, vt, mask, state):
        car, acc = state
        z = jnp.einsum('htd,hds->hts', q, kt.astype(BF16), preferred_element_type=F32)
        att, car = _sb_weights(z.reshape(rows, page), tri, mask, car)
        return car, acc + jnp.einsum('hts,hes->hte', att.reshape(nh, tlen, page).astype(BF16),
                                     vt.astype(BF16), preferred_element_type=F32)

    @pl.when(j == 0)
    def _():
        kpad_ref[...] = jnp.zeros(kpad_ref.shape, F32)
        vpad_ref[...] = jnp.zeros(vpad_ref.shape, F32)
        kpad_ref[:, :, :tnew] = knt_ref[0]
        vpad_ref[:, :, :tnew] = vnt_ref[0]
        t = lax.broadcasted_iota(jnp.int32, (rows, page), 0) % tlen
        s = lax.broadcasted_iota(jnp.int32, (rows, page), 1)
        car_ref[...], acc_ref[...] = visit(kpad_ref[...], vpad_ref[...], s < t,
                                           (jnp.zeros(car_ref.shape, F32), jnp.zeros(acc_ref.shape, F32)))

    zs = [jnp.einsum('htd,hds->hts', q, k_ref[0].astype(BF16), preferred_element_type=F32).reshape(rows, page)
          for k_ref in k_refs]
    prepared = [_sb_prepare(z, tri, None) for z in zs]
    car = car_ref[...]
    acc = acc_ref[...]
    for z, prep, v_ref in zip(zs, prepared, v_refs):
        att, car = _sb_finish(z, prep, None, car)
        acc = acc + jnp.einsum('hts,hes->hte', att.reshape(nh, tlen, page).astype(BF16),
                               v_ref[0].astype(BF16), preferred_element_type=F32)
    car_ref[...] = car
    acc_ref[...] = acc

    @pl.when(j == pl.num_programs(1) - 1)
    def _():
        o_ref[0] = acc_ref[...]


def _sb_sample(page_table, q4, knt, vnt, cache_kt, cache_vt, pages_per_step):
    bsz, nh, tlen, hd = q4.shape
    npages = page_table.shape[1]
    page = cache_kt.shape[3]
    seq = lambda a: pl.BlockSpec((1,) + a.shape[1:], lambda b, j, pt: (b, 0, 0, 0))
    pages = _page_specs(page_table, pages_per_step, (1, nh, hd, page))
    grid_spec = pltpu.PrefetchScalarGridSpec(
        num_scalar_prefetch=1,
        grid=(bsz, npages // pages_per_step),
        in_specs=[seq(q4), seq(knt), seq(vnt)] + pages + pages,
        out_specs=seq(q4),
        scratch_shapes=[pltpu.VMEM((nh, hd, page), F32), pltpu.VMEM((nh, hd, page), F32),
                        pltpu.VMEM((nh, tlen, hd), F32), pltpu.VMEM((nh * tlen, 1), F32)])
    return pl.pallas_call(
        functools.partial(_sb_sample_kernel, pages_per_step=pages_per_step, scale=HEAD64 ** -0.5 * LOG2E),
        grid_spec=grid_spec,
        out_shape=jax.ShapeDtypeStruct(q4.shape, F32),
        compiler_params=_cp(("parallel", "arbitrary")),
        name="sb_sample",
    )(page_table, q4, knt, vnt, *([cache_kt] * pages_per_step), *([cache_vt] * pages_per_step))


def _diff_sample_kernel(pt_ref, q_ref, knt_ref, vn_ref, slope_ref, lqk_ref, sub_ref, *refs,
                        pages_per_step, scale, lam_init, past_len):
    del pt_ref
    k_refs, v_refs = refs[:pages_per_step], refs[pages_per_step:2 * pages_per_step]
    o_ref, kpad_ref, vpad_ref, acc_ref, m_ref, l_ref = refs[2 * pages_per_step:]
    j = pl.program_id(1)
    nmap, tlen, _ = q_ref.shape[1:]
    nh = nmap // 2
    tnew = knt_ref.shape[3]
    page = kpad_ref.shape[2]
    npages = pl.num_programs(1) * pages_per_step
    rows = nmap * tlen
    t = lax.broadcasted_iota(jnp.int32, (rows, page), 0) % tlen
    s = lax.broadcasted_iota(jnp.int32, (rows, page), 1)
    slope = slope_ref[...] * LOG2E
    q = (q_ref[0] * scale).astype(BF16)

    wide = page * nh
    spread = (lax.broadcasted_iota(jnp.int32, (page, wide), 1) // nh
              == lax.broadcasted_iota(jnp.int32, (page, wide), 0)).astype(BF16)
    own_head = (lax.broadcasted_iota(jnp.int32, (rows, wide), 1) % nh
                == lax.broadcasted_iota(jnp.int32, (rows, wide), 0) // (2 * tlen))

    def visit(kts, v2s, biases, state):
        m_old, l_old, acc = state
        scs = [jnp.einsum('rtd,rds->rts', q, kt.astype(BF16), preferred_element_type=F32).reshape(rows, page)
               + bias for kt, bias in zip(kts, biases)]
        m_new = m_old
        for sc in scs:
            m_new = jnp.maximum(m_new, jnp.max(sc, axis=-1, keepdims=True))
        corr = jnp.exp2(m_old - m_new)
        l_new = corr * l_old
        acc = corr * acc
        for sc, v2 in zip(scs, v2s):
            p = jnp.exp2(sc - m_new)
            l_new = l_new + jnp.sum(p, axis=-1, keepdims=True)
            pw = jnp.where(own_head, _dot(p.astype(BF16), spread), 0.0).astype(BF16)
            acc = acc + _dot(pw, v2.reshape(wide, v2.shape[-1]).astype(BF16))
        return m_new, l_new, acc

    @pl.when(j == 0)
    def _():
        kpad_ref[...] = jnp.zeros(kpad_ref.shape, F32)
        vpad_ref[...] = jnp.zeros(vpad_ref.shape, F32)
        kpad_ref[:, :, :tnew] = knt_ref[0]
        vpad_ref[:tnew] = vn_ref[0]
        bias = jnp.where(s <= t, -slope * (t - s).astype(F32), NEG_INF)
        init = (jnp.full(m_ref.shape, NEG_INF, F32), jnp.zeros(l_ref.shape, F32),
                jnp.zeros(acc_ref.shape, F32))
        m_ref[...], l_ref[...], acc_ref[...] = visit([kpad_ref[...]], [vpad_ref[...]], [bias], init)

    biases = []
    for i in range(pages_per_step):
        kpos = (npages - 1 - (j * pages_per_step + i)) * page + s
        biases.append(-slope * (past_len + t - kpos).astype(F32))
    m_ref[...], l_ref[...], acc_ref[...] = visit([r[0] for r in k_refs], [r[0] for r in v_refs], biases,
                                                 (m_ref[...], l_ref[...], acc_ref[...]))

    @pl.when(j == pl.num_programs(1) - 1)
    def _():
        lam = _diff_lambda(lqk_ref, lam_init)
        o3 = (acc_ref[...] / l_ref[...]).reshape(nh, 2 * tlen, acc_ref.shape[1])
        for h in range(nh):
            o = o3[h, :tlen] - lam * o3[h, tlen:]
            o_ref[0, h] = _rms_rows(o, sub_ref[...]) * (1.0 - lam_init)


def _diff_sample(page_table, q4, knt, vn4, cache_kt, cache_v, slopes, lqk, subln, lam_init, pages_per_step):
    bsz, nmap, tlen, hd = q4.shape
    npages = page_table.shape[1]
    page = cache_kt.shape[3]
    nheads, dv = cache_v.shape[2], cache_v.shape[3]
    rows = nmap * tlen
    slope_rows = jnp.broadcast_to(jnp.repeat(slopes, rows // nheads)[:, None], (rows, page))
    seq = lambda a: pl.BlockSpec((1,) + a.shape[1:], lambda b, j, pt: (b, 0, 0, 0))
    const2 = lambda shape: pl.BlockSpec(shape, lambda b, j, pt: (0, 0))
    grid_spec = pltpu.PrefetchScalarGridSpec(
        num_scalar_prefetch=1,
        grid=(bsz, npages // pages_per_step),
        in_specs=([seq(q4), seq(knt), seq(vn4), const2((rows, page)), const2(lqk.shape), const2((1, dv))]
                  + _page_specs(page_table, pages_per_step, (1, nmap, hd, page))
                  + _page_specs(page_table, pages_per_step, (1, page, nheads, dv))),
        out_specs=pl.BlockSpec((1, nheads, tlen, dv), lambda b, j, pt: (b, 0, 0, 0)),
        scratch_shapes=[pltpu.VMEM((nmap, hd, page), F32), pltpu.VMEM((page, nheads, dv), F32),
                        pltpu.VMEM((rows, dv), F32),
                        pltpu.VMEM((rows, 1), F32), pltpu.VMEM((rows, 1), F32)])
    return pl.pallas_call(
        functools.partial(_diff_sample_kernel, pages_per_step=pages_per_step, scale=HEAD64 ** -0.5 * LOG2E,
                          lam_init=lam_init, past_len=npages * page),
        grid_spec=grid_spec,
        out_shape=jax.ShapeDtypeStruct((bsz, nheads, tlen, dv), F32),
        compiler_params=_cp(("parallel", "arbitrary")),
        name="diff_sample",
    )(page_table, q4, knt, vn4, slope_rows, lqk, subln.astype(F32).reshape(1, dv),
      *([cache_kt] * pages_per_step), *([cache_v] * pages_per_step))


def _largest_tile(m, cap, mult=16):
    t = min(m, cap)
    while m % t or t % mult:
        t -= mult
    return t


def _layer0(x, bsz, tlen, n_valid, s5_re0, s5_im0, rwkv_s0, shift_prev, wt):
    m = x.shape[0]
    tm = _largest_tile(m, 640)
    u = _proj(x, wt['norm_mix0'], wt['w_in0_u'], tm, 512)
    p = _proj(x, wt['norm_mix0'], wt['w_in0_p'], tm, 1152)

    bu_re, bu_im = _s5_b(u, wt['s5_wb_re'], wt['s5_wb_im'], tm)
    width = bu_re.shape[1]
    nb_s5, tb_s5 = (1, LANE) if bsz == 1 else (_largest_tile(bsz, 16, 1), tlen)
    h_re, h_im, f_re, f_im = _s5_scan(bu_re.reshape(bsz, tlen, width), bu_im.reshape(bsz, tlen, width),
                                      s5_re0, s5_im0, wt['s5_lb_re'], wt['s5_lb_im'], nb_s5, tb_s5)
    if n_valid < tlen:
        f_re, f_im = h_re[:, n_valid - 1], h_im[:, n_valid - 1]
    ya = _s5_c(h_re.reshape(m, width), h_im.reshape(m, width), u, wt['s5_wc_re'], wt['s5_wc_im'],
               wt['s5_d'], wt['s5_w_glu'], _largest_tile(m, 256))

    c = p.shape[1]
    if bsz == 1:
        tr = _largest_tile(tlen, 320)
        prev = jnp.concatenate([shift_prev.astype(F32), p[tr - 1::tr][:-1]])[:, None, :]
        p3, nb = p.reshape(tlen // tr, tr, c), 1
    else:
        prev = shift_prev.astype(F32)[:, None, :]
        p3, nb = p.reshape(bsz, tlen, c), _largest_tile(bsz, 32, 1)
    r, lw, k, v, kk, kka, g = _rwkv_pre(p3, prev, wt['rwkv_mu'], wt['rwkv_w0'], wt['rwkv_w2'],
                                        wt['rwkv_a0'], wt['rwkv_a2'], wt['rwkv_g2'], wt['rwkv_k_k'],
                                        wt['rwkv_k_a'], nb, m if bsz > 1 else n_valid)
    w_r = r.shape[1]
    if bsz == 1:
        v3 = lambda a: a.reshape(bsz, tlen, w_r)
        y, s_last = _rwkv_scan(v3(r), v3(lw), v3(k), v3(v), v3(kk), v3(kka), rwkv_s0,
                               _largest_tile(tlen, 640, LANE), HEAD64)
        y = y.reshape(m, w_r)
    else:
        lanes_b = lambda a: jnp.transpose(a.reshape(bsz, tlen, w_r // HEAD64, HEAD64), (1, 2, 3, 0))
        y, s_last = _rwkv_steps(lanes_b(r), lanes_b(lw), lanes_b(k), lanes_b(v), lanes_b(kk), lanes_b(kka),
                                jnp.transpose(rwkv_s0.astype(F32), (1, 2, 3, 0)))
        y = jnp.transpose(y, (3, 0, 1, 2)).reshape(m, w_r)
        s_last = jnp.transpose(s_last, (3, 0, 1, 2))
    yb = _rwkv_post(y, r, k, v, g, wt['rwkv_ln_w'], wt['rwkv_ln_b'], wt['rwkv_r_k'], _largest_tile(m, 512))

    x = _outproj(x, ya, yb, wt['w_out0_a'], wt['w_out0_b'], tm, 512)
    x = _ffn(x, wt['norm_ffn0'], wt['ffn_w1'], wt['ffn_w3'], wt['ffn_w2'], tm, 512)
    return x, f_re, f_im, s_last, p


def kernel(x_prompt, x_sample, state_s5_re, state_s5_im, state_rwkv, state_rwkv_shift, cache_sb_k, cache_sb_v, cache_diff_k, cache_diff_v, page_table, meta_tokens, norm_mix0, w_in0, s5_lambda_re, s5_lambda_im, s5_log_dt, s5_b_re, s5_b_im, s5_c_re, s5_c_im, s5_d, s5_w_glu, rwkv_mu, rwkv_w0, rwkv_w2, rwkv_a0, rwkv_a2, rwkv_g2, rwkv_k_k, rwkv_k_a, rwkv_r_k, rwkv_ln_w, rwkv_ln_b, w_out0, norm_ffn0, ffn_w1, ffn_w3, ffn_w2, norm_mix1, w_in1, diff_q_norm, diff_k_norm, diff_lambda_q1, diff_lambda_k1, diff_lambda_q2, diff_lambda_k2, diff_subln, w_out1, norm_ffn1, moe_router, moe_w1, moe_w3, moe_w2):
    d_model = x_prompt.shape[-1]
    n_groups, n_state, grp = s5_b_re.shape
    s5_width = n_groups * grp
    rwkv_width = rwkv_w0.shape[0]
    rwkv_heads = rwkv_width // HEAD64
    sb_width = cache_sb_k.shape[2] * cache_sb_k.shape[3]
    diff_heads = cache_diff_v.shape[2]
    diff_qk_width = cache_diff_k.shape[2] * cache_diff_k.shape[3] * cache_diff_k.shape[4]
    diff_v_width = diff_heads * cache_diff_v.shape[3]
    lam_init = 0.8 - 0.6 * math.exp(-0.3)
    bf = lambda a: a.astype(BF16)

    lb_re, lb_im, bb_re, bb_im = _s5_discretize(s5_lambda_re, s5_lambda_im, s5_log_dt, s5_b_re, s5_b_im)
    to_ghp = lambda a: jnp.transpose(a, (1, 0, 2))
    wt = dict(
        norm_mix0=norm_mix0.astype(F32), w_in0_u=bf(w_in0[:, :s5_width]), w_in0_p=bf(w_in0[:, s5_width:]),
        s5_lb_re=lb_re, s5_lb_im=lb_im,
        s5_wb_re=bf(_blockdiag(to_ghp(bb_re), 8)), s5_wb_im=bf(_blockdiag(to_ghp(bb_im), 8)),
        s5_wc_re=bf(_blockdiag(jnp.transpose(s5_c_re.astype(F32), (0, 2, 1)), 16)),
        s5_wc_im=bf(_blockdiag(jnp.transpose(s5_c_im.astype(F32), (0, 2, 1)), 16)),
        s5_d=s5_d.reshape(-1), s5_w_glu=bf(s5_w_glu),
        rwkv_mu=rwkv_mu, rwkv_w0=rwkv_w0, rwkv_w2=bf(rwkv_w2), rwkv_a0=rwkv_a0, rwkv_a2=bf(rwkv_a2),
        rwkv_g2=bf(rwkv_g2), rwkv_k_k=rwkv_k_k, rwkv_k_a=rwkv_k_a, rwkv_r_k=rwkv_r_k,
        rwkv_ln_w=rwkv_ln_w, rwkv_ln_b=rwkv_ln_b,
        w_out0_a=bf(w_out0[:s5_width]), w_out0_b=bf(w_out0[s5_width:]),
        norm_ffn0=norm_ffn0.astype(F32), ffn_w1=bf(ffn_w1), ffn_w3=bf(ffn_w3), ffn_w2=bf(ffn_w2),
    )
    w_in1_b = bf(w_in1)
    w_out1_a, w_out1_b = bf(w_out1[:sb_width]), bf(w_out1[sb_width:])
    moe_w1_b, moe_w3_b, moe_w2_b = bf(moe_w1), bf(moe_w3), bf(moe_w2)
    lqk = jnp.stack([diff_lambda_q1, diff_lambda_k1, diff_lambda_q2, diff_lambda_k2]).astype(F32)
    slopes = _diff_slopes(diff_heads)
    q_col = 3 * sb_width // diff_qk_width
    npair = sb_width // LANE

    bp, seq, _ = x_prompt.shape
    assert bp == 1
    real = N_META + seq
    lp = real + (-real) % LANE
    xp = jnp.concatenate([meta_tokens.astype(F32), x_prompt[0].astype(F32), jnp.zeros((lp - real, d_model), F32)])
    tr_p = _largest_tile(lp, 640)
    tm_p = tr_p
    c_proj = w_in0.shape[1] - s5_width
    z_s5 = jnp.zeros((1, n_groups, n_state), F32)
    z_rwkv = jnp.zeros((1, rwkv_heads, HEAD64, HEAD64), F32)
    xh, p_s5_re, p_s5_im, p_rwkv, p_proj = _layer0(xp, 1, lp, real, z_s5, z_s5, z_rwkv,
                                                    jnp.zeros((1, c_proj), F32), wt)
    p_rwkv_shift = p_proj[real - 1:real]

    proj1 = _proj(xh, norm_mix1.astype(F32), w_in1_b, tm_p, 512, split=True)
    blk_dq = 3 * npair
    blk_dk = blk_dq + diff_heads
    blk_dv = blk_dk + diff_heads
    dqn = _qknorm_blocks(proj1, blk_dq, diff_heads, diff_q_norm, tr_p)
    dkn = _qknorm_blocks(proj1, blk_dk, diff_heads, diff_k_norm, tr_p)
    tq = _largest_tile(lp, 640, LANE)
    y_sb = _sb_prompt(proj1, lp, LANE, tq)
    y_d = _diff_prompt(dqn, dkn, proj1, blk_dv, slopes, lqk, diff_subln, lp, LANE, tq, lam_init)
    xh = _outproj(xh, y_sb, y_d, w_out1_a, w_out1_b, tm_p, 512)
    xh = _moe(xh, norm_ffn1.astype(F32), moe_router, moe_w1_b, moe_w3_b, moe_w2_b, tm_p, 256)
    y_prompt = xh[N_META:real][None]

    def tokens(blocks):
        return jnp.transpose(blocks[:, :real], (1, 0, 2)).reshape(1, real, -1)

    p_sb_k = tokens(proj1[npair:2 * npair]).reshape(1, real, sb_width // HEAD64, HEAD64)
    p_sb_v = tokens(proj1[2 * npair:3 * npair]).reshape(1, real, sb_width // HEAD64, HEAD64)
    p_diff_k = tokens(dkn).reshape(1, real, diff_heads, 2, HEAD64)
    p_diff_v = tokens(proj1[blk_dv:blk_dv + diff_heads]).reshape(1, real, diff_heads, diff_v_width // diff_heads)

    db, dseq, _ = x_sample.shape
    ms = db * dseq
    xs = x_sample.astype(F32).reshape(ms, d_model)
    xs, s_s5_re, s_s5_im, s_rwkv, s_proj = _layer0(xs, db, dseq, dseq, state_s5_re, state_s5_im, state_rwkv,
                                                   state_rwkv_shift, wt)
    s_rwkv_shift = s_proj.reshape(db, dseq, c_proj)[:, -1]

    tm_s = _largest_tile(ms, 640)
    proj_s = _proj(xs, norm_mix1.astype(F32), w_in1_b, tm_s, 512)
    dqn_s = _qknorm_cols(proj_s, q_col, diff_qk_width, diff_q_norm, tm_s)
    dkn_s = _qknorm_cols(proj_s, q_col + 1, diff_qk_width, diff_k_norm, tm_s)
    n_pool, page = cache_sb_k.shape[0], cache_sb_k.shape[1]
    pt = page_table.astype(jnp.int32)
    pages_per_step = next(n for n in (8, 4, 2, 1) if pt.shape[1] % n == 0)
    v_col = (3 * sb_width + 2 * diff_qk_width) // diff_v_width
    heads = lambda a: a.reshape(db, dseq, -1, HEAD64)
    q_first = lambda a: jnp.transpose(heads(a), (0, 2, 1, 3))
    t_last = lambda a: jnp.transpose(heads(a), (0, 2, 3, 1))
    slot_last = lambda c: jnp.transpose(c.reshape(n_pool, page, -1, HEAD64), (0, 2, 3, 1))
    ys_sb = _sb_sample(pt, q_first(proj_s[:, :sb_width]), t_last(proj_s[:, sb_width:2 * sb_width]),
                       t_last(proj_s[:, 2 * sb_width:3 * sb_width]), slot_last(cache_sb_k),
                       slot_last(cache_sb_v), pages_per_step)
    ys_d = _diff_sample(pt, q_first(dqn_s), t_last(dkn_s),
                        proj_s[:, v_col * diff_v_width:].reshape(db, dseq, diff_heads, -1),
                        slot_last(cache_diff_k), cache_diff_v, slopes, lqk, diff_subln, lam_init,
                        pages_per_step)
    tokens_s = lambda a: jnp.transpose(a, (0, 2, 1, 3)).reshape(ms, -1)
    xs = _outproj(xs, tokens_s(ys_sb), tokens_s(ys_d), w_out1_a, w_out1_b, tm_s, 512)
    xs = _moe(xs, norm_ffn1.astype(F32), moe_router, moe_w1_b, moe_w3_b, moe_w2_b, tm_s, 256)
    y_sample = xs.reshape(db, dseq, d_model)

    s_sb_k = proj_s[:, sb_width:2 * sb_width].reshape(db, dseq, sb_width // HEAD64, HEAD64)
    s_sb_v = proj_s[:, 2 * sb_width:3 * sb_width].reshape(db, dseq, sb_width // HEAD64, HEAD64)
    s_diff_k = dkn_s.reshape(db, dseq, diff_heads, 2, HEAD64)
    s_diff_v = proj_s[:, v_col * diff_v_width:].reshape(db, dseq, diff_heads, diff_v_width // diff_heads)

    return (y_prompt, y_sample,
            p_s5_re.reshape(1, n_groups, n_state), p_s5_im.reshape(1, n_groups, n_state), p_rwkv, p_rwkv_shift,
            p_sb_k, p_sb_v, p_diff_k, p_diff_v,
            s_s5_re.reshape(db, n_groups, n_state), s_s5_im.reshape(db, n_groups, n_state), s_rwkv, s_rwkv_shift,
            s_sb_k, s_sb_v, s_diff_k, s_diff_v)
```

```python
import functools
import math

import jax
import jax.numpy as jnp
from jax import lax
from jax.experimental import pallas as pl
from jax.experimental.pallas import tpu as pltpu

F32 = jnp.float32
BF16 = jnp.bfloat16
HIGHEST = lax.Precision.HIGHEST

LANE = 128
HEAD64 = 64
RMS_EPS = 1e-6
RWKV_LN_EPS = 64e-5
N_META = 16
NEG_INF = -1e30
LOG2E = math.log2(math.e)
VMEM_LIMIT = 56 * 1024 * 1024


def _cp(sem, vmem=VMEM_LIMIT):
    return pltpu.CompilerParams(dimension_semantics=sem, vmem_limit_bytes=vmem)


def _dot(a, b, precision=None):
    return jnp.dot(a, b, preferred_element_type=F32, precision=precision)


def _dot_nt(a, b, precision=None):
    return lax.dot_general(a, b, (((1,), (1,)), ((), ())), preferred_element_type=F32,
                           precision=precision)


def _dot_tn(a, b, precision=None):
    return lax.dot_general(a, b, (((0,), (0,)), ((), ())), preferred_element_type=F32,
                           precision=precision)


def _split_bf16(x):
    hi = x.astype(BF16)
    lo = (x - hi.astype(F32)).astype(BF16)
    return hi, lo


_NN = ((1,), (0,))
_NT = ((1,), (1,))
_TN = ((0,), (0,))


def _mm3(a, b, dims):
    dg = lambda x, y: lax.dot_general(x, y, (dims, ((), ())), preferred_element_type=F32)
    return dg(a[0], b[0]) + dg(a[0], b[1]) + dg(a[1], b[0])


def _rms_rows(x, gain):
    ms = jnp.mean(x * x, axis=-1, keepdims=True)
    return x * lax.rsqrt(ms + RMS_EPS) * gain


def _softplus(t):
    return jnp.maximum(t, 0.0) + jnp.log1p(jnp.exp(-jnp.abs(t)))


def _segsum64(x, e128):
    cols = [_dot(x[:, c * LANE:(c + 1) * LANE], e128, HIGHEST) for c in range(x.shape[1] // LANE)]
    return cols[0] if len(cols) == 1 else jnp.concatenate(cols, axis=1)


def _seg_ones():
    r = lax.broadcasted_iota(jnp.int32, (LANE, LANE), 0) // HEAD64
    c = lax.broadcasted_iota(jnp.int32, (LANE, LANE), 1) // HEAD64
    return (r == c).astype(F32)


def _proj_kernel(x_ref, g_ref, w_ref, o_ref, xn_ref, *, split):
    @pl.when(pl.program_id(1) == 0)
    def _():
        xn_ref[...] = _rms_rows(x_ref[...], g_ref[...]).astype(BF16)

    acc = _dot(xn_ref[...], w_ref[...])
    if split:
        for c in range(acc.shape[1] // LANE):
            o_ref[c] = acc[:, c * LANE:(c + 1) * LANE]
    else:
        o_ref[...] = acc


def _proj(x, gain, w, tm, tn, split=False):
    m, d = x.shape
    n = w.shape[1]
    if split:
        out_shape = jax.ShapeDtypeStruct((n // LANE, m, LANE), F32)
        out_spec = pl.BlockSpec((tn // LANE, tm, LANE), lambda i, j: (j, i, 0))
    else:
        out_shape = jax.ShapeDtypeStruct((m, n), F32)
        out_spec = pl.BlockSpec((tm, tn), lambda i, j: (i, j))
    return pl.pallas_call(
        functools.partial(_proj_kernel, split=split),
        grid=(m // tm, n // tn),
        in_specs=[pl.BlockSpec((tm, d), lambda i, j: (i, 0)),
                  pl.BlockSpec((1, d), lambda i, j: (0, 0)),
                  pl.BlockSpec((d, tn), lambda i, j: (0, j))],
        out_specs=out_spec,
        out_shape=out_shape,
        scratch_shapes=[pltpu.VMEM((tm, d), BF16)],
        compiler_params=_cp(("parallel", "arbitrary")),
        name="proj",
    )(x, gain.reshape(1, d), w)


def _outproj_kernel(res_ref, a_ref, b_ref, wa_ref, wb_ref, o_ref):
    o_ref[...] = (res_ref[...] + _dot(a_ref[...].astype(BF16), wa_ref[...])
                  + _dot(b_ref[...].astype(BF16), wb_ref[...]))


def _outproj(res, a, b, wa, wb, tm, tn):
    m, n = res.shape
    ka, kb = a.shape[1], b.shape[1]
    return pl.pallas_call(
        _outproj_kernel,
        grid=(m // tm, n // tn),
        in_specs=[pl.BlockSpec((tm, tn), lambda i, j: (i, j)),
                  pl.BlockSpec((tm, ka), lambda i, j: (i, 0)),
                  pl.BlockSpec((tm, kb), lambda i, j: (i, 0)),
                  pl.BlockSpec((ka, tn), lambda i, j: (0, j)),
                  pl.BlockSpec((kb, tn), lambda i, j: (0, j))],
        out_specs=pl.BlockSpec((tm, tn), lambda i, j: (i, j)),
        out_shape=jax.ShapeDtypeStruct((m, n), F32),
        compiler_params=_cp(("parallel", "arbitrary")),
        name="outproj",
    )(res, a, b, wa, wb)


def _ffn_kernel(x_ref, g_ref, w1_ref, w3_ref, w2_ref, o_ref, xn_ref):
    @pl.when(pl.program_id(1) == 0)
    def _():
        x = x_ref[...]
        xn_ref[...] = _rms_rows(x, g_ref[...]).astype(BF16)
        o_ref[...] = x

    xn = xn_ref[...]
    h1 = _dot(xn, w1_ref[...])
    h3 = _dot(xn, w3_ref[...])
    h = (h1 * jax.nn.sigmoid(h1) * h3).astype(BF16)
    o_ref[...] += _dot(h, w2_ref[...])


def _ffn(x, gain, w1, w3, w2, tm, tf):
    m, d = x.shape
    f = w1.shape[1]
    return pl.pallas_call(
        _ffn_kernel,
        grid=(m // tm, f // tf),
        in_specs=[pl.BlockSpec((tm, d), lambda i, j: (i, 0)),
                  pl.BlockSpec((1, d), lambda i, j: (0, 0)),
                  pl.BlockSpec((d, tf), lambda i, j: (0, j)),
                  pl.BlockSpec((d, tf), lambda i, j: (0, j)),
                  pl.BlockSpec((tf, d), lambda i, j: (j, 0))],
        out_specs=pl.BlockSpec((tm, d), lambda i, j: (i, 0)),
        out_shape=jax.ShapeDtypeStruct((m, d), F32),
        scratch_shapes=[pltpu.VMEM((tm, d), BF16)],
        compiler_params=_cp(("parallel", "arbitrary")),
        name="ffn",
    )(x, gain.reshape(1, d), w1, w3, w2)


def _moe_route_kernel(x_ref, g_ref, r_ref, xn_ref, gate_ref, *, n_experts):
    tm = x_ref.shape[0]
    lane = lax.broadcasted_iota(jnp.int32, (tm, LANE), 1)
    xn = _rms_rows(x_ref[...], g_ref[...])
    xn_ref[...] = xn.astype(BF16)
    logits = jnp.where(lane < n_experts, _dot(xn, r_ref[...], HIGHEST), NEG_INF)
    v1 = jnp.max(logits, axis=-1, keepdims=True)
    i1 = jnp.min(jnp.where(logits == v1, lane, LANE), axis=-1, keepdims=True)
    rest = jnp.where(lane == i1, NEG_INF, logits)
    v2 = jnp.max(rest, axis=-1, keepdims=True)
    i2 = jnp.min(jnp.where(rest == v2, lane, LANE), axis=-1, keepdims=True)
    ex = jnp.exp(v2 - v1)
    den = 1.0 + ex
    gate_ref[...] = jnp.where(lane == i1, 1.0 / den, 0.0) + jnp.where(lane == i2, ex / den, 0.0)


def _moe_route(x, gain, router, tm):
    m, d = x.shape
    ne = router.shape[1]
    router_p = jnp.zeros((d, LANE), F32).at[:, :ne].set(router.astype(F32))
    return pl.pallas_call(
        functools.partial(_moe_route_kernel, n_experts=ne),
        grid=(m // tm,),
        in_specs=[pl.BlockSpec((tm, d), lambda i: (i, 0)),
                  pl.BlockSpec((1, d), lambda i: (0, 0)),
                  pl.BlockSpec((d, LANE), lambda i: (0, 0))],
        out_specs=(pl.BlockSpec((tm, d), lambda i: (i, 0)), pl.BlockSpec((tm, LANE), lambda i: (i, 0))),
        out_shape=(jax.ShapeDtypeStruct((m, d), BF16), jax.ShapeDtypeStruct((m, LANE), F32)),
        compiler_params=_cp(("parallel",)),
        name="moe_route",
    )(x, gain.reshape(1, d), router_p)


def _moe_kernel(cnt_ref, x_ref, xn_ref, gate_ref, gt_ref, w1_ref, w3_ref, w2_ref, o_ref,
                sel_ref, xa_ref, acc_ref, *, chunk):
    i = pl.program_id(0)
    e = pl.program_id(1)
    j = pl.program_id(2)
    tm = x_ref.shape[0]
    cnt = cnt_ref[i, e]
    chunks = [(k, slice(k * chunk, (k + 1) * chunk)) for k in range(sel_ref.shape[0] // chunk)]

    @pl.when((e == 0) & (j == 0))
    def _():
        o_ref[...] = x_ref[...]

    @pl.when((j == 0) & (cnt > 0))
    def _():
        routed = (gt_ref[pl.ds(e, 1), :] > 0.0).astype(F32)
        before = (lax.broadcasted_iota(jnp.int32, (tm, tm), 0)
                  < lax.broadcasted_iota(jnp.int32, (tm, tm), 1)).astype(BF16)
        rank = _dot(jnp.broadcast_to(routed, (8, tm)).astype(BF16), before)[0:1]
        slot = lax.broadcasted_iota(jnp.int32, (chunk, tm), 0).astype(F32)
        for k, rows in chunks:
            @pl.when(k * chunk < cnt)
            def _():
                sel = jnp.where(rank == slot + float(k * chunk), routed, 0.0).astype(BF16)
                sel_ref[rows] = sel
                xa_ref[rows] = _dot(sel, xn_ref[...]).astype(BF16)
                acc_ref[rows] = jnp.zeros((chunk, acc_ref.shape[1]), F32)

    for k, rows in chunks:
        @pl.when(k * chunk < cnt)
        def _():
            xa = xa_ref[rows]
            h1 = _dot(xa, w1_ref[0])
            h3 = _dot(xa, w3_ref[0])
            h = (h1 * jax.nn.sigmoid(h1) * h3).astype(BF16)
            acc_ref[rows] += _dot(h, w2_ref[0])

    @pl.when((j == pl.num_programs(2) - 1) & (cnt > 0))
    def _():
        lane = lax.broadcasted_iota(jnp.int32, (tm, LANE), 1)
        ge = jnp.sum(jnp.where(lane == e, gate_ref[...], 0.0), axis=-1, keepdims=True)
        for k, rows in chunks:
            @pl.when(k * chunk < cnt)
            def _():
                hi, lo = _split_bf16(acc_ref[rows])
                sel = sel_ref[rows]
                o_ref[...] += ge * (_dot_tn(sel, hi) + _dot_tn(sel, lo))


def _moe(x, gain, router, w1, w3, w2, tm, tf, chunk=192):
    m, d = x.shape
    ne, _, f = w1.shape
    xn, gate = _moe_route(x, gain, router, tm)
    counts = jnp.sum((gate[:, :ne] > 0.0).reshape(m // tm, tm, ne), axis=1).astype(jnp.int32)
    cap = -(-tm // chunk) * chunk
    grid_spec = pltpu.PrefetchScalarGridSpec(
        num_scalar_prefetch=1,
        grid=(m // tm, ne, f // tf),
        in_specs=[pl.BlockSpec((tm, d), lambda i, e, j, c: (i, 0)),
                  pl.BlockSpec((tm, d), lambda i, e, j, c: (i, 0)),
                  pl.BlockSpec((tm, LANE), lambda i, e, j, c: (i, 0)),
                  pl.BlockSpec((LANE, tm), lambda i, e, j, c: (0, i)),
                  pl.BlockSpec((1, d, tf), lambda i, e, j, c: (e, 0, j)),
                  pl.BlockSpec((1, d, tf), lambda i, e, j, c: (e, 0, j)),
                  pl.BlockSpec((1, tf, d), lambda i, e, j, c: (e, j, 0))],
        out_specs=pl.BlockSpec((tm, d), lambda i, e, j, c: (i, 0)),
        scratch_shapes=[pltpu.VMEM((cap, tm), BF16), pltpu.VMEM((cap, d), BF16), pltpu.VMEM((cap, d), F32)])
    return pl.pallas_call(
        functools.partial(_moe_kernel, chunk=chunk),
        grid_spec=grid_spec,
        out_shape=jax.ShapeDtypeStruct((m, d), F32),
        compiler_params=_cp(("parallel", "arbitrary", "arbitrary")),
        name="moe",
    )(counts, x, xn, gate, gate.T, w1, w3, w2)


def _s5_disc_kernel(lr_ref, li_ref, ldt_ref, bre_ref, bim_ref, lbre_ref, lbim_ref, bbre_ref, bbim_ref):
    lr = lr_ref[...]
    li = li_ref[...]
    dt = jnp.exp(ldt_ref[...])
    mag = jnp.exp(lr * dt)
    lb_re = mag * jnp.cos(li * dt)
    lb_im = mag * jnp.sin(li * dt)
    den = lr * lr + li * li
    f_re = ((lb_re - 1.0) * lr + lb_im * li) / den
    f_im = (lb_im * lr - (lb_re - 1.0) * li) / den
    lbre_ref[...] = lb_re
    lbim_ref[...] = lb_im
    for h in range(bre_ref.shape[0]):
        br = bre_ref[h]
        bi = bim_ref[h]
        bbre_ref[h] = f_re * br - f_im * bi
        bbim_ref[h] = f_re * bi + f_im * br


def _s5_discretize(lam_re, lam_im, log_dt, b_re, b_im):
    g, p, h = b_re.shape
    sds = jax.ShapeDtypeStruct
    return pl.pallas_call(
        _s5_disc_kernel,
        out_shape=(sds((g, p), F32), sds((g, p), F32), sds((h, g, p), F32), sds((h, g, p), F32)),
        name="s5_disc",
    )(lam_re.astype(F32), lam_im.astype(F32), log_dt.astype(F32).reshape(g, 1),
      jnp.transpose(b_re.astype(F32), (2, 0, 1)), jnp.transpose(b_im.astype(F32), (2, 0, 1)))


def _s5_b_kernel(u_ref, wre_ref, wim_ref, ore_ref, oim_ref):
    u = u_ref[...].astype(BF16)
    ore_ref[...] = _dot(u, wre_ref[0])
    oim_ref[...] = _dot(u, wim_ref[0])


def _s5_b(u, wre, wim, tm):
    m = u.shape[0]
    nc, kc, ncol = wre.shape
    spec_o = pl.BlockSpec((tm, ncol), lambda i, c: (i, c))
    spec_w = pl.BlockSpec((1, kc, ncol), lambda i, c: (c, 0, 0))
    sds = jax.ShapeDtypeStruct((m, nc * ncol), F32)
    return pl.pallas_call(
        _s5_b_kernel,
        grid=(m // tm, nc),
        in_specs=[pl.BlockSpec((tm, kc), lambda i, c: (i, c)), spec_w, spec_w],
        out_specs=(spec_o, spec_o),
        out_shape=(sds, sds),
        compiler_params=_cp(("parallel", "arbitrary")),
        name="s5_b",
    )(u, wre, wim)


S5_STEPS = 8


def _cmul(ar, ai, br, bi):
    return ar * br - ai * bi, ar * bi + ai * br


def _s5_scan_kernel(bre_ref, bim_ref, h0re_ref, h0im_ref, lre_ref, lim_ref, hre_ref, him_ref,
                    cre_ref, cim_ref, pwre_ref, pwim_ref, *, chained, cols):
    tb, width = bre_ref.shape
    row = lax.broadcasted_iota(jnp.int32, (S5_STEPS, cols), 0)

    pr, pi = lre_ref[...], lim_ref[...]
    pwre_ref[0:1] = pr
    pwim_ref[0:1] = pi
    for t in range(1, S5_STEPS):
        pr, pi = _cmul(pr, pi, lre_ref[...], lim_ref[...])
        pwre_ref[t:t + 1] = pr
        pwim_ref[t:t + 1] = pi

    if chained:
        @pl.when(pl.program_id(0) == 0)
        def _():
            cre_ref[...] = h0re_ref[...]
            cim_ref[...] = h0im_ref[...]

    def group(g, _):
        rows = pl.ds(pl.multiple_of(g * S5_STEPS, S5_STEPS), S5_STEPS)
        for c0 in range(0, width, cols):
            lanes = slice(c0, c0 + cols)
            xr = bre_ref[rows, lanes]
            xi = bim_ref[rows, lanes]
            d = 1
            while d < S5_STEPS:
                sr = jnp.where(row >= d, pltpu.roll(xr, d, axis=0), 0.0)
                si = jnp.where(row >= d, pltpu.roll(xi, d, axis=0), 0.0)
                mr, mi = _cmul(pwre_ref[d - 1:d, lanes], pwim_ref[d - 1:d, lanes], sr, si)
                xr, xi = xr + mr, xi + mi
                d *= 2
            if chained:
                h_r, h_i = cre_ref[:, lanes], cim_ref[:, lanes]
            else:
                h_r, h_i = h0re_ref[pl.ds(g, 1), lanes], h0im_ref[pl.ds(g, 1), lanes]
            er, ei = _cmul(pwre_ref[:, lanes], pwim_ref[:, lanes], h_r, h_i)
            xr, xi = xr + er, xi + ei
            hre_ref[rows, lanes] = xr
            him_ref[rows, lanes] = xi
            if chained:
                cre_ref[:, lanes] = xr[S5_STEPS - 1:]
                cim_ref[:, lanes] = xi[S5_STEPS - 1:]
        return 0

    lax.fori_loop(0, tb // S5_STEPS, group, 0)


def _s5_scan(bu_re, bu_im, h0_re, h0_im, lb_re, lb_im, tb, chained):
    m, width = bu_re.shape
    spec_x = pl.BlockSpec((tb, width), lambda i: (i, 0))
    h_rows = 1 if chained else tb // S5_STEPS
    spec_h = pl.BlockSpec((h_rows, width), (lambda i: (0, 0)) if chained else (lambda i: (i, 0)))
    spec_l = pl.BlockSpec((1, width), lambda i: (0, 0))
    sds = jax.ShapeDtypeStruct((m, width), F32)
    row = lambda a: a.astype(F32).reshape(-1, width)
    return pl.pallas_call(
        functools.partial(_s5_scan_kernel, chained=chained, cols=1024),
        grid=(m // tb,),
        in_specs=[spec_x, spec_x, spec_h, spec_h, spec_l, spec_l],
        out_specs=(spec_x, spec_x),
        out_shape=(sds, sds),
        scratch_shapes=[pltpu.VMEM((1, width), F32), pltpu.VMEM((1, width), F32),
                        pltpu.VMEM((S5_STEPS, width), F32), pltpu.VMEM((S5_STEPS, width), F32)],
        compiler_params=_cp(("arbitrary",)),
        name="s5_scan",
    )(bu_re, bu_im, row(h0_re), row(h0_im), row(lb_re), row(lb_im))


def _s5_c_kernel(hre_ref, him_ref, u_ref, wcre_ref, wcim_ref, d_ref, wglu_ref, o_ref):
    nc, kc, _ = wcre_ref.shape
    cols = []
    for c in range(nc):
        hr = hre_ref[:, c * kc:(c + 1) * kc].astype(BF16)
        hi = him_ref[:, c * kc:(c + 1) * kc].astype(BF16)
        cols.append(_dot(hr, wcre_ref[c]) - _dot(hi, wcim_ref[c]))
    y = jnp.concatenate(cols, axis=1) + d_ref[...] * u_ref[...]
    z = jax.nn.gelu(y)
    o_ref[...] = z * jax.nn.sigmoid(_dot(z.astype(BF16), wglu_ref[...]))


def _s5_c(h_re, h_im, u, wcre, wcim, d_skip, w_glu, tm):
    m, width = h_re.shape
    w = u.shape[1]
    full3 = lambda a: pl.BlockSpec(a.shape, lambda i: (0, 0, 0))
    return pl.pallas_call(
        _s5_c_kernel,
        grid=(m // tm,),
        in_specs=[pl.BlockSpec((tm, width), lambda i: (i, 0)),
                  pl.BlockSpec((tm, width), lambda i: (i, 0)),
                  pl.BlockSpec((tm, w), lambda i: (i, 0)),
                  full3(wcre), full3(wcim),
                  pl.BlockSpec((1, w), lambda i: (0, 0)),
                  pl.BlockSpec((w, w), lambda i: (0, 0))],
        out_specs=pl.BlockSpec((tm, w), lambda i: (i, 0)),
        out_shape=jax.ShapeDtypeStruct((m, w), F32),
        compiler_params=_cp(("parallel",)),
        name="s5_c",
    )(h_re, h_im, u, wcre, wcim, d_skip.astype(F32).reshape(1, w), w_glu)


def _blockdiag(w, nblk):
    g, a, b = w.shape
    w4 = w.reshape(g // nblk, nblk, a, b)
    eye = jnp.eye(nblk, dtype=w.dtype)
    return jnp.einsum('cgab,gh->cgahb', w4, eye).reshape(g // nblk, nblk * a, nblk * b)


def _rwkv_pre_kernel(p_ref, prev_ref, mu_ref, w0_ref, w2_ref, a0_ref, a2_ref, g2_ref, kk_ref, ka_ref,
                     r_o, lw_o, k_o, v_o, kk_o, kka_o, g_o, *, width, lora_w, lora_a, n_valid):
    nb, tlen, c = p_ref.shape
    p = p_ref[...]
    tidx = lax.broadcasted_iota(jnp.int32, (nb, tlen, c), 1)
    prev = jnp.where(tidx == 0, prev_ref[...], pltpu.roll(p, 1, axis=1))
    xm = (p + (prev - p) * mu_ref[...]).reshape(nb * tlen, c)
    r = xm[:, :width]
    k = xm[:, width:2 * width]
    v = xm[:, 2 * width:3 * width]
    o = 3 * width
    w_lo = xm[:, o:o + lora_w]
    a_lo = xm[:, o + lora_w:o + lora_w + lora_a]
    g_lo = xm[:, o + lora_w + lora_a:]
    wraw = -_softplus(-(w0_ref[...] + _dot(jnp.tanh(w_lo).astype(BF16), w2_ref[...]))) - 0.5
    a = jax.nn.sigmoid(a0_ref[...] + _dot(a_lo.astype(BF16), a2_ref[...]))
    g = _dot(jax.nn.sigmoid(g_lo).astype(BF16), g2_ref[...])
    kkr = k * kk_ref[...]
    nrm = jnp.sqrt(_segsum64(kkr * kkr, _seg_ones()))
    kk = kkr / jnp.maximum(nrm, 1e-12)
    row = pl.program_id(0) * (nb * tlen) + lax.broadcasted_iota(jnp.int32, (nb * tlen, 1), 0)
    keep = (row < n_valid).astype(F32)
    r_o[...] = r
    lw_o[...] = -jnp.exp(wraw) * keep
    k_o[...] = k * (1.0 + (a - 1.0) * ka_ref[...]) * keep
    v_o[...] = v * keep
    kk_o[...] = kk * keep
    kka_o[...] = kk * a * keep
    g_o[...] = g


def _rwkv_pre(p3, prev, mu, w0, w2, a0, a2, g2, k_k, k_a, nb, n_valid):
    nbt, tlen, c = p3.shape
    width = w0.shape[0]
    lora_w, lora_a = w2.shape[0], a2.shape[0]
    rows = nb * tlen
    row1 = lambda a: a.astype(F32).reshape(1, -1)
    full2 = lambda a: pl.BlockSpec(a.shape, lambda i: (0, 0))
    spec_o = pl.BlockSpec((rows, width), lambda i: (i, 0))
    sds = jax.ShapeDtypeStruct((nbt * tlen, width), F32)
    args = (mu.astype(F32).reshape(1, 1, c), row1(w0), w2, row1(a0), a2, g2, row1(k_k), row1(k_a))
    return pl.pallas_call(
        functools.partial(_rwkv_pre_kernel, width=width, lora_w=lora_w, lora_a=lora_a, n_valid=n_valid),
        grid=(nbt // nb,),
        in_specs=[pl.BlockSpec((nb, tlen, c), lambda i: (i, 0, 0)),
                  pl.BlockSpec((nb, 1, c), lambda i: (i, 0, 0)),
                  pl.BlockSpec((1, 1, c), lambda i: (0, 0, 0))] + [full2(a) for a in args[1:]],
        out_specs=(spec_o,) * 7,
        out_shape=(sds,) * 7,
        compiler_params=_cp(("parallel",)),
        name="rwkv_pre",
    )(p3, prev, *args)


def _rwkv_scan_kernel(r_ref, lw_ref, k_ref, v_ref, kk_ref, kka_ref, s0_ref, y_ref, sl_ref, s_ref, *, tc):
    tb = r_ref.shape[1]
    npair = r_ref.shape[2] // LANE
    n2 = 2 * tc
    lane = lax.broadcasted_iota(jnp.int32, (tc, LANE), 1)
    first = lane < HEAD64
    ri = lax.broadcasted_iota(jnp.int32, (n2, n2), 0)
    ci = lax.broadcasted_iota(jnp.int32, (n2, n2), 1)
    same = (ri // tc) == (ci // tc)
    strict = same & ((ci % tc) < (ri % tc))
    incl = same & ((ci % tc) <= (ri % tc))
    eye = (ri == ci).astype(F32)
    trow = lax.broadcasted_iota(jnp.int32, (tc, LANE), 0)
    shifts = [1 << b for b in range(max(tc - 1, 0).bit_length())]
    n_sq = max(int(math.ceil(math.log2(tc))) - 1, 0)

    def stack(x):
        return jnp.concatenate([jnp.where(first, x, 0.0), jnp.where(first, 0.0, x)], axis=0)

    @pl.when(pl.program_id(1) == 0)
    def _():
        for pr in range(npair):
            sa = jnp.concatenate([s0_ref[0, 2 * pr], jnp.zeros((HEAD64, HEAD64), F32)], axis=1)
            sb = jnp.concatenate([jnp.zeros((HEAD64, HEAD64), F32), s0_ref[0, 2 * pr + 1]], axis=1)
            s_ref[pr] = jnp.concatenate([sa, sb], axis=0)

    def prepare(rows, pr):
        lanes = slice(pr * LANE, (pr + 1) * LANE)
        lw = lw_ref[0, rows, lanes]
        c = lw
        for sh in shifts:
            c = c + jnp.where(trow >= sh, pltpu.roll(c, sh, axis=0), 0.0)
        c_last = c[tc - 1:tc, :]
        kk = kk_ref[0, rows, lanes]
        kka = kka_ref[0, rows, lanes]
        kx = k_ref[0, rows, lanes]
        e_neg = jnp.exp(-c)
        e_end = jnp.exp(c_last - c)
        ops = (-kk * jnp.exp(c - lw), r_ref[0, rows, lanes] * jnp.exp(c), kka * e_neg, kx * e_neg,
               kka * e_end, kx * e_end, v_ref[0, rows, lanes])
        return tuple(_split_bf16(stack(x)) for x in ops) + (jnp.exp(c_last),)

    def chunk(ci_, _):
        rows = pl.ds(pl.multiple_of(ci_ * tc, tc), tc)
        prs = range(npair)
        al, rt, bt, kt, bh, kh, vs, w_end = zip(*[prepare(rows, pr) for pr in prs])
        nab = [jnp.where(strict, _mm3(al[p], bt[p], _NT), 0.0) for p in prs]
        aak = [jnp.where(strict, _mm3(al[p], kt[p], _NT), 0.0) for p in prs]
        arb = [jnp.where(incl, _mm3(rt[p], bt[p], _NT), 0.0) for p in prs]
        ark = [jnp.where(incl, _mm3(rt[p], kt[p], _NT), 0.0) for p in prs]
        inv = [eye + nab[p] for p in prs]
        pw = nab
        for _ in range(n_sq):
            pw_s = [_split_bf16(x) for x in pw]
            pw = [_mm3(pw_s[p], pw_s[p], _NN) for p in prs]
            inv = [inv[p] + _mm3(_split_bf16(inv[p]), _split_bf16(pw[p]), _NN) for p in prs]
        s0 = [s_ref[p] for p in prs]
        s0_s = [_split_bf16(x) for x in s0]
        rhs = [_mm3(al[p], s0_s[p], _NT) + _mm3(_split_bf16(aak[p]), vs[p], _NN) for p in prs]
        u = [_split_bf16(_mm3(_split_bf16(inv[p]), _split_bf16(rhs[p]), _NN)) for p in prs]
        for p in prs:
            ys = (_mm3(rt[p], s0_s[p], _NT) + _mm3(_split_bf16(arb[p]), u[p], _NN)
                  + _mm3(_split_bf16(ark[p]), vs[p], _NN))
            y_ref[0, rows, p * LANE:(p + 1) * LANE] = ys[:tc] + ys[tc:]
            s_ref[p] = s0[p] * w_end[p] + _mm3(u[p], bh[p], _TN) + _mm3(vs[p], kh[p], _TN)
        return 0

    lax.fori_loop(0, tb // tc, chunk, 0)

    for pr in range(npair):
        s = s_ref[pr]
        sl_ref[0, 2 * pr] = s[:HEAD64, :HEAD64]
        sl_ref[0, 2 * pr + 1] = s[HEAD64:, HEAD64:]


def _rwkv_scan(r, lw, k, v, kk, kka, s0, tb, tc):
    bsz, length, width = r.shape
    spec_x = pl.BlockSpec((1, tb, width), lambda b, t: (b, t, 0))
    spec_s = pl.BlockSpec((1,) + s0.shape[1:], lambda b, t: (b, 0, 0, 0))
    return pl.pallas_call(
        functools.partial(_rwkv_scan_kernel, tc=tc),
        grid=(bsz, length // tb),
        in_specs=[spec_x] * 6 + [spec_s],
        out_specs=(spec_x, spec_s),
        out_shape=(jax.ShapeDtypeStruct((bsz, length, width), F32),
                   jax.ShapeDtypeStruct(s0.shape, F32)),
        scratch_shapes=[pltpu.VMEM((width // LANE, LANE, LANE), F32)],
        compiler_params=_cp(("parallel", "arbitrary")),
        name="rwkv_scan",
    )(r, lw, k, v, kk, kka, s0.astype(F32))


def _rwkv_steps_kernel(r_ref, lw_ref, k_ref, v_ref, kk_ref, kka_ref, s0_ref, y_ref, sl_ref):
    tlen = r_ref.shape[0]
    n = r_ref.shape[2]
    sl_ref[...] = s0_ref[...]
    for t in range(tlen):
        w = jnp.exp(lw_ref[t, 0])
        kk = kk_ref[t, 0]
        kka = kka_ref[t, 0]
        kx = k_ref[t, 0]
        rx = r_ref[t, 0]

        def row(i, _):
            s = sl_ref[0, i]
            sa = -jnp.sum(s * kk, axis=0, keepdims=True)
            s = s * w + sa * kka + v_ref[t, 0, pl.ds(i, 1), :] * kx
            sl_ref[0, i] = s
            y_ref[t, 0, pl.ds(i, 1), :] = jnp.sum(s * rx, axis=0, keepdims=True)
            return 0

        lax.fori_loop(0, n, row, 0, unroll=4)


def _rwkv_steps(r, lw, k, v, kk, kka, s0):
    tlen, nh, n, bsz = r.shape
    spec_x = pl.BlockSpec((tlen, 1, n, bsz), lambda h: (0, h, 0, 0))
    spec_s = pl.BlockSpec((1, n, n, bsz), lambda h: (h, 0, 0, 0))
    return pl.pallas_call(
        _rwkv_steps_kernel,
        grid=(nh,),
        in_specs=[spec_x] * 6 + [spec_s],
        out_specs=(spec_x, spec_s),
        out_shape=(jax.ShapeDtypeStruct(r.shape, F32), jax.ShapeDtypeStruct(s0.shape, F32)),
        compiler_params=_cp(("parallel",)),
        name="rwkv_steps",
    )(r, lw, k, v, kk, kka, s0)


def _rwkv_post_kernel(y_ref, r_ref, k_ref, v_ref, g_ref, lnw_ref, lnb_ref, rk_ref, o_ref):
    e128 = _seg_ones()
    y = y_ref[...]
    mean = _segsum64(y, e128) * (1.0 / HEAD64)
    d = y - mean
    var = _segsum64(d * d, e128) * (1.0 / HEAD64)
    yn = d * lax.rsqrt(var + RWKV_LN_EPS) * lnw_ref[...] + lnb_ref[...]
    bonus = _segsum64(r_ref[...] * k_ref[...] * rk_ref[...], e128) * v_ref[...]
    o_ref[...] = (yn + bonus) * g_ref[...]


def _rwkv_post(y, r, k, v, g, ln_w, ln_b, r_k, tm):
    m, w = y.shape
    spec_x = pl.BlockSpec((tm, w), lambda i: (i, 0))
    spec_p = pl.BlockSpec((1, w), lambda i: (0, 0))
    row1 = lambda a: a.astype(F32).reshape(1, w)
    return pl.pallas_call(
        _rwkv_post_kernel,
        grid=(m // tm,),
        in_specs=[spec_x] * 5 + [spec_p] * 3,
        out_specs=spec_x,
        out_shape=jax.ShapeDtypeStruct((m, w), F32),
        compiler_params=_cp(("parallel",)),
        name="rwkv_post",
    )(y, r, k, v, g, row1(ln_w), row1(ln_b), row1(r_k))


def _qknorm_kernel(x_ref, g_ref, o_ref):
    x = x_ref[...]
    x2 = x.reshape(-1, x.shape[-1])
    ms = _segsum64(x2 * x2, _seg_ones()) * (1.0 / HEAD64)
    o_ref[...] = (x2 * lax.rsqrt(ms + RMS_EPS) * g_ref[...]).reshape(x.shape)


def _qknorm_blocks(proj, first_blk, nblk, gain, tr):
    length = proj.shape[1]
    g = jnp.tile(gain.astype(F32), LANE // HEAD64).reshape(1, LANE)
    return pl.pallas_call(
        _qknorm_kernel,
        grid=(nblk, length // tr),
        in_specs=[pl.BlockSpec((1, tr, LANE), lambda h, i: (first_blk + h, i, 0)),
                  pl.BlockSpec((1, LANE), lambda h, i: (0, 0))],
        out_specs=pl.BlockSpec((1, tr, LANE), lambda h, i: (h, i, 0)),
        out_shape=jax.ShapeDtypeStruct((nblk, length, LANE), F32),
        compiler_params=_cp(("parallel", "parallel")),
        name="qknorm_blocks",
    )(proj, g)


def _qknorm_cols(proj, col_blk, width, gain, tr):
    m = proj.shape[0]
    g = jnp.tile(gain.astype(F32), width // HEAD64).reshape(1, width)
    return pl.pallas_call(
        _qknorm_kernel,
        grid=(m // tr,),
        in_specs=[pl.BlockSpec((tr, width), lambda i: (i, col_blk)),
                  pl.BlockSpec((1, width), lambda i: (0, 0))],
        out_specs=pl.BlockSpec((tr, width), lambda i: (i, 0)),
        out_shape=jax.ShapeDtypeStruct((m, width), F32),
        compiler_params=_cp(("parallel",)),
        name="qknorm_cols",
    )(proj, g)


def _sb_weights(z2, tri, mask, carry):
    return _sb_finish(z2, _sb_prepare(z2, tri, mask), mask, carry)


def _sb_prepare(z2, tri, mask):
    sign = jnp.uint32(0x80000000)
    neg_abs = lax.bitcast_convert_type(lax.bitcast_convert_type(z2, jnp.uint32) | sign, F32)
    n = jnp.maximum(z2, 0.0) + jnp.log2(1.0 + jnp.exp2(neg_abs))
    if mask is not None:
        n = jnp.where(mask, n, 0.0)
    sub = tri.shape[0]
    out = []
    for b in range(z2.shape[1] // sub - 1, -1, -1):
        nb = n[:, b * sub:(b + 1) * sub]
        out.append((_dot(nb.astype(BF16), tri), jnp.sum(nb, axis=-1, keepdims=True)))
    return out


def _sb_finish(z2, prepared, mask, carry):
    parts = []
    for rest, total in prepared:
        parts.append(carry - rest)
        carry = carry - total
    log2_rest = parts[0] if len(parts) == 1 else jnp.concatenate(parts[::-1], axis=1)
    att = jnp.exp2(z2 + log2_rest)
    if mask is not None:
        att = jnp.where(mask, att, 0.0)
    return att, carry


def _sb_block(qh, kblk, vblk, tri, mask, carry):
    att, carry = _sb_weights(_dot_nt(qh, kblk), tri, mask, carry)
    return _dot(att.astype(BF16), vblk), carry


def _later_tri(n):
    return (lax.broadcasted_iota(jnp.int32, (n, n), 0)
            >= lax.broadcasted_iota(jnp.int32, (n, n), 1)).astype(BF16)


def _stack_halves(q):
    first = lax.broadcasted_iota(jnp.int32, q.shape, 1) < HEAD64
    return jnp.concatenate([jnp.where(first, q, 0.0), jnp.where(first, 0.0, q)], axis=0)


def _key_minus_query(tq, tk):
    row = lax.broadcasted_iota(jnp.int32, (2 * tq, tk), 0)
    row = jnp.where(row >= tq, row - tq, row)
    return lax.broadcasted_iota(jnp.int32, (2 * tq, tk), 1) - row


def _sb_prompt_kernel(q_ref, k_ref, v_ref, o_ref, *, tq, tk, scale):
    qi = pl.program_id(1)
    qs = _stack_halves(q_ref[0] * scale).astype(BF16)
    tri = _later_tri(LANE)
    kmq = _key_minus_query(tq, tk)

    def block(kb, carry, masked):
        c, acc = carry
        rows = pl.ds(pl.multiple_of(kb * tk, tk), tk)
        kblk = k_ref[0, rows, :].astype(BF16)
        vblk = v_ref[0, rows, :].astype(BF16)
        mask = (kmq < qi * tq - kb * tk) if masked else None
        d, c = _sb_block(qs, kblk, vblk, tri, mask, c)
        return c, acc + d

    carry = (jnp.zeros((2 * tq, 1), F32), jnp.zeros((2 * tq, LANE), F32))
    n_full = (qi * tq) // tk
    last = ((qi + 1) * tq - 1) // tk
    carry = lax.fori_loop(0, last + 1 - n_full, lambda it, cr: block(last - it, cr, True), carry)
    def group(top, size, cr):
        c, acc = cr
        rows = [pl.ds(pl.multiple_of((top - s) * tk, tk), tk) for s in range(size)]
        zs = [_dot_nt(qs, k_ref[0, r, :].astype(BF16)) for r in rows]
        prepared = [_sb_prepare(z, tri, None) for z in zs]
        for z, prep, r in zip(zs, prepared, rows):
            att, c = _sb_finish(z, prep, None, c)
            acc = acc + _dot(att.astype(BF16), v_ref[0, r, :].astype(BF16))
        return c, acc

    top = n_full - 1
    for size in (1, 2):
        has = (n_full // size) % 2
        carry = lax.fori_loop(0, has, lambda it, cr, top=top, size=size: group(top, size, cr), carry)
        top = top - has * size
    carry = lax.fori_loop(0, n_full // 4, lambda it, cr, top=top: group(top - 4 * it, 4, cr), carry)
    acc = carry[1]
    first = lax.broadcasted_iota(jnp.int32, (tq, LANE), 1) < HEAD64
    o_ref[...] = jnp.where(first, acc[:tq], acc[tq:])


def _sb_prompt(proj, length, tq, tk):
    npair = 8
    return pl.pallas_call(
        functools.partial(_sb_prompt_kernel, tq=tq, tk=tk, scale=HEAD64 ** -0.5 * LOG2E),
        grid=(npair, length // tq),
        in_specs=[pl.BlockSpec((1, tq, LANE), lambda p, i: (p, i, 0)),
                  pl.BlockSpec((1, length, LANE), lambda p, i: (npair + p, 0, 0)),
                  pl.BlockSpec((1, length, LANE), lambda p, i: (2 * npair + p, 0, 0))],
        out_specs=pl.BlockSpec((tq, LANE), lambda p, i: (i, p)),
        out_shape=jax.ShapeDtypeStruct((length, npair * LANE), F32),
        compiler_params=_cp(("parallel", "parallel")),
        name="sb_prompt",
    )(proj, proj, proj)


def _diff_lambda(lqk_ref, lam_init):
    lqk = lqk_ref[...]
    s1 = jnp.sum(lqk[0:1] * lqk[1:2], axis=-1, keepdims=True)
    s2 = jnp.sum(lqk[2:3] * lqk[3:4], axis=-1, keepdims=True)
    return jnp.exp(s1) - jnp.exp(s2) + lam_init


def _softmax_block(qh, kblk, vblk, bias, state):
    m, l, acc = state
    s = _dot_nt(qh, kblk) + bias
    m_new = jnp.maximum(m, jnp.max(s, axis=-1, keepdims=True))
    p = jnp.exp2(s - m_new)
    corr = jnp.exp2(m - m_new)
    return (m_new, corr * l + jnp.sum(p, axis=-1, keepdims=True),
            corr * acc + _dot(p.astype(BF16), vblk))


def _diff_prompt_kernel(q_ref, k_ref, v_ref, slope_ref, lqk_ref, sub_ref, o_ref, *,
                        tq, tk, scale, lam_init):
    qi = pl.program_id(1)
    qs = _stack_halves(q_ref[0] * scale).astype(BF16)
    slope = slope_ref[0][:, :1] * LOG2E
    kmq = _key_minus_query(tq, tk)
    alibi = slope * kmq.astype(F32)

    def block(kb, state, masked):
        rows = pl.ds(pl.multiple_of(kb * tk, tk), tk)
        kblk = k_ref[0, rows, :].astype(BF16)
        vblk = v_ref[0, rows, :].astype(BF16)
        off = kb * tk - qi * tq
        bias = alibi + slope * off.astype(F32)
        if masked:
            bias = jnp.where(kmq <= -off, bias, NEG_INF)
        return _softmax_block(qs, kblk, vblk, bias, state)

    state = (jnp.full((2 * tq, 1), NEG_INF, F32), jnp.zeros((2 * tq, 1), F32),
             jnp.zeros((2 * tq, LANE), F32))
    n_full = (qi * tq) // tk
    last = ((qi + 1) * tq - 1) // tk
    def group(kb0, size, st):
        m_old, l_old, acc = st
        rows = [pl.ds(pl.multiple_of((kb0 + b) * tk, tk), tk) for b in range(size)]
        ss = [_dot_nt(qs, k_ref[0, r, :].astype(BF16)) + alibi
              + slope * ((kb0 + b) * tk - qi * tq).astype(F32) for b, r in enumerate(rows)]
        m_new = m_old
        for s in ss:
            m_new = jnp.maximum(m_new, jnp.max(s, axis=-1, keepdims=True))
        corr = jnp.exp2(m_old - m_new)
        l_new = corr * l_old
        acc = corr * acc
        for s, r in zip(ss, rows):
            p = jnp.exp2(s - m_new)
            l_new = l_new + jnp.sum(p, axis=-1, keepdims=True)
            acc = acc + _dot(p.astype(BF16), v_ref[0, r, :].astype(BF16))
        return m_new, l_new, acc

    state = lax.fori_loop(0, n_full // 4, lambda it, st: group(4 * it, 4, st), state)
    done = n_full // 4 * 4
    for size in (2, 1):
        has = (n_full // size) % 2
        state = lax.fori_loop(0, has, lambda it, st, done=done, size=size: group(done, size, st), state)
        done = done + has * size
    state = lax.fori_loop(n_full, last + 1, lambda kb, st: block(kb, st, True), state)
    _, l, acc = state
    lam = _diff_lambda(lqk_ref, lam_init)
    o = acc[:tq] / l[:tq] - lam * (acc[tq:] / l[tq:])
    o_ref[...] = _rms_rows(o, sub_ref[...]) * (1.0 - lam_init)


def _diff_slopes(nheads):
    return jnp.exp2(-8.0 * jnp.arange(1, nheads + 1, dtype=F32) / nheads)


def _diff_prompt(qn, kn, proj, v_blk0, slopes, lqk, subln, length, tq, tk, lam_init):
    nheads = qn.shape[0]
    return pl.pallas_call(
        functools.partial(_diff_prompt_kernel, tq=tq, tk=tk, scale=HEAD64 ** -0.5 * LOG2E, lam_init=lam_init),
        grid=(nheads, length // tq),
        in_specs=[pl.BlockSpec((1, tq, LANE), lambda h, i: (h, i, 0)),
                  pl.BlockSpec((1, length, LANE), lambda h, i: (h, 0, 0)),
                  pl.BlockSpec((1, length, LANE), lambda h, i: (v_blk0 + h, 0, 0)),
                  pl.BlockSpec((1, 1, LANE), lambda h, i: (h, 0, 0)),
                  pl.BlockSpec(lqk.shape, lambda h, i: (0, 0)),
                  pl.BlockSpec((1, LANE), lambda h, i: (0, 0))],
        out_specs=pl.BlockSpec((tq, LANE), lambda h, i: (i, h)),
        out_shape=jax.ShapeDtypeStruct((length, nheads * LANE), F32),
        compiler_params=_cp(("parallel", "parallel")),
        name="diff_prompt",
    )(qn, kn, proj, jnp.broadcast_to(slopes.reshape(nheads, 1, 1), (nheads, 1, LANE)), lqk,
      subln.astype(F32).reshape(1, LANE))


def _page_specs(page_table, pages_per_step, block):
    npages = page_table.shape[1]
    zeros = (0,) * (len(block) - 1)

    def spec(i):
        return pl.BlockSpec(block, lambda b, j, pt: (pt[b, npages - 1 - (j * pages_per_step + i)],) + zeros)

    return [spec(i) for i in range(pages_per_step)]


def _query_blockdiag(q, seg):
    tlen, w = q.shape
    nseg = w // seg
    rows = nseg * tlen
    qt = jnp.broadcast_to(q[None], (nseg, tlen, w)).reshape(rows, w)
    keep = (lax.broadcasted_iota(jnp.int32, (rows, w), 1) // seg
            == lax.broadcasted_iota(jnp.int32, (rows, w), 0) // tlen)
    return jnp.where(keep, qt, 0.0)


def _sb_sample_kernel(pt_ref, q_ref, knt_ref, vnt_ref, *refs, pages_per_step, scale):
    del pt_ref
    k_refs, v_refs = refs[:pages_per_step], refs[pages_per_step:2 * pages_per_step]
    o_ref, qbd_ref, kpad_ref, vpad_ref, acc_ref, car_ref = refs[2 * pages_per_step:]
    j = pl.program_id(1)
    tlen, w = q_ref.shape[1:]
    tnew = knt_ref.shape[2]
    page = kpad_ref.shape[1]
    rows = qbd_ref.shape[0]
    tri = _later_tri(page)

    @pl.when(j == 0)
    def _():
        qbd_ref[...] = _query_blockdiag(q_ref[0] * scale, HEAD64).astype(BF16)
        kpad_ref[...] = jnp.zeros(kpad_ref.shape, F32)
        vpad_ref[...] = jnp.zeros(vpad_ref.shape, F32)
        kpad_ref[:, :tnew] = knt_ref[0]
        vpad_ref[:, :tnew] = vnt_ref[0]
        t = lax.broadcasted_iota(jnp.int32, (rows, page), 0) % tlen
        s = lax.broadcasted_iota(jnp.int32, (rows, page), 1)
        z = _dot(qbd_ref[...], kpad_ref[...].astype(BF16))
        att, car_ref[...] = _sb_weights(z, tri, s < t, jnp.zeros(car_ref.shape, F32))
        acc_ref[...] = _dot_nt(att.astype(BF16), vpad_ref[...].astype(BF16))

    qbd = qbd_ref[...]
    zs = [_dot(qbd, k_ref[0].reshape(w, page).astype(BF16)) for k_ref in k_refs]
    prepared = [_sb_prepare(z, tri, None) for z in zs]
    car = car_ref[...]
    acc = acc_ref[...]
    for z, prep, v_ref in zip(zs, prepared, v_refs):
        att, car = _sb_finish(z, prep, None, car)
        acc = acc + _dot_nt(att.astype(BF16), v_ref[0].reshape(w, page).astype(BF16))
    car_ref[...] = car
    acc_ref[...] = acc

    @pl.when(j == pl.num_programs(1) - 1)
    def _():
        keep = (lax.broadcasted_iota(jnp.int32, (rows, w), 1) // HEAD64
                == lax.broadcasted_iota(jnp.int32, (rows, w), 0) // tlen)
        o_ref[0] = jnp.sum(jnp.where(keep, acc_ref[...], 0.0).reshape(rows // tlen, tlen, w), axis=0)


def _sb_sample(page_table, proj3, knt, vnt, cache_kt, cache_vt, pages_per_step):
    bsz, tlen, _ = proj3.shape
    nh, hd, page = cache_kt.shape[1:]
    w = nh * hd
    npages = page_table.shape[1]
    rows = nh * tlen
    seq = lambda shape: pl.BlockSpec((1,) + shape, lambda b, j, pt: (b, 0, 0))
    pages = _page_specs(page_table, pages_per_step, (1, nh, hd, page))
    grid_spec = pltpu.PrefetchScalarGridSpec(
        num_scalar_prefetch=1,
        grid=(bsz, npages // pages_per_step),
        in_specs=[seq((tlen, w)), seq((w, tlen)), seq((w, tlen))] + pages + pages,
        out_specs=seq((tlen, w)),
        scratch_shapes=[pltpu.VMEM((rows, w), BF16), pltpu.VMEM((w, page), F32), pltpu.VMEM((w, page), F32),
                        pltpu.VMEM((rows, w), F32), pltpu.VMEM((rows, 1), F32)])
    return pl.pallas_call(
        functools.partial(_sb_sample_kernel, pages_per_step=pages_per_step, scale=HEAD64 ** -0.5 * LOG2E),
        grid_spec=grid_spec,
        out_shape=jax.ShapeDtypeStruct((bsz, tlen, w), F32),
        compiler_params=_cp(("parallel", "arbitrary")),
        name="sb_sample",
    )(page_table, proj3, knt, vnt, *([cache_kt] * pages_per_step), *([cache_vt] * pages_per_step))


def _diff_sample_kernel(pt_ref, q_ref, knt_ref, vn_ref, slope_ref, lqk_ref, sub_ref, *refs,
                        pages_per_step, scale, lam_init, past_len):
    del pt_ref
    k_refs, v_refs = refs[:pages_per_step], refs[pages_per_step:2 * pages_per_step]
    o_ref, kpad_ref, vpad_ref, acc_ref, m_ref, l_ref = refs[2 * pages_per_step:]
    j = pl.program_id(1)
    nmap, tlen, _ = q_ref.shape[1:]
    nh = nmap // 2
    tnew = knt_ref.shape[3]
    page = kpad_ref.shape[2]
    npages = pl.num_programs(1) * pages_per_step
    rows = nmap * tlen
    t = lax.broadcasted_iota(jnp.int32, (rows, page), 0) % tlen
    s = lax.broadcasted_iota(jnp.int32, (rows, page), 1)
    slope = slope_ref[...] * LOG2E
    q = (q_ref[0] * scale).astype(BF16)

    wide = page * nh
    spread = (lax.broadcasted_iota(jnp.int32, (page, wide), 1) // nh
              == lax.broadcasted_iota(jnp.int32, (page, wide), 0)).astype(BF16)
    own_head = (lax.broadcasted_iota(jnp.int32, (rows, wide), 1) % nh
                == lax.broadcasted_iota(jnp.int32, (rows, wide), 0) // (2 * tlen))

    def visit(kts, v2s, biases, state):
        m_old, l_old, acc = state
        scs = [jnp.einsum('rtd,rds->rts', q, kt.astype(BF16), preferred_element_type=F32).reshape(rows, page)
               + bias for kt, bias in zip(kts, biases)]
        m_new = m_old
        for sc in scs:
            m_new = jnp.maximum(m_new, jnp.max(sc, axis=-1, keepdims=True))
        corr = jnp.exp2(m_old - m_new)
        l_new = corr * l_old
        acc = corr * acc
        for sc, v2 in zip(scs, v2s):
            p = jnp.exp2(sc - m_new)
            l_new = l_new + jnp.sum(p, axis=-1, keepdims=True)
            pw = jnp.where(own_head, _dot(p.astype(BF16), spread), 0.0).astype(BF16)
            acc = acc + _dot(pw, v2.reshape(wide, v2.shape[-1]).astype(BF16))
        return m_new, l_new, acc

    @pl.when(j == 0)
    def _():
        kpad_ref[...] = jnp.zeros(kpad_ref.shape, F32)
        vpad_ref[...] = jnp.zeros(vpad_ref.shape, F32)
        kpad_ref[:, :, :tnew] = knt_ref[0]
        vpad_ref[:tnew] = vn_ref[0]
        bias = jnp.where(s <= t, -slope * (t - s).astype(F32), NEG_INF)
        init = (jnp.full(m_ref.shape, NEG_INF, F32), jnp.zeros(l_ref.shape, F32),
                jnp.zeros(acc_ref.shape, F32))
        m_ref[...], l_ref[...], acc_ref[...] = visit([kpad_ref[...]], [vpad_ref[...]], [bias], init)

    biases = []
    for i in range(pages_per_step):
        kpos = (npages - 1 - (j * pages_per_step + i)) * page + s
        biases.append(-slope * (past_len + t - kpos).astype(F32))
    m_ref[...], l_ref[...], acc_ref[...] = visit([r[0] for r in k_refs], [r[0] for r in v_refs], biases,
                                                 (m_ref[...], l_ref[...], acc_ref[...]))

    @pl.when(j == pl.num_programs(1) - 1)
    def _():
        lam = _diff_lambda(lqk_ref, lam_init)
        o3 = (acc_ref[...] / l_ref[...]).reshape(nh, 2 * tlen, acc_ref.shape[1])
        for h in range(nh):
            o = o3[h, :tlen] - lam * o3[h, tlen:]
            o_ref[0, h] = _rms_rows(o, sub_ref[...]) * (1.0 - lam_init)


def _diff_sample(page_table, q4, knt, vn4, cache_kt, cache_v, slopes, lqk, subln, lam_init, pages_per_step):
    bsz, nmap, tlen, hd = q4.shape
    npages = page_table.shape[1]
    page = cache_kt.shape[3]
    nheads, dv = cache_v.shape[2], cache_v.shape[3]
    rows = nmap * tlen
    slope_rows = jnp.broadcast_to(jnp.repeat(slopes, rows // nheads)[:, None], (rows, page))
    seq = lambda a: pl.BlockSpec((1,) + a.shape[1:], lambda b, j, pt: (b, 0, 0, 0))
    const2 = lambda shape: pl.BlockSpec(shape, lambda b, j, pt: (0, 0))
    grid_spec = pltpu.PrefetchScalarGridSpec(
        num_scalar_prefetch=1,
        grid=(bsz, npages // pages_per_step),
        in_specs=([seq(q4), seq(knt), seq(vn4), const2((rows, page)), const2(lqk.shape), const2((1, dv))]
                  + _page_specs(page_table, pages_per_step, (1, nmap, hd, page))
                  + _page_specs(page_table, pages_per_step, (1, page, nheads, dv))),
        out_specs=pl.BlockSpec((1, nheads, tlen, dv), lambda b, j, pt: (b, 0, 0, 0)),
        scratch_shapes=[pltpu.VMEM((nmap, hd, page), F32), pltpu.VMEM((page, nheads, dv), F32),
                        pltpu.VMEM((rows, dv), F32),
                        pltpu.VMEM((rows, 1), F32), pltpu.VMEM((rows, 1), F32)])
    return pl.pallas_call(
        functools.partial(_diff_sample_kernel, pages_per_step=pages_per_step, scale=HEAD64 ** -0.5 * LOG2E,
                          lam_init=lam_init, past_len=npages * page),
        grid_spec=grid_spec,
        out_shape=jax.ShapeDtypeStruct((bsz, nheads, tlen, dv), F32),
        compiler_params=_cp(("parallel", "arbitrary")),
        name="diff_sample",
    )(page_table, q4, knt, vn4, slope_rows, lqk, subln.astype(F32).reshape(1, dv),
      *([cache_kt] * pages_per_step), *([cache_v] * pages_per_step))


def _largest_tile(m, cap, mult=16):
    t = min(m, cap)
    while m % t or t % mult:
        t -= mult
    return t


def _layer0(x, bsz, tlen, n_valid, s5_re0, s5_im0, rwkv_s0, shift_prev, wt):
    m = x.shape[0]
    tm = _largest_tile(m, 640)
    u = _proj(x, wt['norm_mix0'], wt['w_in0_u'], tm, 512)
    p = _proj(x, wt['norm_mix0'], wt['w_in0_p'], tm, 1152)

    bu_re, bu_im = _s5_b(u, wt['s5_wb_re'], wt['s5_wb_im'], tm)
    width = bu_re.shape[1]
    assert bsz == 1 or tlen == S5_STEPS
    h_re, h_im = _s5_scan(bu_re, bu_im, s5_re0, s5_im0, wt['s5_lb_re'], wt['s5_lb_im'],
                          LANE if bsz == 1 else _largest_tile(m, 256, 64), bsz == 1)
    last = lambda h: h.reshape(bsz, tlen, width)[:, n_valid - 1]
    f_re, f_im = last(h_re), last(h_im)
    ya = _s5_c(h_re, h_im, u, wt['s5_wc_re'], wt['s5_wc_im'],
               wt['s5_d'], wt['s5_w_glu'], _largest_tile(m, 256))

    c = p.shape[1]
    if bsz == 1:
        tr = _largest_tile(tlen, 320)
        prev = jnp.concatenate([shift_prev.astype(F32), p[tr - 1::tr][:-1]])[:, None, :]
        p3, nb = p.reshape(tlen // tr, tr, c), 1
    else:
        prev = shift_prev.astype(F32)[:, None, :]
        p3, nb = p.reshape(bsz, tlen, c), _largest_tile(bsz, 32, 1)
    r, lw, k, v, kk, kka, g = _rwkv_pre(p3, prev, wt['rwkv_mu'], wt['rwkv_w0'], wt['rwkv_w2'],
                                        wt['rwkv_a0'], wt['rwkv_a2'], wt['rwkv_g2'], wt['rwkv_k_k'],
                                        wt['rwkv_k_a'], nb, m if bsz > 1 else n_valid)
    w_r = r.shape[1]
    if bsz == 1:
        v3 = lambda a: a.reshape(bsz, tlen, w_r)
        y, s_last = _rwkv_scan(v3(r), v3(lw), v3(k), v3(v), v3(kk), v3(kka), rwkv_s0,
                               _largest_tile(tlen, 640, LANE), HEAD64)
        y = y.reshape(m, w_r)
    else:
        lanes_b = lambda a: jnp.transpose(a.reshape(bsz, tlen, w_r // HEAD64, HEAD64), (1, 2, 3, 0))
        y, s_last = _rwkv_steps(lanes_b(r), lanes_b(lw), lanes_b(k), lanes_b(v), lanes_b(kk), lanes_b(kka),
                                jnp.transpose(rwkv_s0.astype(F32), (1, 2, 3, 0)))
        y = jnp.transpose(y, (3, 0, 1, 2)).reshape(m, w_r)
        s_last = jnp.transpose(s_last, (3, 0, 1, 2))
    yb = _rwkv_post(y, r, k, v, g, wt['rwkv_ln_w'], wt['rwkv_ln_b'], wt['rwkv_r_k'], _largest_tile(m, 512))

    x = _outproj(x, ya, yb, wt['w_out0_a'], wt['w_out0_b'], tm, 512)
    x = _ffn(x, wt['norm_ffn0'], wt['ffn_w1'], wt['ffn_w3'], wt['ffn_w2'], tm, 512)
    return x, f_re, f_im, s_last, p


def kernel(x_prompt, x_sample, state_s5_re, state_s5_im, state_rwkv, state_rwkv_shift, cache_sb_k, cache_sb_v, cache_diff_k, cache_diff_v, page_table, meta_tokens, norm_mix0, w_in0, s5_lambda_re, s5_lambda_im, s5_log_dt, s5_b_re, s5_b_im, s5_c_re, s5_c_im, s5_d, s5_w_glu, rwkv_mu, rwkv_w0, rwkv_w2, rwkv_a0, rwkv_a2, rwkv_g2, rwkv_k_k, rwkv_k_a, rwkv_r_k, rwkv_ln_w, rwkv_ln_b, w_out0, norm_ffn0, ffn_w1, ffn_w3, ffn_w2, norm_mix1, w_in1, diff_q_norm, diff_k_norm, diff_lambda_q1, diff_lambda_k1, diff_lambda_q2, diff_lambda_k2, diff_subln, w_out1, norm_ffn1, moe_router, moe_w1, moe_w3, moe_w2):
    d_model = x_prompt.shape[-1]
    n_groups, n_state, grp = s5_b_re.shape
    s5_width = n_groups * grp
    rwkv_width = rwkv_w0.shape[0]
    rwkv_heads = rwkv_width // HEAD64
    sb_width = cache_sb_k.shape[2] * cache_sb_k.shape[3]
    diff_heads = cache_diff_v.shape[2]
    diff_qk_width = cache_diff_k.shape[2] * cache_diff_k.shape[3] * cache_diff_k.shape[4]
    diff_v_width = diff_heads * cache_diff_v.shape[3]
    lam_init = 0.8 - 0.6 * math.exp(-0.3)
    bf = lambda a: a.astype(BF16)

    lb_re, lb_im, bb_re, bb_im = _s5_discretize(s5_lambda_re, s5_lambda_im, s5_log_dt, s5_b_re, s5_b_im)
    to_ghp = lambda a: jnp.transpose(a, (1, 0, 2))
    wt = dict(
        norm_mix0=norm_mix0.astype(F32), w_in0_u=bf(w_in0[:, :s5_width]), w_in0_p=bf(w_in0[:, s5_width:]),
        s5_lb_re=lb_re, s5_lb_im=lb_im,
        s5_wb_re=bf(_blockdiag(to_ghp(bb_re), 8)), s5_wb_im=bf(_blockdiag(to_ghp(bb_im), 8)),
        s5_wc_re=bf(_blockdiag(jnp.transpose(s5_c_re.astype(F32), (0, 2, 1)), 16)),
        s5_wc_im=bf(_blockdiag(jnp.transpose(s5_c_im.astype(F32), (0, 2, 1)), 16)),
        s5_d=s5_d.reshape(-1), s5_w_glu=bf(s5_w_glu),
        rwkv_mu=rwkv_mu, rwkv_w0=rwkv_w0, rwkv_w2=bf(rwkv_w2), rwkv_a0=rwkv_a0, rwkv_a2=bf(rwkv_a2),
        rwkv_g2=bf(rwkv_g2), rwkv_k_k=rwkv_k_k, rwkv_k_a=rwkv_k_a, rwkv_r_k=rwkv_r_k,
        rwkv_ln_w=rwkv_ln_w, rwkv_ln_b=rwkv_ln_b,
        w_out0_a=bf(w_out0[:s5_width]), w_out0_b=bf(w_out0[s5_width:]),
        norm_ffn0=norm_ffn0.astype(F32), ffn_w1=bf(ffn_w1), ffn_w3=bf(ffn_w3), ffn_w2=bf(ffn_w2),
    )
    w_in1_b = bf(w_in1)
    w_out1_a, w_out1_b = bf(w_out1[:sb_width]), bf(w_out1[sb_width:])
    moe_w1_b, moe_w3_b, moe_w2_b = bf(moe_w1), bf(moe_w3), bf(moe_w2)
    lqk = jnp.stack([diff_lambda_q1, diff_lambda_k1, diff_lambda_q2, diff_lambda_k2]).astype(F32)
    slopes = _diff_slopes(diff_heads)
    q_col = 3 * sb_width // diff_qk_width
    npair = sb_width // LANE

    bp, seq, _ = x_prompt.shape
    assert bp == 1
    real = N_META + seq
    lp = real + (-real) % LANE
    xp = jnp.concatenate([meta_tokens.astype(F32), x_prompt[0].astype(F32), jnp.zeros((lp - real, d_model), F32)])
    tr_p = _largest_tile(lp, 640)
    tm_p = tr_p
    c_proj = w_in0.shape[1] - s5_width
    z_s5 = jnp.zeros((1, n_groups, n_state), F32)
    z_rwkv = jnp.zeros((1, rwkv_heads, HEAD64, HEAD64), F32)
    xh, p_s5_re, p_s5_im, p_rwkv, p_proj = _layer0(xp, 1, lp, real, z_s5, z_s5, z_rwkv,
                                                    jnp.zeros((1, c_proj), F32), wt)
    p_rwkv_shift = p_proj[real - 1:real]

    proj1 = _proj(xh, norm_mix1.astype(F32), w_in1_b, tm_p, 512, split=True)
    blk_dq = 3 * npair
    blk_dk = blk_dq + diff_heads
    blk_dv = blk_dk + diff_heads
    dqn = _qknorm_blocks(proj1, blk_dq, diff_heads, diff_q_norm, tr_p)
    dkn = _qknorm_blocks(proj1, blk_dk, diff_heads, diff_k_norm, tr_p)
    tq = _largest_tile(lp, 640, LANE)
    y_sb = _sb_prompt(proj1, lp, LANE, tq)
    y_d = _diff_prompt(dqn, dkn, proj1, blk_dv, slopes, lqk, diff_subln, lp, LANE, tq, lam_init)
    xh = _outproj(xh, y_sb, y_d, w_out1_a, w_out1_b, tm_p, 512)
    xh = _moe(xh, norm_ffn1.astype(F32), moe_router, moe_w1_b, moe_w3_b, moe_w2_b, tm_p, 256)
    y_prompt = xh[N_META:real][None]

    def tokens(blocks):
        return jnp.transpose(blocks[:, :real], (1, 0, 2)).reshape(1, real, -1)

    p_sb_k = tokens(proj1[npair:2 * npair]).reshape(1, real, sb_width // HEAD64, HEAD64)
    p_sb_v = tokens(proj1[2 * npair:3 * npair]).reshape(1, real, sb_width // HEAD64, HEAD64)
    p_diff_k = tokens(dkn).reshape(1, real, diff_heads, 2, HEAD64)
    p_diff_v = tokens(proj1[blk_dv:blk_dv + diff_heads]).reshape(1, real, diff_heads, diff_v_width // diff_heads)

    db, dseq, _ = x_sample.shape
    ms = db * dseq
    xs = x_sample.astype(F32).reshape(ms, d_model)
    xs, s_s5_re, s_s5_im, s_rwkv, s_proj = _layer0(xs, db, dseq, dseq, state_s5_re, state_s5_im, state_rwkv,
                                                   state_rwkv_shift, wt)
    s_rwkv_shift = s_proj.reshape(db, dseq, c_proj)[:, -1]

    tm_s = _largest_tile(ms, 640)
    proj_s = _proj(xs, norm_mix1.astype(F32), w_in1_b, tm_s, 512)
    dqn_s = _qknorm_cols(proj_s, q_col, diff_qk_width, diff_q_norm, tm_s)
    dkn_s = _qknorm_cols(proj_s, q_col + 1, diff_qk_width, diff_k_norm, tm_s)
    n_pool, page = cache_sb_k.shape[0], cache_sb_k.shape[1]
    pt = page_table.astype(jnp.int32)
    pages_per_step = next(n for n in (8, 4, 2, 1) if pt.shape[1] % n == 0)
    v_col = (3 * sb_width + 2 * diff_qk_width) // diff_v_width
    t_last = lambda a: jnp.transpose(a.reshape(db, dseq, -1), (0, 2, 1))
    slot_last = lambda c: jnp.transpose(c.reshape(n_pool, page, -1, HEAD64), (0, 2, 3, 1))
    ys_sb = _sb_sample(pt, proj_s.reshape(db, dseq, -1), t_last(proj_s[:, sb_width:2 * sb_width]),
                       t_last(proj_s[:, 2 * sb_width:3 * sb_width]), slot_last(cache_sb_k),
                       slot_last(cache_sb_v), pages_per_step)
    heads = lambda a: a.reshape(db, dseq, -1, HEAD64)
    ys_d = _diff_sample(pt, jnp.transpose(heads(dqn_s), (0, 2, 1, 3)), jnp.transpose(heads(dkn_s), (0, 2, 3, 1)),
                        proj_s[:, v_col * diff_v_width:].reshape(db, dseq, diff_heads, -1),
                        slot_last(cache_diff_k), cache_diff_v, slopes, lqk, diff_subln, lam_init,
                        pages_per_step)
    ys_d = jnp.transpose(ys_d, (0, 2, 1, 3)).reshape(ms, -1)
    xs = _outproj(xs, ys_sb.reshape(ms, -1), ys_d, w_out1_a, w_out1_b, tm_s, 512)
    xs = _moe(xs, norm_ffn1.astype(F32), moe_router, moe_w1_b, moe_w3_b, moe_w2_b, tm_s, 256)
    y_sample = xs.reshape(db, dseq, d_model)

    s_sb_k = proj_s[:, sb_width:2 * sb_width].reshape(db, dseq, sb_width // HEAD64, HEAD64)
    s_sb_v = proj_s[:, 2 * sb_width:3 * sb_width].reshape(db, dseq, sb_width // HEAD64, HEAD64)
    s_diff_k = dkn_s.reshape(db, dseq, diff_heads, 2, HEAD64)
    s_diff_v = proj_s[:, v_col * diff_v_width:].reshape(db, dseq, diff_heads, diff_v_width // diff_heads)

    return (y_prompt, y_sample,
            p_s5_re.reshape(1, n_groups, n_state), p_s5_im.reshape(1, n_groups, n_state), p_rwkv, p_rwkv_shift,
            p_sb_k, p_sb_v, p_diff_k, p_diff_v,
            s_s5_re.reshape(db, n_groups, n_state), s_s5_im.reshape(db, n_groups, n_state), s_rwkv, s_rwkv_shift,
            s_sb_k, s_sb_v, s_diff_k, s_diff_v)
```

```python
import functools
import math

import jax
import jax.numpy as jnp
from jax import lax
from jax.experimental import pallas as pl
from jax.experimental.pallas import tpu as pltpu

F32 = jnp.float32
BF16 = jnp.bfloat16
HIGHEST = lax.Precision.HIGHEST

LANE = 128
HEAD64 = 64
RMS_EPS = 1e-6
RWKV_LN_EPS = 64e-5
N_META = 16
NEG_INF = -1e30
LOG2E = math.log2(math.e)
VMEM_LIMIT = 56 * 1024 * 1024


def _cp(sem, vmem=VMEM_LIMIT):
    return pltpu.CompilerParams(dimension_semantics=sem, vmem_limit_bytes=vmem)


def _dot(a, b, precision=None):
    return jnp.dot(a, b, preferred_element_type=F32, precision=precision)


def _dot_nt(a, b, precision=None):
    return lax.dot_general(a, b, (((1,), (1,)), ((), ())), preferred_element_type=F32,
                           precision=precision)


def _dot_tn(a, b, precision=None):
    return lax.dot_general(a, b, (((0,), (0,)), ((), ())), preferred_element_type=F32,
                           precision=precision)


def _split_bf16(x):
    hi = x.astype(BF16)
    lo = (x - hi.astype(F32)).astype(BF16)
    return hi, lo


_NN = ((1,), (0,))
_NT = ((1,), (1,))
_TN = ((0,), (0,))


def _mm3(a, b, dims):
    dg = lambda x, y: lax.dot_general(x, y, (dims, ((), ())), preferred_element_type=F32)
    return dg(a[0], b[0]) + dg(a[0], b[1]) + dg(a[1], b[0])


def _rms_rows(x, gain):
    ms = jnp.mean(x * x, axis=-1, keepdims=True)
    return x * lax.rsqrt(ms + RMS_EPS) * gain


def _softplus(t):
    return jnp.maximum(t, 0.0) + jnp.log1p(jnp.exp(-jnp.abs(t)))


def _segsum64(x, e128):
    cols = [_dot(x[:, c * LANE:(c + 1) * LANE], e128, HIGHEST) for c in range(x.shape[1] // LANE)]
    return cols[0] if len(cols) == 1 else jnp.concatenate(cols, axis=1)


def _seg_ones():
    r = lax.broadcasted_iota(jnp.int32, (LANE, LANE), 0) // HEAD64
    c = lax.broadcasted_iota(jnp.int32, (LANE, LANE), 1) // HEAD64
    return (r == c).astype(F32)


def _proj_kernel(x_ref, g_ref, w_ref, o_ref, xn_ref, *, split):
    @pl.when(pl.program_id(1) == 0)
    def _():
        xn_ref[...] = _rms_rows(x_ref[...], g_ref[...]).astype(BF16)

    acc = _dot(xn_ref[...], w_ref[...])
    if split:
        for c in range(acc.shape[1] // LANE):
            o_ref[c] = acc[:, c * LANE:(c + 1) * LANE]
    else:
        o_ref[...] = acc


def _proj(x, gain, w, tm, tn, split=False):
    m, d = x.shape
    n = w.shape[1]
    if split:
        out_shape = jax.ShapeDtypeStruct((n // LANE, m, LANE), F32)
        out_spec = pl.BlockSpec((tn // LANE, tm, LANE), lambda i, j: (j, i, 0))
    else:
        out_shape = jax.ShapeDtypeStruct((m, n), F32)
        out_spec = pl.BlockSpec((tm, tn), lambda i, j: (i, j))
    return pl.pallas_call(
        functools.partial(_proj_kernel, split=split),
        grid=(m // tm, n // tn),
        in_specs=[pl.BlockSpec((tm, d), lambda i, j: (i, 0)),
                  pl.BlockSpec((1, d), lambda i, j: (0, 0)),
                  pl.BlockSpec((d, tn), lambda i, j: (0, j))],
        out_specs=out_spec,
        out_shape=out_shape,
        scratch_shapes=[pltpu.VMEM((tm, d), BF16)],
        compiler_params=_cp(("parallel", "arbitrary")),
        name="proj",
    )(x, gain.reshape(1, d), w)


def _outproj_kernel(res_ref, a_ref, b_ref, wa_ref, wb_ref, o_ref):
    o_ref[...] = (res_ref[...] + _dot(a_ref[...].astype(BF16), wa_ref[...])
                  + _dot(b_ref[...].astype(BF16), wb_ref[...]))


def _outproj(res, a, b, wa, wb, tm, tn):
    m, n = res.shape
    ka, kb = a.shape[1], b.shape[1]
    return pl.pallas_call(
        _outproj_kernel,
        grid=(m // tm, n // tn),
        in_specs=[pl.BlockSpec((tm, tn), lambda i, j: (i, j)),
                  pl.BlockSpec((tm, ka), lambda i, j: (i, 0)),
                  pl.BlockSpec((tm, kb), lambda i, j: (i, 0)),
                  pl.BlockSpec((ka, tn), lambda i, j: (0, j)),
                  pl.BlockSpec((kb, tn), lambda i, j: (0, j))],
        out_specs=pl.BlockSpec((tm, tn), lambda i, j: (i, j)),
        out_shape=jax.ShapeDtypeStruct((m, n), F32),
        compiler_params=_cp(("parallel", "arbitrary")),
        name="outproj",
    )(res, a, b, wa, wb)


def _ffn_kernel(x_ref, g_ref, w1_ref, w3_ref, w2_ref, o_ref, xn_ref):
    @pl.when(pl.program_id(1) == 0)
    def _():
        x = x_ref[...]
        xn_ref[...] = _rms_rows(x, g_ref[...]).astype(BF16)
        o_ref[...] = x

    xn = xn_ref[...]
    h1 = _dot(xn, w1_ref[...])
    h3 = _dot(xn, w3_ref[...])
    h = (h1 * jax.nn.sigmoid(h1) * h3).astype(BF16)
    o_ref[...] += _dot(h, w2_ref[...])


def _ffn(x, gain, w1, w3, w2, tm, tf):
    m, d = x.shape
    f = w1.shape[1]
    return pl.pallas_call(
        _ffn_kernel,
        grid=(m // tm, f // tf),
        in_specs=[pl.BlockSpec((tm, d), lambda i, j: (i, 0)),
                  pl.BlockSpec((1, d), lambda i, j: (0, 0)),
                  pl.BlockSpec((d, tf), lambda i, j: (0, j)),
                  pl.BlockSpec((d, tf), lambda i, j: (0, j)),
                  pl.BlockSpec((tf, d), lambda i, j: (j, 0))],
        out_specs=pl.BlockSpec((tm, d), lambda i, j: (i, 0)),
        out_shape=jax.ShapeDtypeStruct((m, d), F32),
        scratch_shapes=[pltpu.VMEM((tm, d), BF16)],
        compiler_params=_cp(("parallel", "arbitrary")),
        name="ffn",
    )(x, gain.reshape(1, d), w1, w3, w2)


def _moe_route_kernel(x_ref, g_ref, r_ref, xn_ref, gate_ref, *, n_experts):
    tm = x_ref.shape[0]
    lane = lax.broadcasted_iota(jnp.int32, (tm, LANE), 1)
    xn = _rms_rows(x_ref[...], g_ref[...])
    xn_ref[...] = xn.astype(BF16)
    logits = jnp.where(lane < n_experts, _dot(xn, r_ref[...], HIGHEST), NEG_INF)
    v1 = jnp.max(logits, axis=-1, keepdims=True)
    i1 = jnp.min(jnp.where(logits == v1, lane, LANE), axis=-1, keepdims=True)
    rest = jnp.where(lane == i1, NEG_INF, logits)
    v2 = jnp.max(rest, axis=-1, keepdims=True)
    i2 = jnp.min(jnp.where(rest == v2, lane, LANE), axis=-1, keepdims=True)
    ex = jnp.exp(v2 - v1)
    den = 1.0 + ex
    gate_ref[...] = jnp.where(lane == i1, 1.0 / den, 0.0) + jnp.where(lane == i2, ex / den, 0.0)


def _moe_route(x, gain, router, tm):
    m, d = x.shape
    ne = router.shape[1]
    router_p = jnp.zeros((d, LANE), F32).at[:, :ne].set(router.astype(F32))
    return pl.pallas_call(
        functools.partial(_moe_route_kernel, n_experts=ne),
        grid=(m // tm,),
        in_specs=[pl.BlockSpec((tm, d), lambda i: (i, 0)),
                  pl.BlockSpec((1, d), lambda i: (0, 0)),
                  pl.BlockSpec((d, LANE), lambda i: (0, 0))],
        out_specs=(pl.BlockSpec((tm, d), lambda i: (i, 0)), pl.BlockSpec((tm, LANE), lambda i: (i, 0))),
        out_shape=(jax.ShapeDtypeStruct((m, d), BF16), jax.ShapeDtypeStruct((m, LANE), F32)),
        compiler_params=_cp(("parallel",)),
        name="moe_route",
    )(x, gain.reshape(1, d), router_p)


def _moe_kernel(cnt_ref, x_ref, xn_ref, gate_ref, gt_ref, w1_ref, w3_ref, w2_ref, o_ref,
                sel_ref, xa_ref, acc_ref, *, chunk):
    i = pl.program_id(0)
    e = pl.program_id(1)
    j = pl.program_id(2)
    tm = x_ref.shape[0]
    cnt = cnt_ref[i, e]
    chunks = [(k, slice(k * chunk, (k + 1) * chunk)) for k in range(sel_ref.shape[0] // chunk)]

    @pl.when((e == 0) & (j == 0))
    def _():
        o_ref[...] = x_ref[...]

    @pl.when((j == 0) & (cnt > 0))
    def _():
        routed = (gt_ref[pl.ds(e, 1), :] > 0.0).astype(F32)
        before = (lax.broadcasted_iota(jnp.int32, (tm, tm), 0)
                  < lax.broadcasted_iota(jnp.int32, (tm, tm), 1)).astype(BF16)
        rank = _dot(jnp.broadcast_to(routed, (8, tm)).astype(BF16), before)[0:1]
        slot = lax.broadcasted_iota(jnp.int32, (chunk, tm), 0).astype(F32)
        for k, rows in chunks:
            @pl.when(k * chunk < cnt)
            def _():
                sel = jnp.where(rank == slot + float(k * chunk), routed, 0.0).astype(BF16)
                sel_ref[rows] = sel
                xa_ref[rows] = _dot(sel, xn_ref[...]).astype(BF16)
                acc_ref[rows] = jnp.zeros((chunk, acc_ref.shape[1]), F32)

    for k, rows in chunks:
        @pl.when(k * chunk < cnt)
        def _():
            xa = xa_ref[rows]
            h1 = _dot(xa, w1_ref[0])
            h3 = _dot(xa, w3_ref[0])
            h = (h1 * jax.nn.sigmoid(h1) * h3).astype(BF16)
            acc_ref[rows] += _dot(h, w2_ref[0])

    @pl.when((j == pl.num_programs(2) - 1) & (cnt > 0))
    def _():
        lane = lax.broadcasted_iota(jnp.int32, (tm, LANE), 1)
        ge = jnp.sum(jnp.where(lane == e, gate_ref[...], 0.0), axis=-1, keepdims=True)
        for k, rows in chunks:
            @pl.when(k * chunk < cnt)
            def _():
                hi, lo = _split_bf16(acc_ref[rows])
                sel = sel_ref[rows]
                o_ref[...] += ge * (_dot_tn(sel, hi) + _dot_tn(sel, lo))


def _moe(x, gain, router, w1, w3, w2, tm, tf, chunk=192):
    m, d = x.shape
    ne, _, f = w1.shape
    xn, gate = _moe_route(x, gain, router, tm)
    counts = jnp.sum((gate[:, :ne] > 0.0).reshape(m // tm, tm, ne), axis=1).astype(jnp.int32)
    cap = -(-tm // chunk) * chunk
    grid_spec = pltpu.PrefetchScalarGridSpec(
        num_scalar_prefetch=1,
        grid=(m // tm, ne, f // tf),
        in_specs=[pl.BlockSpec((tm, d), lambda i, e, j, c: (i, 0)),
                  pl.BlockSpec((tm, d), lambda i, e, j, c: (i, 0)),
                  pl.BlockSpec((tm, LANE), lambda i, e, j, c: (i, 0)),
                  pl.BlockSpec((LANE, tm), lambda i, e, j, c: (0, i)),
                  pl.BlockSpec((1, d, tf), lambda i, e, j, c: (e, 0, j)),
                  pl.BlockSpec((1, d, tf), lambda i, e, j, c: (e, 0, j)),
                  pl.BlockSpec((1, tf, d), lambda i, e, j, c: (e, j, 0))],
        out_specs=pl.BlockSpec((tm, d), lambda i, e, j, c: (i, 0)),
        scratch_shapes=[pltpu.VMEM((cap, tm), BF16), pltpu.VMEM((cap, d), BF16), pltpu.VMEM((cap, d), F32)])
    return pl.pallas_call(
        functools.partial(_moe_kernel, chunk=chunk),
        grid_spec=grid_spec,
        out_shape=jax.ShapeDtypeStruct((m, d), F32),
        compiler_params=_cp(("parallel", "arbitrary", "arbitrary")),
        name="moe",
    )(counts, x, xn, gate, gate.T, w1, w3, w2)


def _s5_disc_kernel(lr_ref, li_ref, ldt_ref, bre_ref, bim_ref, lbre_ref, lbim_ref, bbre_ref, bbim_ref):
    lr = lr_ref[...]
    li = li_ref[...]
    dt = jnp.exp(ldt_ref[...])
    mag = jnp.exp(lr * dt)
    lb_re = mag * jnp.cos(li * dt)
    lb_im = mag * jnp.sin(li * dt)
    den = lr * lr + li * li
    f_re = ((lb_re - 1.0) * lr + lb_im * li) / den
    f_im = (lb_im * lr - (lb_re - 1.0) * li) / den
    lbre_ref[...] = lb_re
    lbim_ref[...] = lb_im
    for h in range(bre_ref.shape[0]):
        br = bre_ref[h]
        bi = bim_ref[h]
        bbre_ref[h] = f_re * br - f_im * bi
        bbim_ref[h] = f_re * bi + f_im * br


def _s5_discretize(lam_re, lam_im, log_dt, b_re, b_im):
    g, p, h = b_re.shape
    sds = jax.ShapeDtypeStruct
    return pl.pallas_call(
        _s5_disc_kernel,
        out_shape=(sds((g, p), F32), sds((g, p), F32), sds((h, g, p), F32), sds((h, g, p), F32)),
        name="s5_disc",
    )(lam_re.astype(F32), lam_im.astype(F32), log_dt.astype(F32).reshape(g, 1),
      jnp.transpose(b_re.astype(F32), (2, 0, 1)), jnp.transpose(b_im.astype(F32), (2, 0, 1)))


def _s5_b_kernel(u_ref, wre_ref, wim_ref, ore_ref, oim_ref):
    u = u_ref[...].astype(BF16)
    ore_ref[...] = _dot(u, wre_ref[0])
    oim_ref[...] = _dot(u, wim_ref[0])


def _s5_b(u, wre, wim, tm):
    m = u.shape[0]
    nc, kc, ncol = wre.shape
    spec_o = pl.BlockSpec((tm, ncol), lambda i, c: (i, c))
    spec_w = pl.BlockSpec((1, kc, ncol), lambda i, c: (c, 0, 0))
    sds = jax.ShapeDtypeStruct((m, nc * ncol), F32)
    return pl.pallas_call(
        _s5_b_kernel,
        grid=(m // tm, nc),
        in_specs=[pl.BlockSpec((tm, kc), lambda i, c: (i, c)), spec_w, spec_w],
        out_specs=(spec_o, spec_o),
        out_shape=(sds, sds),
        compiler_params=_cp(("parallel", "arbitrary")),
        name="s5_b",
    )(u, wre, wim)


S5_STEPS = 8


def _cmul(ar, ai, br, bi):
    return ar * br - ai * bi, ar * bi + ai * br


def _s5_scan_kernel(bre_ref, bim_ref, h0re_ref, h0im_ref, lre_ref, lim_ref, hre_ref, him_ref,
                    cre_ref, cim_ref, pwre_ref, pwim_ref, *, chained, cols):
    tb, width = bre_ref.shape
    row = lax.broadcasted_iota(jnp.int32, (S5_STEPS, cols), 0)

    pr, pi = lre_ref[...], lim_ref[...]
    pwre_ref[0:1] = pr
    pwim_ref[0:1] = pi
    for t in range(1, S5_STEPS):
        pr, pi = _cmul(pr, pi, lre_ref[...], lim_ref[...])
        pwre_ref[t:t + 1] = pr
        pwim_ref[t:t + 1] = pi

    if chained:
        @pl.when(pl.program_id(0) == 0)
        def _():
            cre_ref[...] = h0re_ref[...]
            cim_ref[...] = h0im_ref[...]

    def group(g, _):
        rows = pl.ds(pl.multiple_of(g * S5_STEPS, S5_STEPS), S5_STEPS)
        for c0 in range(0, width, cols):
            lanes = slice(c0, c0 + cols)
            xr = bre_ref[rows, lanes]
            xi = bim_ref[rows, lanes]
            d = 1
            while d < S5_STEPS:
                sr = jnp.where(row >= d, pltpu.roll(xr, d, axis=0), 0.0)
                si = jnp.where(row >= d, pltpu.roll(xi, d, axis=0), 0.0)
                mr, mi = _cmul(pwre_ref[d - 1:d, lanes], pwim_ref[d - 1:d, lanes], sr, si)
                xr, xi = xr + mr, xi + mi
                d *= 2
            if chained:
                h_r, h_i = cre_ref[:, lanes], cim_ref[:, lanes]
            else:
                h_r, h_i = h0re_ref[pl.ds(g, 1), lanes], h0im_ref[pl.ds(g, 1), lanes]
            er, ei = _cmul(pwre_ref[:, lanes], pwim_ref[:, lanes], h_r, h_i)
            xr, xi = xr + er, xi + ei
            hre_ref[rows, lanes] = xr
            him_ref[rows, lanes] = xi
            if chained:
                cre_ref[:, lanes] = xr[S5_STEPS - 1:]
                cim_ref[:, lanes] = xi[S5_STEPS - 1:]
        return 0

    lax.fori_loop(0, tb // S5_STEPS, group, 0)


def _s5_scan(bu_re, bu_im, h0_re, h0_im, lb_re, lb_im, tb, chained):
    m, width = bu_re.shape
    spec_x = pl.BlockSpec((tb, width), lambda i: (i, 0))
    h_rows = 1 if chained else tb // S5_STEPS
    spec_h = pl.BlockSpec((h_rows, width), (lambda i: (0, 0)) if chained else (lambda i: (i, 0)))
    spec_l = pl.BlockSpec((1, width), lambda i: (0, 0))
    sds = jax.ShapeDtypeStruct((m, width), F32)
    row = lambda a: a.astype(F32).reshape(-1, width)
    return pl.pallas_call(
        functools.partial(_s5_scan_kernel, chained=chained, cols=1024),
        grid=(m // tb,),
        in_specs=[spec_x, spec_x, spec_h, spec_h, spec_l, spec_l],
        out_specs=(spec_x, spec_x),
        out_shape=(sds, sds),
        scratch_shapes=[pltpu.VMEM((1, width), F32), pltpu.VMEM((1, width), F32),
                        pltpu.VMEM((S5_STEPS, width), F32), pltpu.VMEM((S5_STEPS, width), F32)],
        compiler_params=_cp(("arbitrary",)),
        name="s5_scan",
    )(bu_re, bu_im, row(h0_re), row(h0_im), row(lb_re), row(lb_im))


def _s5_c_kernel(hre_ref, him_ref, u_ref, wcre_ref, wcim_ref, d_ref, wglu_ref, o_ref):
    nc, kc, _ = wcre_ref.shape
    cols = []
    for c in range(nc):
        hr = hre_ref[:, c * kc:(c + 1) * kc].astype(BF16)
        hi = him_ref[:, c * kc:(c + 1) * kc].astype(BF16)
        cols.append(_dot(hr, wcre_ref[c]) - _dot(hi, wcim_ref[c]))
    y = jnp.concatenate(cols, axis=1) + d_ref[...] * u_ref[...]
    z = jax.nn.gelu(y)
    o_ref[...] = z * jax.nn.sigmoid(_dot(z.astype(BF16), wglu_ref[...]))


def _s5_c(h_re, h_im, u, wcre, wcim, d_skip, w_glu, tm):
    m, width = h_re.shape
    w = u.shape[1]
    full3 = lambda a: pl.BlockSpec(a.shape, lambda i: (0, 0, 0))
    return pl.pallas_call(
        _s5_c_kernel,
        grid=(m // tm,),
        in_specs=[pl.BlockSpec((tm, width), lambda i: (i, 0)),
                  pl.BlockSpec((tm, width), lambda i: (i, 0)),
                  pl.BlockSpec((tm, w), lambda i: (i, 0)),
                  full3(wcre), full3(wcim),
                  pl.BlockSpec((1, w), lambda i: (0, 0)),
                  pl.BlockSpec((w, w), lambda i: (0, 0))],
        out_specs=pl.BlockSpec((tm, w), lambda i: (i, 0)),
        out_shape=jax.ShapeDtypeStruct((m, w), F32),
        compiler_params=_cp(("parallel",)),
        name="s5_c",
    )(h_re, h_im, u, wcre, wcim, d_skip.astype(F32).reshape(1, w), w_glu)


def _blockdiag(w, nblk):
    g, a, b = w.shape
    w4 = w.reshape(g // nblk, nblk, a, b)
    eye = jnp.eye(nblk, dtype=w.dtype)
    return jnp.einsum('cgab,gh->cgahb', w4, eye).reshape(g // nblk, nblk * a, nblk * b)


def _rwkv_pre_kernel(p_ref, prev_ref, mu_ref, w0_ref, w2_ref, a0_ref, a2_ref, g2_ref, kk_ref, ka_ref,
                     r_o, lw_o, k_o, v_o, kk_o, kka_o, g_o, *, width, lora_w, lora_a, n_valid):
    nb, tlen, c = p_ref.shape
    p = p_ref[...]
    tidx = lax.broadcasted_iota(jnp.int32, (nb, tlen, c), 1)
    prev = jnp.where(tidx == 0, prev_ref[...], pltpu.roll(p, 1, axis=1))
    xm = (p + (prev - p) * mu_ref[...]).reshape(nb * tlen, c)
    r = xm[:, :width]
    k = xm[:, width:2 * width]
    v = xm[:, 2 * width:3 * width]
    o = 3 * width
    w_lo = xm[:, o:o + lora_w]
    a_lo = xm[:, o + lora_w:o + lora_w + lora_a]
    g_lo = xm[:, o + lora_w + lora_a:]
    wraw = -_softplus(-(w0_ref[...] + _dot(jnp.tanh(w_lo).astype(BF16), w2_ref[...]))) - 0.5
    a = jax.nn.sigmoid(a0_ref[...] + _dot(a_lo.astype(BF16), a2_ref[...]))
    g = _dot(jax.nn.sigmoid(g_lo).astype(BF16), g2_ref[...])
    kkr = k * kk_ref[...]
    nrm = jnp.sqrt(_segsum64(kkr * kkr, _seg_ones()))
    kk = kkr / jnp.maximum(nrm, 1e-12)
    row = pl.program_id(0) * (nb * tlen) + lax.broadcasted_iota(jnp.int32, (nb * tlen, 1), 0)
    keep = (row < n_valid).astype(F32)
    r_o[...] = r
    lw_o[...] = -jnp.exp(wraw) * keep
    k_o[...] = k * (1.0 + (a - 1.0) * ka_ref[...]) * keep
    v_o[...] = v * keep
    kk_o[...] = kk * keep
    kka_o[...] = kk * a * keep
    g_o[...] = g


def _rwkv_pre(p3, prev, mu, w0, w2, a0, a2, g2, k_k, k_a, nb, n_valid):
    nbt, tlen, c = p3.shape
    width = w0.shape[0]
    lora_w, lora_a = w2.shape[0], a2.shape[0]
    rows = nb * tlen
    row1 = lambda a: a.astype(F32).reshape(1, -1)
    full2 = lambda a: pl.BlockSpec(a.shape, lambda i: (0, 0))
    spec_o = pl.BlockSpec((rows, width), lambda i: (i, 0))
    sds = jax.ShapeDtypeStruct((nbt * tlen, width), F32)
    args = (mu.astype(F32).reshape(1, 1, c), row1(w0), w2, row1(a0), a2, g2, row1(k_k), row1(k_a))
    return pl.pallas_call(
        functools.partial(_rwkv_pre_kernel, width=width, lora_w=lora_w, lora_a=lora_a, n_valid=n_valid),
        grid=(nbt // nb,),
        in_specs=[pl.BlockSpec((nb, tlen, c), lambda i: (i, 0, 0)),
                  pl.BlockSpec((nb, 1, c), lambda i: (i, 0, 0)),
                  pl.BlockSpec((1, 1, c), lambda i: (0, 0, 0))] + [full2(a) for a in args[1:]],
        out_specs=(spec_o,) * 7,
        out_shape=(sds,) * 7,
        compiler_params=_cp(("parallel",)),
        name="rwkv_pre",
    )(p3, prev, *args)


def _rwkv_scan_kernel(r_ref, lw_ref, k_ref, v_ref, kk_ref, kka_ref, s0_ref, y_ref, sl_ref, s_ref, *, tc):
    tb = r_ref.shape[1]
    npair = r_ref.shape[2] // LANE
    n2 = 2 * tc
    lane = lax.broadcasted_iota(jnp.int32, (tc, LANE), 1)
    first = lane < HEAD64
    ri = lax.broadcasted_iota(jnp.int32, (n2, n2), 0)
    ci = lax.broadcasted_iota(jnp.int32, (n2, n2), 1)
    same = (ri // tc) == (ci // tc)
    strict = same & ((ci % tc) < (ri % tc))
    incl = same & ((ci % tc) <= (ri % tc))
    eye = (ri == ci).astype(F32)
    trow = lax.broadcasted_iota(jnp.int32, (tc, LANE), 0)
    shifts = [1 << b for b in range(max(tc - 1, 0).bit_length())]
    n_sq = max(int(math.ceil(math.log2(tc))) - 1, 0)

    def stack(x):
        return jnp.concatenate([jnp.where(first, x, 0.0), jnp.where(first, 0.0, x)], axis=0)

    @pl.when(pl.program_id(1) == 0)
    def _():
        for pr in range(npair):
            sa = jnp.concatenate([s0_ref[0, 2 * pr], jnp.zeros((HEAD64, HEAD64), F32)], axis=1)
            sb = jnp.concatenate([jnp.zeros((HEAD64, HEAD64), F32), s0_ref[0, 2 * pr + 1]], axis=1)
            s_ref[pr] = jnp.concatenate([sa, sb], axis=0)

    def prepare(rows, pr):
        lanes = slice(pr * LANE, (pr + 1) * LANE)
        lw = lw_ref[0, rows, lanes]
        c = lw
        for sh in shifts:
            c = c + jnp.where(trow >= sh, pltpu.roll(c, sh, axis=0), 0.0)
        c_last = c[tc - 1:tc, :]
        kk = kk_ref[0, rows, lanes]
        kka = kka_ref[0, rows, lanes]
        kx = k_ref[0, rows, lanes]
        e_neg = jnp.exp(-c)
        e_end = jnp.exp(c_last - c)
        ops = (-kk * jnp.exp(c - lw), r_ref[0, rows, lanes] * jnp.exp(c), kka * e_neg, kx * e_neg,
               kka * e_end, kx * e_end, v_ref[0, rows, lanes])
        return tuple(_split_bf16(stack(x)) for x in ops) + (jnp.exp(c_last),)

    def chunk(ci_, _):
        rows = pl.ds(pl.multiple_of(ci_ * tc, tc), tc)
        prs = range(npair)
        al, rt, bt, kt, bh, kh, vs, w_end = zip(*[prepare(rows, pr) for pr in prs])
        nab = [jnp.where(strict, _mm3(al[p], bt[p], _NT), 0.0) for p in prs]
        aak = [jnp.where(strict, _mm3(al[p], kt[p], _NT), 0.0) for p in prs]
        arb = [jnp.where(incl, _mm3(rt[p], bt[p], _NT), 0.0) for p in prs]
        ark = [jnp.where(incl, _mm3(rt[p], kt[p], _NT), 0.0) for p in prs]
        inv = [eye + nab[p] for p in prs]
        pw = nab
        for _ in range(n_sq):
            pw_s = [_split_bf16(x) for x in pw]
            pw = [_mm3(pw_s[p], pw_s[p], _NN) for p in prs]
            inv = [inv[p] + _mm3(_split_bf16(inv[p]), _split_bf16(pw[p]), _NN) for p in prs]
        s0 = [s_ref[p] for p in prs]
        s0_s = [_split_bf16(x) for x in s0]
        rhs = [_mm3(al[p], s0_s[p], _NT) + _mm3(_split_bf16(aak[p]), vs[p], _NN) for p in prs]
        u = [_split_bf16(_mm3(_split_bf16(inv[p]), _split_bf16(rhs[p]), _NN)) for p in prs]
        for p in prs:
            ys = (_mm3(rt[p], s0_s[p], _NT) + _mm3(_split_bf16(arb[p]), u[p], _NN)
                  + _mm3(_split_bf16(ark[p]), vs[p], _NN))
            y_ref[0, rows, p * LANE:(p + 1) * LANE] = ys[:tc] + ys[tc:]
            s_ref[p] = s0[p] * w_end[p] + _mm3(u[p], bh[p], _TN) + _mm3(vs[p], kh[p], _TN)
        return 0

    lax.fori_loop(0, tb // tc, chunk, 0)

    for pr in range(npair):
        s = s_ref[pr]
        sl_ref[0, 2 * pr] = s[:HEAD64, :HEAD64]
        sl_ref[0, 2 * pr + 1] = s[HEAD64:, HEAD64:]


def _rwkv_scan(r, lw, k, v, kk, kka, s0, tb, tc):
    bsz, length, width = r.shape
    spec_x = pl.BlockSpec((1, tb, width), lambda b, t: (b, t, 0))
    spec_s = pl.BlockSpec((1,) + s0.shape[1:], lambda b, t: (b, 0, 0, 0))
    return pl.pallas_call(
        functools.partial(_rwkv_scan_kernel, tc=tc),
        grid=(bsz, length // tb),
        in_specs=[spec_x] * 6 + [spec_s],
        out_specs=(spec_x, spec_s),
        out_shape=(jax.ShapeDtypeStruct((bsz, length, width), F32),
                   jax.ShapeDtypeStruct(s0.shape, F32)),
        scratch_shapes=[pltpu.VMEM((width // LANE, LANE, LANE), F32)],
        compiler_params=_cp(("parallel", "arbitrary")),
        name="rwkv_scan",
    )(r, lw, k, v, kk, kka, s0.astype(F32))


def _rwkv_steps_kernel(r_ref, lw_ref, k_ref, v_ref, kk_ref, kka_ref, s0_ref, y_ref, sl_ref):
    tlen = r_ref.shape[0]
    n = r_ref.shape[2]
    sl_ref[...] = s0_ref[...]
    for t in range(tlen):
        w = jnp.exp(lw_ref[t, 0])
        kk = kk_ref[t, 0]
        kka = kka_ref[t, 0]
        kx = k_ref[t, 0]
        rx = r_ref[t, 0]

        def row(i, _):
            s = sl_ref[0, i]
            sa = -jnp.sum(s * kk, axis=0, keepdims=True)
            s = s * w + sa * kka + v_ref[t, 0, pl.ds(i, 1), :] * kx
            sl_ref[0, i] = s
            y_ref[t, 0, pl.ds(i, 1), :] = jnp.sum(s * rx, axis=0, keepdims=True)
            return 0

        lax.fori_loop(0, n, row, 0, unroll=4)


def _rwkv_steps(r, lw, k, v, kk, kka, s0):
    tlen, nh, n, bsz = r.shape
    spec_x = pl.BlockSpec((tlen, 1, n, bsz), lambda h: (0, h, 0, 0))
    spec_s = pl.BlockSpec((1, n, n, bsz), lambda h: (h, 0, 0, 0))
    return pl.pallas_call(
        _rwkv_steps_kernel,
        grid=(nh,),
        in_specs=[spec_x] * 6 + [spec_s],
        out_specs=(spec_x, spec_s),
        out_shape=(jax.ShapeDtypeStruct(r.shape, F32), jax.ShapeDtypeStruct(s0.shape, F32)),
        compiler_params=_cp(("parallel",)),
        name="rwkv_steps",
    )(r, lw, k, v, kk, kka, s0)


def _rwkv_post_kernel(y_ref, r_ref, k_ref, v_ref, g_ref, lnw_ref, lnb_ref, rk_ref, o_ref):
    e128 = _seg_ones()
    y = y_ref[...]
    mean = _segsum64(y, e128) * (1.0 / HEAD64)
    d = y - mean
    var = _segsum64(d * d, e128) * (1.0 / HEAD64)
    yn = d * lax.rsqrt(var + RWKV_LN_EPS) * lnw_ref[...] + lnb_ref[...]
    bonus = _segsum64(r_ref[...] * k_ref[...] * rk_ref[...], e128) * v_ref[...]
    o_ref[...] = (yn + bonus) * g_ref[...]


def _rwkv_post(y, r, k, v, g, ln_w, ln_b, r_k, tm):
    m, w = y.shape
    spec_x = pl.BlockSpec((tm, w), lambda i: (i, 0))
    spec_p = pl.BlockSpec((1, w), lambda i: (0, 0))
    row1 = lambda a: a.astype(F32).reshape(1, w)
    return pl.pallas_call(
        _rwkv_post_kernel,
        grid=(m // tm,),
        in_specs=[spec_x] * 5 + [spec_p] * 3,
        out_specs=spec_x,
        out_shape=jax.ShapeDtypeStruct((m, w), F32),
        compiler_params=_cp(("parallel",)),
        name="rwkv_post",
    )(y, r, k, v, g, row1(ln_w), row1(ln_b), row1(r_k))


def _qknorm_kernel(x_ref, g_ref, o_ref):
    x = x_ref[...]
    x2 = x.reshape(-1, x.shape[-1])
    ms = _segsum64(x2 * x2, _seg_ones()) * (1.0 / HEAD64)
    o_ref[...] = (x2 * lax.rsqrt(ms + RMS_EPS) * g_ref[...]).reshape(x.shape)


def _qknorm_blocks(proj, first_blk, nblk, gain, tr):
    length = proj.shape[1]
    g = jnp.tile(gain.astype(F32), LANE // HEAD64).reshape(1, LANE)
    return pl.pallas_call(
        _qknorm_kernel,
        grid=(nblk, length // tr),
        in_specs=[pl.BlockSpec((1, tr, LANE), lambda h, i: (first_blk + h, i, 0)),
                  pl.BlockSpec((1, LANE), lambda h, i: (0, 0))],
        out_specs=pl.BlockSpec((1, tr, LANE), lambda h, i: (h, i, 0)),
        out_shape=jax.ShapeDtypeStruct((nblk, length, LANE), F32),
        compiler_params=_cp(("parallel", "parallel")),
        name="qknorm_blocks",
    )(proj, g)


def _qknorm_cols(proj, col_blk, width, gain, tr):
    m = proj.shape[0]
    g = jnp.tile(gain.astype(F32), width // HEAD64).reshape(1, width)
    return pl.pallas_call(
        _qknorm_kernel,
        grid=(m // tr,),
        in_specs=[pl.BlockSpec((tr, width), lambda i: (i, col_blk)),
                  pl.BlockSpec((1, width), lambda i: (0, 0))],
        out_specs=pl.BlockSpec((tr, width), lambda i: (i, 0)),
        out_shape=jax.ShapeDtypeStruct((m, width), F32),
        compiler_params=_cp(("parallel",)),
        name="qknorm_cols",
    )(proj, g)


def _sb_weights(z2, tri, mask, carry):
    return _sb_finish(z2, _sb_prepare(z2, tri, mask), mask, carry)


def _sb_prepare(z2, tri, mask):
    sign = jnp.uint32(0x80000000)
    neg_abs = lax.bitcast_convert_type(lax.bitcast_convert_type(z2, jnp.uint32) | sign, F32)
    n = jnp.maximum(z2, 0.0) + jnp.log2(1.0 + jnp.exp2(neg_abs))
    if mask is not None:
        n = jnp.where(mask, n, 0.0)
    sub = tri.shape[0]
    out = []
    for b in range(z2.shape[1] // sub - 1, -1, -1):
        nb = n[:, b * sub:(b + 1) * sub]
        out.append((_dot(nb.astype(BF16), tri), jnp.sum(nb, axis=-1, keepdims=True)))
    return out


def _sb_finish(z2, prepared, mask, carry):
    parts = []
    for rest, total in prepared:
        parts.append(carry - rest)
        carry = carry - total
    log2_rest = parts[0] if len(parts) == 1 else jnp.concatenate(parts[::-1], axis=1)
    att = jnp.exp2(z2 + log2_rest)
    if mask is not None:
        att = jnp.where(mask, att, 0.0)
    return att, carry


def _sb_block(qh, kblk, vblk, tri, mask, carry):
    att, carry = _sb_weights(_dot_nt(qh, kblk), tri, mask, carry)
    return _dot(att.astype(BF16), vblk), carry


def _later_tri(n):
    return (lax.broadcasted_iota(jnp.int32, (n, n), 0)
            >= lax.broadcasted_iota(jnp.int32, (n, n), 1)).astype(BF16)


def _stack_halves(q):
    first = lax.broadcasted_iota(jnp.int32, q.shape, 1) < HEAD64
    return jnp.concatenate([jnp.where(first, q, 0.0), jnp.where(first, 0.0, q)], axis=0)


def _key_minus_query(tq, tk):
    row = lax.broadcasted_iota(jnp.int32, (2 * tq, tk), 0)
    row = jnp.where(row >= tq, row - tq, row)
    return lax.broadcasted_iota(jnp.int32, (2 * tq, tk), 1) - row


def _sb_prompt_kernel(q_ref, k_ref, v_ref, o_ref, *, tq, tk, scale):
    qi = pl.program_id(1)
    qs = _stack_halves(q_ref[0] * scale).astype(BF16)
    tri = _later_tri(LANE)

    def block(kb, carry, masked):
        c, acc = carry
        rows = pl.ds(pl.multiple_of(kb * tk, tk), tk)
        kblk = k_ref[0, rows, :].astype(BF16)
        vblk = v_ref[0, rows, :].astype(BF16)
        mask = (_key_minus_query(tq, tk) < qi * tq - kb * tk) if masked else None
        d, c = _sb_block(qs, kblk, vblk, tri, mask, c)
        return c, acc + d

    carry = (jnp.zeros((2 * tq, 1), F32), jnp.zeros((2 * tq, LANE), F32))
    n_full = (qi * tq) // tk
    last = ((qi + 1) * tq - 1) // tk
    if tk % tq == 0:
        def diagonal(width, cr):
            c, acc = cr
            rows = pl.ds(pl.multiple_of(n_full * tk, tk), width)
            mask = _key_minus_query(tq, width) < qi * tq - n_full * tk
            d, c = _sb_block(qs, k_ref[0, rows, :].astype(BF16), v_ref[0, rows, :].astype(BF16), tri, mask, c)
            return c, acc + d

        carry = lax.switch((qi * tq - n_full * tk) // tq,
                           [functools.partial(diagonal, (n + 1) * tq) for n in range(tk // tq)], carry)
    else:
        carry = lax.fori_loop(0, last + 1 - n_full, lambda it, cr: block(last - it, cr, True), carry)

    def group(top, size, cr):
        c, acc = cr
        rows = [pl.ds(pl.multiple_of((top - s) * tk, tk), tk) for s in range(size)]
        zs = [_dot_nt(qs, k_ref[0, r, :].astype(BF16)) for r in rows]
        prepared = [_sb_prepare(z, tri, None) for z in zs]
        for z, prep, r in zip(zs, prepared, rows):
            att, c = _sb_finish(z, prep, None, c)
            acc = acc + _dot(att.astype(BF16), v_ref[0, r, :].astype(BF16))
        return c, acc

    top = n_full - 1
    for size in (1, 2):
        has = (n_full // size) % 2
        carry = lax.fori_loop(0, has, lambda it, cr, top=top, size=size: group(top, size, cr), carry)
        top = top - has * size
    carry = lax.fori_loop(0, n_full // 4, lambda it, cr, top=top: group(top - 4 * it, 4, cr), carry)
    acc = carry[1]
    first = lax.broadcasted_iota(jnp.int32, (tq, LANE), 1) < HEAD64
    o_ref[...] = jnp.where(first, acc[:tq], acc[tq:])


def _sb_prompt(proj, length, tq, tk):
    npair = 8
    return pl.pallas_call(
        functools.partial(_sb_prompt_kernel, tq=tq, tk=tk, scale=HEAD64 ** -0.5 * LOG2E),
        grid=(npair, length // tq),
        in_specs=[pl.BlockSpec((1, tq, LANE), lambda p, i: (p, i, 0)),
                  pl.BlockSpec((1, length, LANE), lambda p, i: (npair + p, 0, 0)),
                  pl.BlockSpec((1, length, LANE), lambda p, i: (2 * npair + p, 0, 0))],
        out_specs=pl.BlockSpec((tq, LANE), lambda p, i: (i, p)),
        out_shape=jax.ShapeDtypeStruct((length, npair * LANE), F32),
        compiler_params=_cp(("parallel", "parallel")),
        name="sb_prompt",
    )(proj, proj, proj)


def _diff_lambda(lqk_ref, lam_init):
    lqk = lqk_ref[...]
    s1 = jnp.sum(lqk[0:1] * lqk[1:2], axis=-1, keepdims=True)
    s2 = jnp.sum(lqk[2:3] * lqk[3:4], axis=-1, keepdims=True)
    return jnp.exp(s1) - jnp.exp(s2) + lam_init


def _softmax_block(qh, kblk, vblk, bias, state):
    m, l, acc = state
    s = _dot_nt(qh, kblk) + bias
    m_new = jnp.maximum(m, jnp.max(s, axis=-1, keepdims=True))
    p = jnp.exp2(s - m_new)
    corr = jnp.exp2(m - m_new)
    return (m_new, corr * l + jnp.sum(p, axis=-1, keepdims=True),
            corr * acc + _dot(p.astype(BF16), vblk))


def _diff_prompt_kernel(q_ref, k_ref, v_ref, slope_ref, lqk_ref, sub_ref, o_ref, alibi_ref, *,
                        tq, tk, scale, lam_init):
    qi = pl.program_id(1)
    qs = _stack_halves(q_ref[0] * scale).astype(BF16)
    slope = slope_ref[0][:, :1] * LOG2E

    @pl.when(qi == 0)
    def _():
        alibi_ref[...] = slope * _key_minus_query(tq, tk).astype(F32)

    def block(kb, state, masked):
        rows = pl.ds(pl.multiple_of(kb * tk, tk), tk)
        kblk = k_ref[0, rows, :].astype(BF16)
        vblk = v_ref[0, rows, :].astype(BF16)
        off = kb * tk - qi * tq
        bias = alibi_ref[...] + slope * off.astype(F32)
        if masked:
            bias = jnp.where(_key_minus_query(tq, tk) <= -off, bias, NEG_INF)
        return _softmax_block(qs, kblk, vblk, bias, state)

    state = (jnp.full((2 * tq, 1), NEG_INF, F32), jnp.zeros((2 * tq, 1), F32),
             jnp.zeros((2 * tq, LANE), F32))
    n_full = (qi * tq) // tk
    last = ((qi + 1) * tq - 1) // tk
    def group(kb0, size, st):
        m_old, l_old, acc = st
        rows = [pl.ds(pl.multiple_of((kb0 + b) * tk, tk), tk) for b in range(size)]
        ss = [_dot_nt(qs, k_ref[0, r, :].astype(BF16)) + alibi_ref[...]
              + slope * ((kb0 + b) * tk - qi * tq).astype(F32) for b, r in enumerate(rows)]
        m_new = m_old
        for s in ss:
            m_new = jnp.maximum(m_new, jnp.max(s, axis=-1, keepdims=True))
        corr = jnp.exp2(m_old - m_new)
        l_new = corr * l_old
        acc = corr * acc
        for s, r in zip(ss, rows):
            p = jnp.exp2(s - m_new)
            l_new = l_new + jnp.sum(p, axis=-1, keepdims=True)
            acc = acc + _dot(p.astype(BF16), v_ref[0, r, :].astype(BF16))
        return m_new, l_new, acc

    state = lax.fori_loop(0, n_full // 4, lambda it, st: group(4 * it, 4, st), state)
    done = n_full // 4 * 4
    for size in (2, 1):
        has = (n_full // size) % 2
        state = lax.fori_loop(0, has, lambda it, st, done=done, size=size: group(done, size, st), state)
        done = done + has * size
    if tk % tq == 0:
        def diagonal(width, st):
            rows = pl.ds(pl.multiple_of(n_full * tk, tk), width)
            off = n_full * tk - qi * tq
            bias = jnp.where(_key_minus_query(tq, width) <= -off,
                             alibi_ref[:, :width] + slope * off.astype(F32), NEG_INF)
            return _softmax_block(qs, k_ref[0, rows, :].astype(BF16), v_ref[0, rows, :].astype(BF16), bias, st)

        state = lax.switch((qi * tq - n_full * tk) // tq,
                           [functools.partial(diagonal, (n + 1) * tq) for n in range(tk // tq)], state)
    else:
        state = lax.fori_loop(n_full, last + 1, lambda kb, st: block(kb, st, True), state)
    _, l, acc = state
    lam = _diff_lambda(lqk_ref, lam_init)
    o = acc[:tq] / l[:tq] - lam * (acc[tq:] / l[tq:])
    o_ref[...] = _rms_rows(o, sub_ref[...]) * (1.0 - lam_init)


def _diff_slopes(nheads):
    return jnp.exp2(-8.0 * jnp.arange(1, nheads + 1, dtype=F32) / nheads)


def _diff_prompt(qn, kn, proj, v_blk0, slopes, lqk, subln, length, tq, tk, lam_init):
    nheads = qn.shape[0]
    return pl.pallas_call(
        functools.partial(_diff_prompt_kernel, tq=tq, tk=tk, scale=HEAD64 ** -0.5 * LOG2E, lam_init=lam_init),
        grid=(nheads, length // tq),
        in_specs=[pl.BlockSpec((1, tq, LANE), lambda h, i: (h, i, 0)),
                  pl.BlockSpec((1, length, LANE), lambda h, i: (h, 0, 0)),
                  pl.BlockSpec((1, length, LANE), lambda h, i: (v_blk0 + h, 0, 0)),
                  pl.BlockSpec((1, 1, LANE), lambda h, i: (h, 0, 0)),
                  pl.BlockSpec(lqk.shape, lambda h, i: (0, 0)),
                  pl.BlockSpec((1, LANE), lambda h, i: (0, 0))],
        out_specs=pl.BlockSpec((tq, LANE), lambda h, i: (i, h)),
        out_shape=jax.ShapeDtypeStruct((length, nheads * LANE), F32),
        scratch_shapes=[pltpu.VMEM((2 * tq, tk), F32)],
        compiler_params=_cp(("parallel", "arbitrary")),
        name="diff_prompt",
    )(qn, kn, proj, jnp.broadcast_to(slopes.reshape(nheads, 1, 1), (nheads, 1, LANE)), lqk,
      subln.astype(F32).reshape(1, LANE))


def _page_specs(page_table, pages_per_step, block):
    npages = page_table.shape[1]
    zeros = (0,) * (len(block) - 1)

    def spec(i):
        return pl.BlockSpec(block, lambda b, j, pt: (pt[b, npages - 1 - (j * pages_per_step + i)],) + zeros)

    return [spec(i) for i in range(pages_per_step)]


def _query_blockdiag(q, seg):
    tlen, w = q.shape
    nseg = w // seg
    rows = nseg * tlen
    qt = jnp.broadcast_to(q[None], (nseg, tlen, w)).reshape(rows, w)
    keep = (lax.broadcasted_iota(jnp.int32, (rows, w), 1) // seg
            == lax.broadcasted_iota(jnp.int32, (rows, w), 0) // tlen)
    return jnp.where(keep, qt, 0.0)


def _sb_sample_kernel(pt_ref, q_ref, knt_ref, vnt_ref, *refs, pages_per_step, scale):
    del pt_ref
    k_refs, v_refs = refs[:pages_per_step], refs[pages_per_step:2 * pages_per_step]
    o_ref, qbd_ref, kpad_ref, vpad_ref, acc_ref, car_ref = refs[2 * pages_per_step:]
    j = pl.program_id(1)
    tlen, w = q_ref.shape[1:]
    tnew = knt_ref.shape[2]
    page = kpad_ref.shape[1]
    rows = qbd_ref.shape[0]
    tri = _later_tri(page)

    @pl.when(j == 0)
    def _():
        qbd_ref[...] = _query_blockdiag(q_ref[0] * scale, HEAD64).astype(BF16)
        kpad_ref[...] = jnp.zeros(kpad_ref.shape, F32)
        vpad_ref[...] = jnp.zeros(vpad_ref.shape, F32)
        kpad_ref[:, :tnew] = knt_ref[0]
        vpad_ref[:, :tnew] = vnt_ref[0]
        t = lax.broadcasted_iota(jnp.int32, (rows, page), 0) % tlen
        s = lax.broadcasted_iota(jnp.int32, (rows, page), 1)
        z = _dot(qbd_ref[...], kpad_ref[...].astype(BF16))
        att, car_ref[...] = _sb_weights(z, tri, s < t, jnp.zeros(car_ref.shape, F32))
        acc_ref[...] = _dot_nt(att.astype(BF16), vpad_ref[...].astype(BF16))

    qbd = qbd_ref[...]
    zs = [_dot(qbd, k_ref[0].reshape(w, page).astype(BF16)) for k_ref in k_refs]
    prepared = [_sb_prepare(z, tri, None) for z in zs]
    car = car_ref[...]
    acc = acc_ref[...]
    for z, prep, v_ref in zip(zs, prepared, v_refs):
        att, car = _sb_finish(z, prep, None, car)
        acc = acc + _dot_nt(att.astype(BF16), v_ref[0].reshape(w, page).astype(BF16))
    car_ref[...] = car
    acc_ref[...] = acc

    @pl.when(j == pl.num_programs(1) - 1)
    def _():
        keep = (lax.broadcasted_iota(jnp.int32, (rows, w), 1) // HEAD64
                == lax.broadcasted_iota(jnp.int32, (rows, w), 0) // tlen)
        o_ref[0] = jnp.sum(jnp.where(keep, acc_ref[...], 0.0).reshape(rows // tlen, tlen, w), axis=0)


def _sb_sample(page_table, proj3, knt, vnt, cache_kt, cache_vt, pages_per_step):
    bsz, tlen, _ = proj3.shape
    nh, hd, page = cache_kt.shape[1:]
    w = nh * hd
    npages = page_table.shape[1]
    rows = nh * tlen
    seq = lambda shape: pl.BlockSpec((1,) + shape, lambda b, j, pt: (b, 0, 0))
    pages = _page_specs(page_table, pages_per_step, (1, nh, hd, page))
    grid_spec = pltpu.PrefetchScalarGridSpec(
        num_scalar_prefetch=1,
        grid=(bsz, npages // pages_per_step),
        in_specs=[seq((tlen, w)), seq((w, tlen)), seq((w, tlen))] + pages + pages,
        out_specs=seq((tlen, w)),
        scratch_shapes=[pltpu.VMEM((rows, w), BF16), pltpu.VMEM((w, page), F32), pltpu.VMEM((w, page), F32),
                        pltpu.VMEM((rows, w), F32), pltpu.VMEM((rows, 1), F32)])
    return pl.pallas_call(
        functools.partial(_sb_sample_kernel, pages_per_step=pages_per_step, scale=HEAD64 ** -0.5 * LOG2E),
        grid_spec=grid_spec,
        out_shape=jax.ShapeDtypeStruct((bsz, tlen, w), F32),
        compiler_params=_cp(("parallel", "arbitrary")),
        name="sb_sample",
    )(page_table, proj3, knt, vnt, *([cache_kt] * pages_per_step), *([cache_vt] * pages_per_step))


def _diff_sample_kernel(pt_ref, q_ref, knt_ref, vn_ref, slope_ref, lqk_ref, sub_ref, *refs,
                        pages_per_step, scale, lam_init, past_len):
    del pt_ref
    k_refs, v_refs = refs[:pages_per_step], refs[pages_per_step:2 * pages_per_step]
    o_ref, kpad_ref, vpad_ref, acc_ref, m_ref, l_ref = refs[2 * pages_per_step:]
    j = pl.program_id(1)
    nmap, tlen, _ = q_ref.shape[1:]
    nh = nmap // 2
    tnew = knt_ref.shape[3]
    page = kpad_ref.shape[2]
    npages = pl.num_programs(1) * pages_per_step
    rows = nmap * tlen
    t = lax.broadcasted_iota(jnp.int32, (rows, page), 0) % tlen
    s = lax.broadcasted_iota(jnp.int32, (rows, page), 1)
    slope = slope_ref[...] * LOG2E
    q = (q_ref[0] * scale).astype(BF16)

    wide = page * nh
    spread = (lax.broadcasted_iota(jnp.int32, (page, wide), 1) // nh
              == lax.broadcasted_iota(jnp.int32, (page, wide), 0)).astype(BF16)
    own_head = (lax.broadcasted_iota(jnp.int32, (rows, wide), 1) % nh
                == lax.broadcasted_iota(jnp.int32, (rows, wide), 0) // (2 * tlen))

    def visit(kts, v2s, biases, state):
        m_old, l_old, acc = state
        scs = [jnp.einsum('rtd,rds->rts', q, kt.astype(BF16), preferred_element_type=F32).reshape(rows, page)
               + bias for kt, bias in zip(kts, biases)]
        m_new = m_old
        for sc in scs:
            m_new = jnp.maximum(m_new, jnp.max(sc, axis=-1, keepdims=True))
        corr = jnp.exp2(m_old - m_new)
        l_new = corr * l_old
        acc = corr * acc
        for sc, v2 in zip(scs, v2s):
            p = jnp.exp2(sc - m_new)
            l_new = l_new + jnp.sum(p, axis=-1, keepdims=True)
            pw = jnp.where(own_head, _dot(p.astype(BF16), spread), 0.0).astype(BF16)
            acc = acc + _dot(pw, v2.reshape(wide, v2.shape[-1]).astype(BF16))
        return m_new, l_new, acc

    @pl.when(j == 0)
    def _():
        kpad_ref[...] = jnp.zeros(kpad_ref.shape, F32)
        vpad_ref[...] = jnp.zeros(vpad_ref.shape, F32)
        kpad_ref[:, :, :tnew] = knt_ref[0]
        vpad_ref[:tnew] = vn_ref[0]
        bias = jnp.where(s <= t, -slope * (t - s).astype(F32), NEG_INF)
        init = (jnp.full(m_ref.shape, NEG_INF, F32), jnp.zeros(l_ref.shape, F32),
                jnp.zeros(acc_ref.shape, F32))
        m_ref[...], l_ref[...], acc_ref[...] = visit([kpad_ref[...]], [vpad_ref[...]], [bias], init)

    biases = []
    for i in range(pages_per_step):
        kpos = (npages - 1 - (j * pages_per_step + i)) * page + s
        biases.append(-slope * (past_len + t - kpos).astype(F32))
    m_ref[...], l_ref[...], acc_ref[...] = visit([r[0] for r in k_refs], [r[0] for r in v_refs], biases,
                                                 (m_ref[...], l_ref[...], acc_ref[...]))

    @pl.when(j == pl.num_programs(1) - 1)
    def _():
        lam = _diff_lambda(lqk_ref, lam_init)
        o3 = (acc_ref[...] / l_ref[...]).reshape(nh, 2 * tlen, acc_ref.shape[1])
        for h in range(nh):
            o = o3[h, :tlen] - lam * o3[h, tlen:]
            o_ref[0, h] = _rms_rows(o, sub_ref[...]) * (1.0 - lam_init)


def _diff_sample(page_table, q4, knt, vn4, cache_kt, cache_v, slopes, lqk, subln, lam_init, pages_per_step):
    bsz, nmap, tlen, hd = q4.shape
    npages = page_table.shape[1]
    page = cache_kt.shape[3]
    nheads, dv = cache_v.shape[2], cache_v.shape[3]
    rows = nmap * tlen
    slope_rows = jnp.broadcast_to(jnp.repeat(slopes, rows // nheads)[:, None], (rows, page))
    seq = lambda a: pl.BlockSpec((1,) + a.shape[1:], lambda b, j, pt: (b, 0, 0, 0))
    const2 = lambda shape: pl.BlockSpec(shape, lambda b, j, pt: (0, 0))
    grid_spec = pltpu.PrefetchScalarGridSpec(
        num_scalar_prefetch=1,
        grid=(bsz, npages // pages_per_step),
        in_specs=([seq(q4), seq(knt), seq(vn4), const2((rows, page)), const2(lqk.shape), const2((1, dv))]
                  + _page_specs(page_table, pages_per_step, (1, nmap, hd, page))
                  + _page_specs(page_table, pages_per_step, (1, page, nheads, dv))),
        out_specs=pl.BlockSpec((1, nheads, tlen, dv), lambda b, j, pt: (b, 0, 0, 0)),
        scratch_shapes=[pltpu.VMEM((nmap, hd, page), F32), pltpu.VMEM((page, nheads, dv), F32),
                        pltpu.VMEM((rows, dv), F32),
                        pltpu.VMEM((rows, 1), F32), pltpu.VMEM((rows, 1), F32)])
    return pl.pallas_call(
        functools.partial(_diff_sample_kernel, pages_per_step=pages_per_step, scale=HEAD64 ** -0.5 * LOG2E,
                          lam_init=lam_init, past_len=npages * page),
        grid_spec=grid_spec,
        out_shape=jax.ShapeDtypeStruct((bsz, nheads, tlen, dv), F32),
        compiler_params=_cp(("parallel", "arbitrary")),
        name="diff_sample",
    )(page_table, q4, knt, vn4, slope_rows, lqk, subln.astype(F32).reshape(1, dv),
      *([cache_kt] * pages_per_step), *([cache_v] * pages_per_step))


def _largest_tile(m, cap, mult=16):
    t = min(m, cap)
    while m % t or t % mult:
        t -= mult
    return t


def _layer0(x, bsz, tlen, n_valid, s5_re0, s5_im0, rwkv_s0, shift_prev, wt):
    m = x.shape[0]
    tm = _largest_tile(m, 640)
    u = _proj(x, wt['norm_mix0'], wt['w_in0_u'], tm, 512)
    p = _proj(x, wt['norm_mix0'], wt['w_in0_p'], tm, 1152)

    bu_re, bu_im = _s5_b(u, wt['s5_wb_re'], wt['s5_wb_im'], tm)
    width = bu_re.shape[1]
    assert bsz == 1 or tlen == S5_STEPS
    h_re, h_im = _s5_scan(bu_re, bu_im, s5_re0, s5_im0, wt['s5_lb_re'], wt['s5_lb_im'],
                          LANE if bsz == 1 else _largest_tile(m, 256, 64), bsz == 1)
    last = lambda h: h.reshape(bsz, tlen, width)[:, n_valid - 1]
    f_re, f_im = last(h_re), last(h_im)
    ya = _s5_c(h_re, h_im, u, wt['s5_wc_re'], wt['s5_wc_im'],
               wt['s5_d'], wt['s5_w_glu'], _largest_tile(m, 256))

    c = p.shape[1]
    if bsz == 1:
        tr = _largest_tile(tlen, 320)
        prev = jnp.concatenate([shift_prev.astype(F32), p[tr - 1::tr][:-1]])[:, None, :]
        p3, nb = p.reshape(tlen // tr, tr, c), 1
    else:
        prev = shift_prev.astype(F32)[:, None, :]
        p3, nb = p.reshape(bsz, tlen, c), _largest_tile(bsz, 32, 1)
    r, lw, k, v, kk, kka, g = _rwkv_pre(p3, prev, wt['rwkv_mu'], wt['rwkv_w0'], wt['rwkv_w2'],
                                        wt['rwkv_a0'], wt['rwkv_a2'], wt['rwkv_g2'], wt['rwkv_k_k'],
                                        wt['rwkv_k_a'], nb, m if bsz > 1 else n_valid)
    w_r = r.shape[1]
    if bsz == 1:
        v3 = lambda a: a.reshape(bsz, tlen, w_r)
        y, s_last = _rwkv_scan(v3(r), v3(lw), v3(k), v3(v), v3(kk), v3(kka), rwkv_s0,
                               _largest_tile(tlen, 640, LANE), HEAD64)
        y = y.reshape(m, w_r)
    else:
        lanes_b = lambda a: jnp.transpose(a.reshape(bsz, tlen, w_r // HEAD64, HEAD64), (1, 2, 3, 0))
        y, s_last = _rwkv_steps(lanes_b(r), lanes_b(lw), lanes_b(k), lanes_b(v), lanes_b(kk), lanes_b(kka),
                                jnp.transpose(rwkv_s0.astype(F32), (1, 2, 3, 0)))
        y = jnp.transpose(y, (3, 0, 1, 2)).reshape(m, w_r)
        s_last = jnp.transpose(s_last, (3, 0, 1, 2))
    yb = _rwkv_post(y, r, k, v, g, wt['rwkv_ln_w'], wt['rwkv_ln_b'], wt['rwkv_r_k'], _largest_tile(m, 512))

    x = _outproj(x, ya, yb, wt['w_out0_a'], wt['w_out0_b'], tm, 512)
    x = _ffn(x, wt['norm_ffn0'], wt['ffn_w1'], wt['ffn_w3'], wt['ffn_w2'], tm, 512)
    return x, f_re, f_im, s_last, p


def kernel(x_prompt, x_sample, state_s5_re, state_s5_im, state_rwkv, state_rwkv_shift, cache_sb_k, cache_sb_v, cache_diff_k, cache_diff_v, page_table, meta_tokens, norm_mix0, w_in0, s5_lambda_re, s5_lambda_im, s5_log_dt, s5_b_re, s5_b_im, s5_c_re, s5_c_im, s5_d, s5_w_glu, rwkv_mu, rwkv_w0, rwkv_w2, rwkv_a0, rwkv_a2, rwkv_g2, rwkv_k_k, rwkv_k_a, rwkv_r_k, rwkv_ln_w, rwkv_ln_b, w_out0, norm_ffn0, ffn_w1, ffn_w3, ffn_w2, norm_mix1, w_in1, diff_q_norm, diff_k_norm, diff_lambda_q1, diff_lambda_k1, diff_lambda_q2, diff_lambda_k2, diff_subln, w_out1, norm_ffn1, moe_router, moe_w1, moe_w3, moe_w2):
    d_model = x_prompt.shape[-1]
    n_groups, n_state, grp = s5_b_re.shape
    s5_width = n_groups * grp
    rwkv_width = rwkv_w0.shape[0]
    rwkv_heads = rwkv_width // HEAD64
    sb_width = cache_sb_k.shape[2] * cache_sb_k.shape[3]
    diff_heads = cache_diff_v.shape[2]
    diff_qk_width = cache_diff_k.shape[2] * cache_diff_k.shape[3] * cache_diff_k.shape[4]
    diff_v_width = diff_heads * cache_diff_v.shape[3]
    lam_init = 0.8 - 0.6 * math.exp(-0.3)
    bf = lambda a: a.astype(BF16)

    lb_re, lb_im, bb_re, bb_im = _s5_discretize(s5_lambda_re, s5_lambda_im, s5_log_dt, s5_b_re, s5_b_im)
    to_ghp = lambda a: jnp.transpose(a, (1, 0, 2))
    wt = dict(
        norm_mix0=norm_mix0.astype(F32), w_in0_u=bf(w_in0[:, :s5_width]), w_in0_p=bf(w_in0[:, s5_width:]),
        s5_lb_re=lb_re, s5_lb_im=lb_im,
        s5_wb_re=bf(_blockdiag(to_ghp(bb_re), 8)), s5_wb_im=bf(_blockdiag(to_ghp(bb_im), 8)),
        s5_wc_re=bf(_blockdiag(jnp.transpose(s5_c_re.astype(F32), (0, 2, 1)), 16)),
        s5_wc_im=bf(_blockdiag(jnp.transpose(s5_c_im.astype(F32), (0, 2, 1)), 16)),
        s5_d=s5_d.reshape(-1), s5_w_glu=bf(s5_w_glu),
        rwkv_mu=rwkv_mu, rwkv_w0=rwkv_w0, rwkv_w2=bf(rwkv_w2), rwkv_a0=rwkv_a0, rwkv_a2=bf(rwkv_a2),
        rwkv_g2=bf(rwkv_g2), rwkv_k_k=rwkv_k_k, rwkv_k_a=rwkv_k_a, rwkv_r_k=rwkv_r_k,
        rwkv_ln_w=rwkv_ln_w, rwkv_ln_b=rwkv_ln_b,
        w_out0_a=bf(w_out0[:s5_width]), w_out0_b=bf(w_out0[s5_width:]),
        norm_ffn0=norm_ffn0.astype(F32), ffn_w1=bf(ffn_w1), ffn_w3=bf(ffn_w3), ffn_w2=bf(ffn_w2),
    )
    w_in1_b = bf(w_in1)
    w_out1_a, w_out1_b = bf(w_out1[:sb_width]), bf(w_out1[sb_width:])
    moe_w1_b, moe_w3_b, moe_w2_b = bf(moe_w1), bf(moe_w3), bf(moe_w2)
    lqk = jnp.stack([diff_lambda_q1, diff_lambda_k1, diff_lambda_q2, diff_lambda_k2]).astype(F32)
    slopes = _diff_slopes(diff_heads)
    q_col = 3 * sb_width // diff_qk_width
    npair = sb_width // LANE

    bp, seq, _ = x_prompt.shape
    assert bp == 1
    real = N_META + seq
    lp = real + (-real) % LANE
    xp = jnp.concatenate([meta_tokens.astype(F32), x_prompt[0].astype(F32), jnp.zeros((lp - real, d_model), F32)])
    tr_p = _largest_tile(lp, 640)
    tm_p = tr_p
    c_proj = w_in0.shape[1] - s5_width
    z_s5 = jnp.zeros((1, n_groups, n_state), F32)
    z_rwkv = jnp.zeros((1, rwkv_heads, HEAD64, HEAD64), F32)
    xh, p_s5_re, p_s5_im, p_rwkv, p_proj = _layer0(xp, 1, lp, real, z_s5, z_s5, z_rwkv,
                                                    jnp.zeros((1, c_proj), F32), wt)
    p_rwkv_shift = p_proj[real - 1:real]

    proj1 = _proj(xh, norm_mix1.astype(F32), w_in1_b, tm_p, 512, split=True)
    blk_dq = 3 * npair
    blk_dk = blk_dq + diff_heads
    blk_dv = blk_dk + diff_heads
    dqn = _qknorm_blocks(proj1, blk_dq, diff_heads, diff_q_norm, tr_p)
    dkn = _qknorm_blocks(proj1, blk_dk, diff_heads, diff_k_norm, tr_p)
    tq = _largest_tile(lp, 640, LANE)
    y_sb = _sb_prompt(proj1, lp, LANE, tq)
    y_d = _diff_prompt(dqn, dkn, proj1, blk_dv, slopes, lqk, diff_subln, lp, LANE, tq, lam_init)
    xh = _outproj(xh, y_sb, y_d, w_out1_a, w_out1_b, tm_p, 512)
    xh = _moe(xh, norm_ffn1.astype(F32), moe_router, moe_w1_b, moe_w3_b, moe_w2_b, tm_p, 256)
    y_prompt = xh[N_META:real][None]

    def tokens(blocks):
        return jnp.transpose(blocks[:, :real], (1, 0, 2)).reshape(1, real, -1)

    p_sb_k = tokens(proj1[npair:2 * npair]).reshape(1, real, sb_width // HEAD64, HEAD64)
    p_sb_v = tokens(proj1[2 * npair:3 * npair]).reshape(1, real, sb_width // HEAD64, HEAD64)
    p_diff_k = tokens(dkn).reshape(1, real, diff_heads, 2, HEAD64)
    p_diff_v = tokens(proj1[blk_dv:blk_dv + diff_heads]).reshape(1, real, diff_heads, diff_v_width // diff_heads)

    db, dseq, _ = x_sample.shape
    ms = db * dseq
    xs = x_sample.astype(F32).reshape(ms, d_model)
    xs, s_s5_re, s_s5_im, s_rwkv, s_proj = _layer0(xs, db, dseq, dseq, state_s5_re, state_s5_im, state_rwkv,
                                                   state_rwkv_shift, wt)
    s_rwkv_shift = s_proj.reshape(db, dseq, c_proj)[:, -1]

    tm_s = _largest_tile(ms, 640)
    proj_s = _proj(xs, norm_mix1.astype(F32), w_in1_b, tm_s, 512)
    dqn_s = _qknorm_cols(proj_s, q_col, diff_qk_width, diff_q_norm, tm_s)
    dkn_s = _qknorm_cols(proj_s, q_col + 1, diff_qk_width, diff_k_norm, tm_s)
    n_pool, page = cache_sb_k.shape[0], cache_sb_k.shape[1]
    pt = page_table.astype(jnp.int32)
    pages_per_step = next(n for n in (8, 4, 2, 1) if pt.shape[1] % n == 0)
    v_col = (3 * sb_width + 2 * diff_qk_width) // diff_v_width
    t_last = lambda a: jnp.transpose(a.reshape(db, dseq, -1), (0, 2, 1))
    slot_last = lambda c: jnp.transpose(c.reshape(n_pool, page, -1, HEAD64), (0, 2, 3, 1))
    ys_sb = _sb_sample(pt, proj_s.reshape(db, dseq, -1), t_last(proj_s[:, sb_width:2 * sb_width]),
                       t_last(proj_s[:, 2 * sb_width:3 * sb_width]), slot_last(cache_sb_k),
                       slot_last(cache_sb_v), pages_per_step)
    heads = lambda a: a.reshape(db, dseq, -1, HEAD64)
    ys_d = _diff_sample(pt, jnp.transpose(heads(dqn_s), (0, 2, 1, 3)), jnp.transpose(heads(dkn_s), (0, 2, 3, 1)),
                        proj_s[:, v_col * diff_v_width:].reshape(db, dseq, diff_heads, -1),
                        slot_last(cache_diff_k), cache_diff_v, slopes, lqk, diff_subln, lam_init,
                        pages_per_step)
    ys_d = jnp.transpose(ys_d, (0, 2, 1, 3)).reshape(ms, -1)
    xs = _outproj(xs, ys_sb.reshape(ms, -1), ys_d, w_out1_a, w_out1_b, tm_s, 512)
    xs = _moe(xs, norm_ffn1.astype(F32), moe_router, moe_w1_b, moe_w3_b, moe_w2_b, tm_s, 256)
    y_sample = xs.reshape(db, dseq, d_model)

    s_sb_k = proj_s[:, sb_width:2 * sb_width].reshape(db, dseq, sb_width // HEAD64, HEAD64)
    s_sb_v = proj_s[:, 2 * sb_width:3 * sb_width].reshape(db, dseq, sb_width // HEAD64, HEAD64)
    s_diff_k = dkn_s.reshape(db, dseq, diff_heads, 2, HEAD64)
    s_diff_v = proj_s[:, v_col * diff_v_width:].reshape(db, dseq, diff_heads, diff_v_width // diff_heads)

    return (y_prompt, y_sample,
            p_s5_re.reshape(1, n_groups, n_state), p_s5_im.reshape(1, n_groups, n_state), p_rwkv, p_rwkv_shift,
            p_sb_k, p_sb_v, p_diff_k, p_diff_v,
            s_s5_re.reshape(db, n_groups, n_state), s_s5_im.reshape(db, n_groups, n_state), s_rwkv, s_rwkv_shift,
            s_sb_k, s_sb_v, s_diff_k, s_diff_v)
```
